```python
import math
import jax
import jax.numpy as jnp
from jax import lax
import numpy as np

D_MODEL = 1024
BATCH = 32
SEQ = 2048
DEPTH = 1

PLE_DIM = 256
EPS = 1e-6
GDN_HEADS = 8
GDN_DK = 128
GDN_DV = 128
GDN_CONV = 4
GDN_CHUNK = 64
GDN_QK = GDN_HEADS * GDN_DK
GDN_V = GDN_HEADS * GDN_DV
FOX_HEADS = 16
FOX_HD = 64
FOX_W = FOX_HEADS * FOX_HD
FOX_QBLOCK = 128
N_EXPERTS = 64
TOP_K = 8
D_EXPERT = 256
D_SHARED = 256
ROUTED_SCALE = 2.5
MOE_TOKEN_BLOCK = 128
IN_SPLITS = (2 * GDN_QK + GDN_V, GDN_V, GDN_HEADS, GDN_HEADS, 3 * FOX_W, FOX_HEADS, 2 * D_MODEL)
D_IN = 2 * GDN_QK + 2 * GDN_V + 2 * GDN_HEADS + 3 * FOX_W + FOX_HEADS + 2 * D_MODEL

kernel_name = 'hybrid_gdn_fox_moe_block'


def rms_norm(x, gain):
    xf = x.astype(jnp.float32)
    y = xf * lax.rsqrt(jnp.mean(xf * xf, axis=-1, keepdims=True) + EPS)
    return (y * gain.astype(jnp.float32)).astype(x.dtype)


def l2_norm(x):
    xf = x.astype(jnp.float32)
    return xf * lax.rsqrt(jnp.sum(xf * xf, axis=-1, keepdims=True) + EPS)


def split_cols(t, sizes):
    out, off = [], 0
    for n in sizes:
        out.append(t[..., off:off + n])
        off += n
    return out


def causal_depthwise_conv(x, w):
    c = x.shape[-1]
    return lax.conv_general_dilated(
        x, w[:, None, :].astype(x.dtype), window_strides=(1,), padding=[(GDN_CONV - 1, 0)],
        dimension_numbers=('NWC', 'WIO', 'NWC'), feature_group_count=c)


def gated_delta_rule(q, k, v, log_alpha, beta):
    f32 = jnp.float32
    B, S, H, DK = q.shape
    DV = v.shape[-1]
    C = GDN_CHUNK
    N = S // C

    def chunks(t):
        t = t.astype(f32).reshape((B, N, C, H) + t.shape[3:])
        return jnp.moveaxis(t, 3, 1)

    q = chunks(q) * (DK ** -0.5)
    k = chunks(k)
    v = chunks(v)
    beta = chunks(beta)
    g = jnp.cumsum(chunks(log_alpha), axis=-1)
    causal = jnp.tril(jnp.ones((C, C), bool))
    strict = jnp.tril(jnp.ones((C, C), bool), -1)
    decay = jnp.exp(jnp.where(causal, g[..., :, None] - g[..., None, :], -jnp.inf))
    kk = jnp.einsum('bhnid,bhnjd->bhnij', k, k)
    lower = jnp.where(strict, beta[..., :, None] * kk * decay, 0.0)
    a_mat = lower + jnp.eye(C, dtype=f32)
    rhs = jnp.concatenate([v * beta[..., None], k * (beta * jnp.exp(g))[..., None]], axis=-1)
    sol = lax.linalg.triangular_solve(a_mat, rhs, left_side=True, lower=True, unit_diagonal=True)
    u, w = sol[..., :DV], sol[..., DV:]
    qk = jnp.where(causal, jnp.einsum('bhnid,bhnjd->bhnij', q, k) * decay, 0.0)
    g_last = g[..., -1]
    k_tail = k * jnp.exp(g_last[..., None] - g)[..., None]
    q_g = q * jnp.exp(g)[..., None]
    xs = tuple(jnp.moveaxis(t, 2, 0) for t in (q_g, qk, u, w, k_tail, g_last))

    def step(state, inp):
        q_c, qk_c, u_c, w_c, kt_c, gl_c = inp
        v_new = u_c - jnp.einsum('bhck,bhkv->bhcv', w_c, state)
        o_c = jnp.einsum('bhck,bhkv->bhcv', q_c, state) + jnp.einsum('bhij,bhjv->bhiv', qk_c, v_new)
        state = state * jnp.exp(gl_c)[..., None, None] + jnp.einsum('bhck,bhcv->bhkv', kt_c, v_new)
        return state, o_c

    s0 = jnp.zeros((B, H, DK, DV), f32)
    _, o = lax.scan(step, s0, xs)
    o = jnp.moveaxis(o, 0, 2)
    return jnp.moveaxis(o, 1, 3).reshape(B, S, H, DV)


def forgetting_attention(q, k, v, log_f):
    f32 = jnp.float32
    B, S, H, D = q.shape
    c = jnp.moveaxis(jnp.cumsum(log_f.astype(f32), axis=1), 1, 2)
    scale = D ** -0.5
    outs = []
    for blk in range(S // FOX_QBLOCK):
        q0, q1 = blk * FOX_QBLOCK, (blk + 1) * FOX_QBLOCK
        s = jnp.einsum('bqhd,bkhd->bhqk', q[:, q0:q1], k[:, :q1], preferred_element_type=f32) * scale
        s = s + c[:, :, q0:q1, None] - c[:, :, None, :q1]
        mask = (q0 + jnp.arange(FOX_QBLOCK))[:, None] >= jnp.arange(q1)[None, :]
        probs = jax.nn.softmax(jnp.where(mask, s, -jnp.inf), axis=-1)
        outs.append(jnp.einsum('bhqk,bkhd->bqhd', probs.astype(v.dtype), v[:, :q1]))
    return jnp.concatenate(outs, axis=1)


def routed_moe(hn, w_router, router_bias, we_gate, we_up, we_down, ws_gate, ws_up, ws_down):
    f32 = jnp.float32
    B, S, D = hn.shape
    t = hn.reshape(B * S, D)
    scores = jax.nn.sigmoid(jnp.dot(t, w_router, preferred_element_type=f32))
    _, idx = lax.top_k(scores + router_bias.astype(f32), TOP_K)
    sel = jnp.take_along_axis(scores, idx, axis=-1)
    wts = sel / jnp.sum(sel, axis=-1, keepdims=True) * ROUTED_SCALE
    gates = jnp.einsum('tk,tke->te', wts, jax.nn.one_hot(idx, N_EXPERTS, dtype=f32))
    nb = t.shape[0] // MOE_TOKEN_BLOCK

    def block(args):
        tb, gb = args
        a = jnp.einsum('td,edf->tef', tb, we_gate)
        b = jnp.einsum('td,edf->tef', tb, we_up)
        hid = jax.nn.silu(a) * b * gb[..., None].astype(tb.dtype)
        return jnp.einsum('tef,efd->td', hid, we_down)

    routed = lax.map(block, (t.reshape(nb, MOE_TOKEN_BLOCK, D), gates.reshape(nb, MOE_TOKEN_BLOCK, N_EXPERTS)))
    routed = routed.reshape(B, S, D)
    shared = jnp.einsum('bsf,fd->bsd', jax.nn.silu(jnp.einsum('bsd,df->bsf', hn, ws_gate)) * jnp.einsum('bsd,df->bsf', hn, ws_up), ws_down)
    return routed + shared


def setup_inputs(seed: int = 0) -> dict:
    key = jax.random.key(seed)
    ks = jax.random.split(key, 32)
    f32 = jnp.float32
    L = DEPTH

    def nrm(k, shape, fan_in):
        return jax.random.normal(k, shape, f32) * (fan_in ** -0.5)

    def gain(k, shape):
        return 1.0 + 0.05 * jax.random.normal(k, shape, f32)

    dt = jnp.exp(jax.random.uniform(ks[5], (L, GDN_HEADS), f32, math.log(1e-3), math.log(0.1)))
    return {
        'x': jax.random.normal(ks[0], (BATCH, SEQ, D_MODEL), f32),
        'p': jax.random.normal(ks[1], (DEPTH, BATCH, SEQ, PLE_DIM), f32),
        'norm_mix': gain(ks[2], (L, D_MODEL)),
        'w_in': nrm(ks[3], (L, D_MODEL, D_IN), D_MODEL),
        'conv_w': nrm(ks[4], (L, GDN_CONV, 2 * GDN_QK + GDN_V), GDN_CONV),
        'a_log': jnp.log(jax.random.uniform(ks[6], (L, GDN_HEADS), f32, 1.0, 16.0)),
        'dt_bias': dt + jnp.log(-jnp.expm1(-dt)),
        'gdn_norm': gain(ks[7], (L, GDN_DV)),
        'fox_f_bias': 3.0 + 0.5 * jax.random.normal(ks[8], (L, FOX_HEADS), f32),
        'q_norm': gain(ks[9], (L, FOX_HD)),
        'k_norm': gain(ks[10], (L, FOX_HD)),
        'w_branch_gdn': nrm(ks[11], (L, GDN_V, D_MODEL), GDN_V),
        'w_branch_fox': nrm(ks[12], (L, FOX_W, D_MODEL), FOX_W),
        'w_out': nrm(ks[13], (L, D_MODEL, D_MODEL), D_MODEL),
        'norm_ffn': gain(ks[14], (L, D_MODEL)),
        'w_router': nrm(ks[15], (L, D_MODEL, N_EXPERTS), D_MODEL),
        'router_bias': 0.01 * jax.random.normal(ks[16], (L, N_EXPERTS), f32),
        'we_gate': nrm(ks[17], (L, N_EXPERTS, D_MODEL, D_EXPERT), D_MODEL),
        'we_up': nrm(ks[18], (L, N_EXPERTS, D_MODEL, D_EXPERT), D_MODEL),
        'we_down': nrm(ks[19], (L, N_EXPERTS, D_EXPERT, D_MODEL), D_EXPERT),
        'ws_gate': nrm(ks[20], (L, D_MODEL, D_SHARED), D_MODEL),
        'ws_up': nrm(ks[21], (L, D_MODEL, D_SHARED), D_MODEL),
        'ws_down': nrm(ks[22], (L, D_SHARED, D_MODEL), D_SHARED),
        'w_ple': nrm(ks[23], (L, PLE_DIM, D_MODEL), PLE_DIM),
        'ple_norm': gain(ks[24], (L, D_MODEL)),
        'ple_gate_norm': gain(ks[25], (L, D_MODEL)),
        'w_ple_gate': nrm(ks[26], (L, D_MODEL, D_MODEL), D_MODEL),
    }


def reference(x, p, norm_mix, w_in, conv_w, a_log, dt_bias, gdn_norm, fox_f_bias, q_norm, k_norm,
              w_branch_gdn, w_branch_fox, w_out, norm_ffn, w_router, router_bias,
              we_gate, we_up, we_down, ws_gate, ws_up, ws_down,
              w_ple, ple_norm, ple_gate_norm, w_ple_gate):
    f32 = jnp.float32
    B, S, _ = x.shape
    h = x
    for i in range(DEPTH):
        xn = rms_norm(h, norm_mix[i])
        proj = jnp.einsum('bsd,de->bse', xn, w_in[i])
        gdn_qkv, gdn_z, gdn_a, gdn_b, fox_qkv, fox_f, merge = split_cols(proj, IN_SPLITS)

        qkv = jax.nn.silu(causal_depthwise_conv(gdn_qkv, conv_w[i]))
        q_a = l2_norm(qkv[..., :GDN_QK].reshape(B, S, GDN_HEADS, GDN_DK))
        k_a = l2_norm(qkv[..., GDN_QK:2 * GDN_QK].reshape(B, S, GDN_HEADS, GDN_DK))
        v_a = qkv[..., 2 * GDN_QK:].reshape(B, S, GDN_HEADS, GDN_DV)
        log_alpha = -jnp.exp(a_log[i].astype(f32)) * jax.nn.softplus(gdn_a.astype(f32) + dt_bias[i].astype(f32))
        beta = jax.nn.sigmoid(gdn_b.astype(f32))
        o_a = gated_delta_rule(q_a, k_a, v_a, log_alpha, beta)
        o_a = rms_norm(o_a, gdn_norm[i]) * jax.nn.silu(gdn_z.reshape(B, S, GDN_HEADS, GDN_DV).astype(f32))
        y_a = jnp.einsum('bse,ed->bsd', o_a.reshape(B, S, GDN_V).astype(h.dtype), w_branch_gdn[i])

        q_f = rms_norm(fox_qkv[..., :FOX_W].reshape(B, S, FOX_HEADS, FOX_HD), q_norm[i])
        k_f = rms_norm(fox_qkv[..., FOX_W:2 * FOX_W].reshape(B, S, FOX_HEADS, FOX_HD), k_norm[i])
        v_f = fox_qkv[..., 2 * FOX_W:].reshape(B, S, FOX_HEADS, FOX_HD)
        log_f = jax.nn.log_sigmoid(fox_f.astype(f32) + fox_f_bias[i].astype(f32))
        o_f = forgetting_attention(q_f, k_f, v_f, log_f)
        y_f = jnp.einsum('bse,ed->bsd', o_f.reshape(B, S, FOX_W).astype(h.dtype), w_branch_fox[i])

        gate_a = jax.nn.sigmoid(merge[..., :D_MODEL])
        gate_f = jax.nn.sigmoid(merge[..., D_MODEL:])
        h = h + jnp.einsum('bsd,de->bse', gate_a * y_a + gate_f * y_f, w_out[i])

        h = h + routed_moe(rms_norm(h, norm_ffn[i]), w_router[i], router_bias[i], we_gate[i], we_up[i],
                           we_down[i], ws_gate[i], ws_up[i], ws_down[i])

        e = rms_norm(jnp.einsum('bsp,pd->bsd', p[i], w_ple[i]), ple_norm[i])
        g = jax.nn.sigmoid(jnp.einsum('bsd,de->bse', rms_norm(h, ple_gate_norm[i]), w_ple_gate[i]))
        h = h + g * e
    return h
```

```python
import functools

import jax
import jax.numpy as jnp
from jax import lax
from jax.experimental import pallas as pl
from jax.experimental.pallas import tpu as pltpu

F32 = jnp.float32
BF16 = jnp.bfloat16

D_MODEL = 1024
PLE_DIM = 256
EPS = 1e-6
GDN_HEADS = 8
GDN_DK = 128
GDN_CONV = 4
GDN_CHUNK = 64
FOX_HEADS = 16
FOX_HD = 64
N_EXPERTS = 64
TOP_K = 8
D_EXPERT = 256
D_SHARED = 256
ROUTED_SCALE = 2.5

LANES = 128
COL_GQKV = 0
COL_GZ = 3072
COL_FQKV = 4096
COL_MERGE = 7168
N_BIG = 9216
LANE_A = 0
LANE_B = 8
LANE_F = 16

VMEM_LIMIT = 48 * 1024 * 1024


def _cparams(sem):
    return pltpu.CompilerParams(dimension_semantics=sem, vmem_limit_bytes=VMEM_LIMIT)


def _split3(x):
    h = x.astype(BF16)
    r = x - h.astype(F32)
    m = r.astype(BF16)
    l = (r - m.astype(F32)).astype(BF16)
    return h, m, l


def _dot(a, b, dims=(((1,), (0,)), ((), ()))):
    return lax.dot_general(a, b, dims, preferred_element_type=F32)


_NT = (((1,), (1,)), ((), ()))
_TN = (((0,), (0,)), ((), ()))


def _dot_xl(x, sel, dims=(((1,), (0,)), ((), ()))):
    h, m, l = _split3(x)
    return _dot(h, sel, dims) + _dot(m, sel, dims) + _dot(l, sel, dims)


def _dot_lx(sel, x, dims=(((1,), (0,)), ((), ()))):
    h, m, l = _split3(x)
    return _dot(sel, h, dims) + _dot(sel, m, dims) + _dot(sel, l, dims)


def _dot3(a, b):
    ah = a.astype(BF16)
    al = (a - ah.astype(F32)).astype(BF16)
    bh = b.astype(BF16)
    bl = (b - bh.astype(F32)).astype(BF16)
    return _dot(ah, bh) + _dot(ah, bl) + _dot(al, bh)


def _sigmoid(x):
    return 1.0 / (1.0 + jnp.exp(-x))


def _silu(x):
    return x * _sigmoid(x)


def _softplus(x):
    return jnp.maximum(x, 0.0) + jnp.log1p(jnp.exp(-jnp.abs(x)))


def _in_proj_kernel(x_ref, g_ref, wb_ref, ws_ref, ob_ref, os_ref, xn_ref):
    @pl.when(pl.program_id(1) == 0)
    def _():
        x = x_ref[...]
        ms = jnp.mean(x * x, axis=-1, keepdims=True)
        xn = (x * lax.rsqrt(ms + EPS) * g_ref[...]).astype(BF16)
        xn_ref[...] = xn
        os_ref[...] = _dot(xn, ws_ref[...])

    ob_ref[...] = _dot(xn_ref[...], wb_ref[...]).astype(ob_ref.dtype)


def _in_proj(x2, gain, w_big, w_small):
    T = x2.shape[0]
    tm = min(1024, T)
    tn = 1024
    return pl.pallas_call(
        _in_proj_kernel,
        grid=(T // tm, N_BIG // tn),
        in_specs=[
            pl.BlockSpec((tm, D_MODEL), lambda i, j: (i, 0)),
            pl.BlockSpec((1, D_MODEL), lambda i, j: (0, 0)),
            pl.BlockSpec((D_MODEL, tn), lambda i, j: (0, j)),
            pl.BlockSpec((D_MODEL, LANES), lambda i, j: (0, 0)),
        ],
        out_specs=[
            pl.BlockSpec((tm, tn), lambda i, j: (i, j)),
            pl.BlockSpec((tm, LANES), lambda i, j: (i, 0)),
        ],
        out_shape=[
            jax.ShapeDtypeStruct((T, N_BIG), BF16),
            jax.ShapeDtypeStruct((T, LANES), F32),
        ],
        scratch_shapes=[pltpu.VMEM((tm, D_MODEL), BF16)],
        compiler_params=_cparams(("parallel", "arbitrary")),
        name="in_proj",
    )(x2, gain, w_big, w_small)


def _gates_kernel(s_ref, alog_ref, bias_ref, o_ref, carry_ref, *, sb):
    @pl.when(pl.program_id(1) == 0)
    def _():
        carry_ref[...] = jnp.zeros_like(carry_ref)

    x = s_ref[...] + bias_ref[...]
    lane = lax.broadcasted_iota(jnp.int32, x.shape, 1)
    e = jnp.log1p(jnp.exp(-jnp.abs(x)))
    log_alpha = -jnp.exp(alog_ref[...]) * (jnp.maximum(x, 0.0) + e)
    beta = _sigmoid(x)
    log_f = -(jnp.maximum(-x, 0.0) + e)
    is_a = lane < LANE_B
    is_b = jnp.logical_and(lane >= LANE_B, lane < LANE_F)
    is_f = jnp.logical_and(lane >= LANE_F, lane < LANE_F + FOX_HEADS)
    val = jnp.where(is_a, log_alpha, jnp.where(is_f, log_f, 0.0))
    row = lax.broadcasted_iota(jnp.int32, (sb, sb), 0)
    col = lax.broadcasted_iota(jnp.int32, (sb, sb), 1)
    tri = row >= col
    tri_full = jnp.where(tri, 1.0, 0.0).astype(BF16)
    tri_chunk = jnp.where(jnp.logical_and(tri, row // GDN_CHUNK == col // GDN_CHUNK), 1.0, 0.0).astype(BF16)
    h, m, l = _split3(val)
    cs_full = _dot(tri_full, h) + _dot(tri_full, m) + _dot(tri_full, l) + carry_ref[0:1, :]
    cs_chunk = _dot(tri_chunk, h) + _dot(tri_chunk, m) + _dot(tri_chunk, l)
    o_ref[...] = jnp.where(is_a, cs_chunk, jnp.where(is_b, beta, jnp.where(is_f, cs_full, 0.0)))
    carry_ref[...] = jnp.broadcast_to(cs_full[sb - 1:sb, :], carry_ref.shape)


def _gates(small, alog_row, bias_row, B, S):
    sb = min(256, S)
    nsb = S // sb
    return pl.pallas_call(
        functools.partial(_gates_kernel, sb=sb),
        grid=(B, nsb),
        in_specs=[
            pl.BlockSpec((sb, LANES), lambda b, j: (b * nsb + j, 0)),
            pl.BlockSpec((1, LANES), lambda b, j: (0, 0)),
            pl.BlockSpec((1, LANES), lambda b, j: (0, 0)),
        ],
        out_specs=pl.BlockSpec((sb, LANES), lambda b, j: (b * nsb + j, 0)),
        out_shape=jax.ShapeDtypeStruct(small.shape, F32),
        scratch_shapes=[pltpu.VMEM((8, LANES), F32)],
        compiler_params=_cparams(("parallel", "arbitrary")),
        name="gates",
    )(small, alog_row, bias_row)


def _gdn_prep_kernel(x_ref, w_ref, o_ref, pad_ref, *, S, rb):
    j = pl.program_id(1)
    pad_ref[0:8, :] = jnp.zeros((8, LANES), F32)
    pad_ref[8:8 + S, :] = x_ref[...].astype(F32)
    is_qk = j < 2 * GDN_HEADS
    scale = jnp.where(j < GDN_HEADS, GDN_DK ** -0.5, 1.0).astype(F32)
    for r0 in range(0, S, rb):
        acc = jnp.zeros((rb, LANES), F32)
        for t in range(GDN_CONV):
            off = 8 - (GDN_CONV - 1) + t
            acc = acc + w_ref[t:t + 1, :] * pad_ref[r0 + off:r0 + off + rb, :]
        y = _silu(acc)
        n = y * lax.rsqrt(jnp.sum(y * y, axis=-1, keepdims=True) + EPS) * scale
        o_ref[r0:r0 + rb, :] = jnp.where(is_qk, n, y).astype(o_ref.dtype)


def _gdn_prep(proj, conv_w, B, S):
    T = B * S
    ncol = 3 * GDN_HEADS
    rb = min(256, S)
    return pl.pallas_call(
        functools.partial(_gdn_prep_kernel, S=S, rb=rb),
        grid=(B, ncol),
        in_specs=[
            pl.BlockSpec((S, LANES), lambda b, j: (b, COL_GQKV // LANES + j)),
            pl.BlockSpec((GDN_CONV, LANES), lambda b, j: (0, j)),
        ],
        out_specs=pl.BlockSpec((S, LANES), lambda b, j: (b, j)),
        out_shape=jax.ShapeDtypeStruct((T, ncol * LANES), BF16),
        scratch_shapes=[pltpu.VMEM((S + 8, LANES), F32)],
        compiler_params=_cparams(("parallel", "parallel")),
        name="gdn_prep",
    )(proj, conv_w)


_PAIR = 2 * GDN_CHUNK


def _inv_unit_lower(low, blk16, eye):
    x = jnp.where(blk16, -low, 0.0)
    off = jnp.where(blk16, 0.0, low)
    x2 = _dot3(x, x)
    x4 = _dot3(x2, x2)
    x8 = _dot3(x4, x4)
    dinv = _dot3(_dot3(_dot3(eye + x, eye + x2), eye + x4), eye + x8)
    n = _dot3(dinv, off)
    n2 = _dot3(n, n)
    return _dot3(_dot3(eye - n, eye + n2), dinv)


def _gdn_kernel(q_ref, k_ref, v_ref, ga_ref, z_ref, gn_ref, o_ref, kw_s, c_s, qp_s, oi_s, a_s, *, S):
    h = pl.program_id(1)
    P = _PAIR
    C = GDN_CHUNK
    row = lax.broadcasted_iota(jnp.int32, (P, P), 0)
    col = lax.broadcasted_iota(jnp.int32, (P, P), 1)
    same = row // C == col // C
    causal = jnp.logical_and(same, row >= col)
    strict = jnp.logical_and(same, row > col)
    blk16 = row // 16 == col // 16
    eye = jnp.where(row == col, 1.0, 0.0).astype(F32)
    sel_g = jnp.where(row == LANE_A + h, 1.0, 0.0).astype(BF16)
    sel_b = jnp.where(row == LANE_B + h, 1.0, 0.0).astype(BF16)
    sel_gr = jnp.where(col == LANE_A + h, 1.0, 0.0).astype(BF16)
    sel_last = jnp.where(col == (row // C) * C + (C - 1), 1.0, 0.0).astype(BF16)

    def prep(p, carry):
        r0 = pl.multiple_of(p * P, P)
        q = q_ref[pl.ds(r0, P), :].astype(F32)
        k = k_ref[pl.ds(r0, P), :].astype(F32)
        v = v_ref[pl.ds(r0, P), :].astype(F32)
        ga = ga_ref[pl.ds(r0, P), :]
        gcol = _dot_xl(ga, sel_g)
        bcol = _dot_xl(ga, sel_b)
        grow = _dot_lx(sel_gr, ga, _NT)
        glast = _dot_lx(sel_last, gcol)
        decay = jnp.where(causal, jnp.exp(jnp.where(causal, gcol - grow, 0.0)), 0.0)
        kb = k.astype(BF16)
        kk = _dot(kb, kb, _NT)
        qk = _dot(q.astype(BF16), kb, _NT) * decay
        low = jnp.where(strict, bcol * kk * decay, 0.0)
        tinv = _inv_unit_lower(low, blk16, eye)
        eg = jnp.exp(gcol)
        rhs = jnp.concatenate([v * bcol, k * (bcol * eg)], axis=1)
        sol = _dot3(tinv, rhs)
        u = sol[:, :LANES]
        w = sol[:, LANES:]
        ktail = (k * jnp.exp(glast - gcol)).astype(BF16)
        qw = _dot(qk.astype(BF16), sol.astype(BF16))
        qp_s[pl.ds(r0, P), :] = q * eg - qw[:, LANES:]
        oi_s[pl.ds(r0, P), :] = qw[:, :LANES]
        wb = w.astype(BF16)
        ub = u.astype(BF16)
        for half in range(2):
            rows = slice(half * C, (half + 1) * C)
            c = 2 * p + half
            kw_s[c] = _dot(ktail[rows], wb[rows], _TN)
            c_s[c] = _dot(ktail[rows], ub[rows], _TN)
            a_s[c] = jnp.broadcast_to(jnp.exp(glast[half * C:half * C + 1, :]), (8, LANES))
        return carry

    lax.fori_loop(0, S // P, prep, 0)

    def step(c, state):
        r0 = pl.multiple_of(c * C, C)
        sb = state.astype(BF16)
        o = _dot(qp_s[pl.ds(r0, C), :].astype(BF16), sb) + oi_s[pl.ds(r0, C), :]
        on = o * lax.rsqrt(jnp.mean(o * o, axis=-1, keepdims=True) + EPS) * gn_ref[...]
        z = z_ref[pl.ds(r0, C), :].astype(F32)
        o_ref[pl.ds(r0, C), :] = (on * _silu(z)).astype(o_ref.dtype)
        return a_s[c][0:1, :] * state - _dot(kw_s[c].astype(BF16), sb) + c_s[c]

    lax.fori_loop(0, S // C, step, jnp.zeros((GDN_DK, LANES), F32))


def _gdn(qkvn, gact, proj, gnorm, B, S):
    T = B * S
    nc = S // GDN_CHUNK
    blk = lambda off: pl.BlockSpec((S, LANES), lambda b, h: (b, off + h))
    return pl.pallas_call(
        functools.partial(_gdn_kernel, S=S),
        grid=(B, GDN_HEADS),
        in_specs=[
            blk(0), blk(GDN_HEADS), blk(2 * GDN_HEADS),
            pl.BlockSpec((S, LANES), lambda b, h: (b, 0)),
            pl.BlockSpec((S, LANES), lambda b, h: (b, COL_GZ // LANES + h)),
            pl.BlockSpec((1, LANES), lambda b, h: (0, 0)),
        ],
        out_specs=pl.BlockSpec((S, LANES), lambda b, h: (b, h)),
        out_shape=jax.ShapeDtypeStruct((T, GDN_HEADS * LANES), BF16),
        scratch_shapes=[
            pltpu.VMEM((nc, GDN_DK, LANES), F32),
            pltpu.VMEM((nc, GDN_DK, LANES), F32),
            pltpu.VMEM((S, LANES), F32),
            pltpu.VMEM((S, LANES), F32),
            pltpu.VMEM((nc, 8, LANES), F32),
        ],
        compiler_params=_cparams(("parallel", "parallel")),
        name="gdn",
    )(qkvn, qkvn, qkvn, gact, proj, gnorm)


def _fox_prep_kernel(x_ref, g_ref, o_ref):
    j = pl.program_id(1)
    x = x_ref[...].astype(F32)
    row = lax.broadcasted_iota(jnp.int32, (LANES, LANES), 0)
    col = lax.broadcasted_iota(jnp.int32, (LANES, LANES), 1)
    grp = jnp.where(row // FOX_HD == col // FOX_HD, 1.0, 0.0).astype(BF16)
    xx = x * x
    hi = xx.astype(BF16)
    lo = (xx - hi.astype(F32)).astype(BF16)
    ms = (_dot(hi, grp) + _dot(lo, grp)) * (1.0 / FOX_HD)
    scale = jnp.where(j < FOX_HEADS // 2, FOX_HD ** -0.5, 1.0).astype(F32)
    o_ref[...] = (x * lax.rsqrt(ms + EPS) * (g_ref[0] * scale)).astype(o_ref.dtype)


def _fox_prep(proj, gains, T):
    tm = min(1024, T)
    ncol = 2 * FOX_HEADS * FOX_HD // LANES
    return pl.pallas_call(
        _fox_prep_kernel,
        grid=(T // tm, ncol),
        in_specs=[
            pl.BlockSpec((tm, LANES), lambda i, j: (i, COL_FQKV // LANES + j)),
            pl.BlockSpec((1, 1, LANES), lambda i, j: (j, 0, 0)),
        ],
        out_specs=pl.BlockSpec((tm, LANES), lambda i, j: (i, j)),
        out_shape=jax.ShapeDtypeStruct((T, ncol * LANES), BF16),
        compiler_params=_cparams(("parallel", "parallel")),
        name="fox_prep",
    )(proj, gains)


def _fox_kernel(q_ref, k_ref, v_ref, gq_ref, gk_ref, o_ref, crow_s, *, S, tq):
    hp = pl.program_id(1)
    i = pl.program_id(2)
    nkb = S // tq
    lane_sq = lax.broadcasted_iota(jnp.int32, (LANES, LANES), 0)
    lane8 = lax.broadcasted_iota(jnp.int32, (8, LANES), 1)

    @pl.when(i == 0)
    def _():
        for hh in range(2):
            sel_r = jnp.where(lane8 == LANE_F + 2 * hp + hh, 1.0, 0.0).astype(BF16)
            for jb in range(nkb):
                crow_s[hh, jb] = _dot_lx(sel_r, gk_ref[jb * tq:(jb + 1) * tq, :], _NT)

    q = q_ref[...]
    gq = gq_ref[...]
    lane_q = lax.broadcasted_iota(jnp.int32, (tq, LANES), 1)
    rowi = lax.broadcasted_iota(jnp.int32, (tq, tq), 0)
    coli = lax.broadcasted_iota(jnp.int32, (tq, tq), 1)
    tri = rowi >= coli
    outs = []
    for hh in range(2):
        in_head = (lane_q // FOX_HD) == hh
        qh = jnp.where(in_head, q, jnp.zeros_like(q))
        sel_c = jnp.where(lane_sq == LANE_F + 2 * hp + hh, 1.0, 0.0).astype(BF16)
        ccol = _dot_xl(gq, sel_c)
        cq = jnp.concatenate([ccol] * (tq // LANES), axis=1)

        def kv_step(jb, carry, masked):
            m, l, acc = carry
            r0 = pl.multiple_of(jb * tq, tq)
            kb = k_ref[pl.ds(r0, tq), :]
            vb = v_ref[pl.ds(r0, tq), :]
            s = _dot(qh, kb, _NT) + (cq - crow_s[hh, jb][0:1, :])
            if masked:
                s = jnp.where(tri, s, -jnp.inf)
            m_new = jnp.maximum(m, jnp.max(s, axis=1, keepdims=True))
            pr = jnp.exp(s - m_new)
            alpha = jnp.exp(m - m_new)
            l = alpha * l + jnp.sum(pr, axis=1, keepdims=True)
            acc = alpha * acc + _dot(pr.astype(BF16), vb)
            return m_new, l, acc

        init = (jnp.full((tq, 1), -jnp.inf, F32), jnp.zeros((tq, 1), F32), jnp.zeros((tq, LANES), F32))
        carry = lax.fori_loop(0, i, lambda jb, c: kv_step(jb, c, False), init)
        m, l, acc = kv_step(i, carry, True)
        outs.append(acc / l)
    o_ref[...] = jnp.where((lane_q // FOX_HD) == 0, outs[0], outs[1]).astype(o_ref.dtype)


def _fox(qkn, proj, gact, B, S):
    T = B * S
    tq = min(256, S)
    nq = S // tq
    nhp = FOX_HEADS // 2
    return pl.pallas_call(
        functools.partial(_fox_kernel, S=S, tq=tq),
        grid=(B, nhp, nq),
        in_specs=[
            pl.BlockSpec((tq, LANES), lambda b, h, i: (b * nq + i, h)),
            pl.BlockSpec((S, LANES), lambda b, h, i: (b, nhp + h)),
            pl.BlockSpec((S, LANES), lambda b, h, i: (b, COL_FQKV // LANES + 2 * nhp + h)),
            pl.BlockSpec((tq, LANES), lambda b, h, i: (b * nq + i, 0)),
            pl.BlockSpec((S, LANES), lambda b, h, i: (b, 0)),
        ],
        out_specs=pl.BlockSpec((tq, LANES), lambda b, h, i: (b * nq + i, h)),
        out_shape=jax.ShapeDtypeStruct((T, nhp * LANES), BF16),
        scratch_shapes=[pltpu.VMEM((2, nq, 8, tq), F32)],
        compiler_params=_cparams(("parallel", "parallel", "arbitrary")),
        name="fox",
    )(qkn, qkn, proj, gact, gact)


HALF = D_MODEL // 2


def _rms(x, gain):
    return x * lax.rsqrt(jnp.mean(x * x, axis=-1, keepdims=True) + EPS) * gain


def _pack_halves(x):
    lo = lax.bitcast_convert_type(x[:, :HALF].astype(BF16).astype(F32), jnp.uint32)
    hi = lax.bitcast_convert_type(x[:, HALF:].astype(BF16).astype(F32), jnp.uint32)
    return lax.shift_right_logical(lo, jnp.uint32(16)) | (hi & jnp.uint32(0xFFFF0000))


def _unpack_halves(w):
    lo = lax.bitcast_convert_type(lax.shift_left(w, jnp.uint32(16)), F32)
    hi = lax.bitcast_convert_type(w & jnp.uint32(0xFFFF0000), F32)
    return lo, hi


def _mix_kernel(oa_ref, of_ref, ma_ref, mf_ref, x_ref, wa_ref, wf_ref, wo_ref, g_ref, h_ref, hp_ref):
    ya = _dot(oa_ref[...], wa_ref[...])
    yf = _dot(of_ref[...], wf_ref[...])
    m = _sigmoid(ma_ref[...].astype(F32)) * ya + _sigmoid(mf_ref[...].astype(F32)) * yf
    h = x_ref[...] + _dot(m.astype(BF16), wo_ref[...])
    h_ref[...] = h
    hp_ref[...] = _pack_halves(_rms(h, g_ref[...]))


def _mix(o_a, o_f, proj, x2, wa, wf, wo, gain):
    T = x2.shape[0]
    tm = min(512, T)
    row = lambda c: pl.BlockSpec((tm, D_MODEL), lambda i: (i, c))
    full = pl.BlockSpec((D_MODEL, D_MODEL), lambda i: (0, 0))
    return pl.pallas_call(
        _mix_kernel,
        grid=(T // tm,),
        in_specs=[row(0), row(0), row(COL_MERGE // D_MODEL), row(COL_MERGE // D_MODEL + 1), row(0),
                  full, full, full, pl.BlockSpec((1, D_MODEL), lambda i: (0, 0))],
        out_specs=[row(0), pl.BlockSpec((tm, HALF), lambda i: (i, 0))],
        out_shape=[jax.ShapeDtypeStruct((T, D_MODEL), F32), jax.ShapeDtypeStruct((T, HALF), jnp.uint32)],
        compiler_params=_cparams(("parallel",)),
        name="mix_out",
    )(o_a, o_f, proj, proj, x2, wa, wf, wo, gain)


def _router_kernel(h_ref, g_ref, wr_ref, bias_ref, gates_ref, sel_ref, cnt_ref):
    @pl.when(pl.program_id(0) == 0)
    def _():
        cnt_ref[...] = jnp.zeros_like(cnt_ref)

    hn = _rms(h_ref[...], g_ref[...])
    hh = hn.astype(BF16)
    hl = (hn - hh.astype(F32)).astype(BF16)
    logits = _dot(wr_ref[0], hh, _NT) + _dot(wr_ref[0], hl, _NT) + _dot(wr_ref[1], hh, _NT)
    scores = _sigmoid(logits)
    work = scores + bias_ref[...]
    eidx = lax.broadcasted_iota(jnp.int32, work.shape, 0)
    sel = jnp.zeros(work.shape, F32)
    for _ in range(TOP_K):
        mx = jnp.max(work, axis=0, keepdims=True)
        first = jnp.min(jnp.where(work == mx, eidx, N_EXPERTS), axis=0, keepdims=True)
        onehot = eidx == first
        sel = jnp.where(onehot, 1.0, sel)
        work = jnp.where(onehot, -jnp.inf, work)
    s = jnp.where(sel > 0.0, scores, 0.0)
    gates_ref[...] = s / jnp.sum(s, axis=0, keepdims=True) * ROUTED_SCALE
    sel_ref[...] = sel
    cnt_ref[...] += jnp.broadcast_to(jnp.sum(sel, axis=1, keepdims=True), cnt_ref.shape)


def _router(h1, gain, wr2, bias_col):
    T = h1.shape[0]
    tm = min(512, T)
    return pl.pallas_call(
        _router_kernel,
        grid=(T // tm,),
        in_specs=[
            pl.BlockSpec((tm, D_MODEL), lambda i: (i, 0)),
            pl.BlockSpec((1, D_MODEL), lambda i: (0, 0)),
            pl.BlockSpec((2, N_EXPERTS, D_MODEL), lambda i: (0, 0, 0)),
            pl.BlockSpec((N_EXPERTS, 1), lambda i: (0, 0)),
        ],
        out_specs=[
            pl.BlockSpec((N_EXPERTS, tm), lambda i: (0, i)),
            pl.BlockSpec((N_EXPERTS, tm), lambda i: (0, i)),
            pl.BlockSpec((N_EXPERTS, LANES), lambda i: (0, 0)),
        ],
        out_shape=[
            jax.ShapeDtypeStruct((N_EXPERTS, T), F32),
            jax.ShapeDtypeStruct((N_EXPERTS, T), F32),
            jax.ShapeDtypeStruct((N_EXPERTS, LANES), F32),
        ],
        compiler_params=_cparams(("arbitrary",)),
        name="router",
    )(h1, gain, wr2, bias_col)


def _pos_kernel(sel_ref, gates_ref, cnt_ref, pos_ref, w_ref, base_s, carry_s, *, tm, tme):
    @pl.when(pl.program_id(0) == 0)
    def _():
        padded = jnp.floor((cnt_ref[...] + (tme - 1)) * (1.0 / tme)) * tme
        r = lax.broadcasted_iota(jnp.int32, (N_EXPERTS, N_EXPERTS), 0)
        c = lax.broadcasted_iota(jnp.int32, (N_EXPERTS, N_EXPERTS), 1)
        base_s[...] = _dot_lx(jnp.where(r > c, 1.0, 0.0).astype(BF16), padded)
        carry_s[...] = jnp.zeros_like(carry_s)

    sel = sel_ref[...]
    gates = gates_ref[...]
    r = lax.broadcasted_iota(jnp.int32, (tm, tm), 0)
    c = lax.broadcasted_iota(jnp.int32, (tm, tm), 1)
    cs = _dot(sel.astype(BF16), jnp.where(r <= c, 1.0, 0.0).astype(BF16))
    start = base_s[...] + carry_s[...]
    posd = cs - sel + jnp.concatenate([start] * (tm // LANES), axis=1)
    carry_s[...] += jnp.broadcast_to(jnp.sum(sel, axis=1, keepdims=True), carry_s.shape)
    eidx = lax.broadcasted_iota(jnp.int32, sel.shape, 0)
    rem = sel
    ps, ws = [], []
    for _ in range(TOP_K):
        first = jnp.min(jnp.where(rem > 0.0, eidx, N_EXPERTS), axis=0, keepdims=True)
        onehot = eidx == first
        ps.append(jnp.sum(jnp.where(onehot, posd, 0.0), axis=0, keepdims=True))
        ws.append(jnp.sum(jnp.where(onehot, gates, 0.0), axis=0, keepdims=True))
        rem = jnp.where(onehot, 0.0, rem)
    pos_ref[...] = jnp.concatenate(ps, axis=0).astype(jnp.int32)
    w_ref[...] = jnp.concatenate(ws, axis=0)


def _positions(sel, gates, cnt, tme):
    T = sel.shape[1]
    tm = min(512, T)
    blk = pl.BlockSpec((N_EXPERTS, tm), lambda i: (0, i))
    return pl.pallas_call(
        functools.partial(_pos_kernel, tm=tm, tme=tme),
        grid=(T // tm,),
        in_specs=[blk, blk, pl.BlockSpec((N_EXPERTS, LANES), lambda i: (0, 0))],
        out_specs=[pl.BlockSpec((TOP_K, tm), lambda i: (0, i)), pl.BlockSpec((TOP_K, tm), lambda i: (0, i))],
        out_shape=[jax.ShapeDtypeStruct((TOP_K, T), jnp.int32), jax.ShapeDtypeStruct((TOP_K, T), F32)],
        scratch_shapes=[pltpu.VMEM((N_EXPERTS, LANES), F32), pltpu.VMEM((N_EXPERTS, LANES), F32)],
        compiler_params=_cparams(("arbitrary",)),
        name="moe_positions",
    )(sel, gates, cnt)


def _dispatch_kernel(meta_ref, pos_ref, hp_ref, xs_ref, zero_s, sem, zsem, *, tmd, tme, n_tiles):
    i = pl.program_id(0)

    @pl.when(i == 0)
    def _():
        zero_s[...] = jnp.zeros_like(zero_s)
        for e in range(N_EXPERTS):
            pltpu.make_async_copy(zero_s, xs_ref.at[pl.ds(pl.multiple_of(meta_ref[e], 8), tme)], zsem).start()
        for e in range(N_EXPERTS):
            pltpu.make_async_copy(zero_s, xs_ref.at[pl.ds(pl.multiple_of(meta_ref[e], 8), tme)], zsem).wait()

        def zero_tile(t, carry):
            cp = pltpu.make_async_copy(zero_s, xs_ref.at[pl.ds(pl.multiple_of(t * tme, tme), tme)], zsem)
            cp.start()
            cp.wait()
            return carry

        lax.fori_loop(meta_ref[N_EXPERTS], n_tiles, zero_tile, 0)

    def issue(t, carry):
        src = hp_ref.at[pl.ds(i * tmd + t, 1)]
        for k in range(TOP_K):
            pltpu.make_async_copy(src, xs_ref.at[pl.ds(pos_ref[0, 0, k * tmd + t], 1)], sem).start()
        return carry

    lax.fori_loop(0, tmd, issue, 0)
    pltpu.make_async_copy(xs_ref.at[pl.ds(0, TOP_K * tmd)], xs_ref.at[pl.ds(0, TOP_K * tmd)], sem).wait()


def _dispatch(meta, pos_tiles, hp, n_tiles, tmd, tme):
    T = hp.shape[0]
    return pl.pallas_call(
        functools.partial(_dispatch_kernel, tmd=tmd, tme=tme, n_tiles=n_tiles),
        grid_spec=pltpu.PrefetchScalarGridSpec(
            num_scalar_prefetch=1,
            grid=(T // tmd,),
            in_specs=[
                pl.BlockSpec((1, 1, TOP_K * tmd), lambda i, f: (i, 0, 0), memory_space=pltpu.SMEM),
                pl.BlockSpec(memory_space=pl.ANY),
            ],
            out_specs=pl.BlockSpec(memory_space=pl.ANY),
            scratch_shapes=[pltpu.VMEM((tme, HALF), jnp.uint32), pltpu.SemaphoreType.DMA, pltpu.SemaphoreType.DMA],
        ),
        out_shape=jax.ShapeDtypeStruct((n_tiles * tme, HALF), jnp.uint32),
        compiler_params=_cparams(("arbitrary",)),
        name="moe_dispatch",
    )(meta, pos_tiles, hp)


def _expert_kernel(te_ref, nu_ref, xs_ref, wgu_ref, wd_ref, ys_ref):
    @pl.when(pl.program_id(0) < nu_ref[0])
    def _():
        lo, hi = _unpack_halves(xs_ref[...])
        gu = _dot(lo.astype(BF16), wgu_ref[0, :HALF, :]) + _dot(hi.astype(BF16), wgu_ref[0, HALF:, :])
        hid = _silu(gu[:, :D_EXPERT]) * gu[:, D_EXPERT:]
        ys_ref[...] = _pack_halves(_dot(hid.astype(BF16), wd_ref[0]))

    @pl.when(pl.program_id(0) >= nu_ref[0])
    def _():
        ys_ref[...] = jnp.zeros_like(ys_ref)


def _experts(tile_e, n_used, xs, wgu, wd, n_tiles, tme):
    clamp = lambda i, te, nu: (jnp.minimum(i, nu[0] - 1), 0)
    return pl.pallas_call(
        _expert_kernel,
        grid_spec=pltpu.PrefetchScalarGridSpec(
            num_scalar_prefetch=2,
            grid=(n_tiles,),
            in_specs=[
                pl.BlockSpec((tme, HALF), clamp),
                pl.BlockSpec((1, D_MODEL, 2 * D_EXPERT), lambda i, te, nu: (te[i], 0, 0)),
                pl.BlockSpec((1, D_EXPERT, D_MODEL), lambda i, te, nu: (te[i], 0, 0)),
            ],
            out_specs=pl.BlockSpec((tme, HALF), lambda i, te, nu: (i, 0)),
        ),
        out_shape=jax.ShapeDtypeStruct((n_tiles * tme, HALF), jnp.uint32),
        compiler_params=_cparams(("arbitrary",)),
        name="moe_experts",
    )(tile_e, n_used, xs, wgu, wd)


def _tail_kernel(pos_ref, ys_ref, h_ref, w_ref, p_ref, gf_ref, wsgu_ref, wsd_ref, wple_ref, gple_ref, gpg_ref,
                 wpg_ref, o_ref, buf, sem, *, tmc):
    def issue(t, carry):
        for k in range(TOP_K):
            pltpu.make_async_copy(ys_ref.at[pl.ds(pos_ref[0, 0, k * tmc + t], 1)], buf.at[k, pl.ds(t, 1)], sem).start()
        return carry

    lax.fori_loop(0, tmc, issue, 0)
    h = h_ref[...]
    hn = _rms(h, gf_ref[...]).astype(BF16)
    sgu = _dot(hn, wsgu_ref[...])
    shared = _dot((_silu(sgu[:, :D_SHARED]) * sgu[:, D_SHARED:]).astype(BF16), wsd_ref[...])
    e = _rms(_dot(p_ref[...].astype(BF16), wple_ref[...]), gple_ref[...])
    pltpu.make_async_copy(buf, buf, sem).wait()
    acc_lo = jnp.zeros((tmc, HALF), F32)
    acc_hi = jnp.zeros((tmc, HALF), F32)
    for k in range(TOP_K):
        lo, hi = _unpack_halves(buf[k])
        wk = w_ref[:, k:k + 1]
        acc_lo = acc_lo + wk * lo
        acc_hi = acc_hi + wk * hi
    h2 = h + jnp.concatenate([acc_lo, acc_hi], axis=1) + shared
    g = _sigmoid(_dot(_rms(h2, gpg_ref[...]).astype(BF16), wpg_ref[...]))
    o_ref[...] = h2 + g * e


def _tail(pos_tiles, ys, h1, w_tok, p2, gf, wsgu, wsd, wple, gple, gpg, wpg, tmc):
    T = h1.shape[0]
    row = lambda n: pl.BlockSpec((tmc, n), lambda i: (i, 0))
    full = lambda a: pl.BlockSpec(a.shape, lambda i: (0,) * a.ndim)
    return pl.pallas_call(
        functools.partial(_tail_kernel, tmc=tmc),
        grid=(T // tmc,),
        in_specs=[
            pl.BlockSpec((1, 1, TOP_K * tmc), lambda i: (i, 0, 0), memory_space=pltpu.SMEM),
            pl.BlockSpec(memory_space=pl.ANY),
            row(D_MODEL), row(TOP_K), row(PLE_DIM),
            full(gf), full(wsgu), full(wsd), full(wple), full(gple), full(gpg), full(wpg),
        ],
        out_specs=row(D_MODEL),
        out_shape=jax.ShapeDtypeStruct((T, D_MODEL), F32),
        scratch_shapes=[pltpu.VMEM((TOP_K, tmc, HALF), jnp.uint32), pltpu.SemaphoreType.DMA],
        compiler_params=_cparams(("arbitrary",)),
        name="moe_combine_ple",
    )(pos_tiles, ys, h1, w_tok, p2, gf, wsgu, wsd, wple, gple, gpg, wpg)


def _stages(x, p, norm_mix, w_in, conv_w, a_log, dt_bias, gdn_norm, fox_f_bias, q_norm, k_norm,
            w_branch_gdn, w_branch_fox, w_out, norm_ffn, w_router, router_bias,
            we_gate, we_up, we_down, ws_gate, ws_up, ws_down, w_ple, ple_norm, ple_gate_norm, w_ple_gate):
    B, S, _ = x.shape
    T = B * S
    st = {}
    wi = w_in[0]
    o0 = 3 * GDN_HEADS * GDN_DK
    o1 = o0 + GDN_HEADS * GDN_DK
    o2 = o1 + GDN_HEADS
    o3 = o2 + GDN_HEADS
    o4 = o3 + 3 * FOX_HEADS * FOX_HD
    o5 = o4 + FOX_HEADS
    w_big = jnp.concatenate([wi[:, :o1], wi[:, o3:o4], wi[:, o5:]], axis=1).astype(BF16)
    w_small = jnp.concatenate([wi[:, o1:o3], wi[:, o4:o5],
                               jnp.zeros((D_MODEL, LANES - 2 * GDN_HEADS - FOX_HEADS), F32)], axis=1).astype(BF16)
    proj, small = _in_proj(x.reshape(T, D_MODEL), norm_mix[0].reshape(1, D_MODEL), w_big, w_small)
    st["proj_big"] = proj
    st["small"] = small

    pad = lambda v, off: jnp.zeros((1, LANES), F32).at[0, off:off + v.shape[0]].set(v)
    alog_row = pad(a_log[0], LANE_A)
    bias_row = pad(dt_bias[0], LANE_A) + pad(fox_f_bias[0], LANE_F)
    gact = _gates(small, alog_row, bias_row, B, S)
    st["gact"] = gact

    qkvn = _gdn_prep(proj, conv_w[0], B, S)
    st["qkvn"] = qkvn
    o_a = _gdn(qkvn, gact, proj, gdn_norm[0].reshape(1, LANES), B, S)
    st["o_a"] = o_a

    nrep = FOX_HEADS * FOX_HD // LANES
    gains = jnp.concatenate([jnp.tile(q_norm[0], (nrep, LANES // FOX_HD)),
                             jnp.tile(k_norm[0], (nrep, LANES // FOX_HD))], axis=0).reshape(2 * nrep, 1, LANES)
    qkn = _fox_prep(proj, gains, T)
    o_f = _fox(qkn, proj, gact, B, S)
    st["o_f"] = o_f

    x2 = x.reshape(T, D_MODEL)
    gf = norm_ffn[0].reshape(1, D_MODEL)
    h1, hp = _mix(o_a, o_f, proj, x2, w_branch_gdn[0].astype(BF16), w_branch_fox[0].astype(BF16),
                  w_out[0].astype(BF16), gf)
    st["h1"] = h1

    wrt = w_router[0].T
    wr_hi = wrt.astype(BF16)
    wr2 = jnp.stack([wr_hi, (wrt - wr_hi.astype(F32)).astype(BF16)])
    gates, sel, cnt = _router(h1, gf, wr2, router_bias[0].reshape(N_EXPERTS, 1))
    st["gates_t"] = gates

    tme = 512 if T * TOP_K // N_EXPERTS >= 2048 else 64
    pos, wts = _positions(sel, gates, cnt, tme)
    cnt1 = cnt[:, 0].astype(jnp.int32)
    padded = (cnt1 + (tme - 1)) // tme * tme
    ends = jnp.cumsum(padded)
    n_tiles = T * TOP_K // tme + N_EXPERTS
    tile_e = jnp.clip(jnp.searchsorted(ends, jnp.arange(n_tiles, dtype=jnp.int32) * tme, side="right"),
                      0, N_EXPERTS - 1).astype(jnp.int32)
    n_used = (ends[-1:] // tme).astype(jnp.int32)
    fill = ((ends - padded + cnt1) // 8 * 8).astype(jnp.int32)
    tmd = min(256, T)
    pos_tiles = pos.reshape(TOP_K, T // tmd, tmd).transpose(1, 0, 2).reshape(T // tmd, 1, TOP_K * tmd)
    xs = _dispatch(jnp.concatenate([fill, n_used]), pos_tiles, hp, n_tiles, tmd, tme)
    wgu = jnp.concatenate([we_gate[0], we_up[0]], axis=2).astype(BF16)
    ys = _experts(tile_e, n_used, xs, wgu, we_down[0].astype(BF16), n_tiles, tme)

    row = lambda v: v.reshape(1, D_MODEL)
    wsgu = jnp.concatenate([ws_gate[0], ws_up[0]], axis=1).astype(BF16)
    out = _tail(pos_tiles, ys, h1, wts.T, p[0].reshape(T, PLE_DIM), gf, wsgu, ws_down[0].astype(BF16),
                w_ple[0].astype(BF16), row(ple_norm[0]), row(ple_gate_norm[0]), w_ple_gate[0].astype(BF16), tmd)
    st["out"] = out.reshape(B, S, D_MODEL)
    return st


def kernel(x, p, norm_mix, w_in, conv_w, a_log, dt_bias, gdn_norm, fox_f_bias, q_norm, k_norm, w_branch_gdn, w_branch_fox, w_out, norm_ffn, w_router, router_bias, we_gate, we_up, we_down, ws_gate, ws_up, ws_down, w_ple, ple_norm, ple_gate_norm, w_ple_gate):
    return _stages(x, p, norm_mix, w_in, conv_w, a_log, dt_bias, gdn_norm, fox_f_bias, q_norm, k_norm,
                   w_branch_gdn, w_branch_fox, w_out, norm_ffn, w_router, router_bias,
                   we_gate, we_up, we_down, ws_gate, ws_up, ws_down, w_ple, ple_norm, ple_gate_norm,
                   w_ple_gate)["out"]
```

```python
import functools

import jax
import jax.numpy as jnp
from jax import lax
from jax.experimental import pallas as pl
from jax.experimental.pallas import tpu as pltpu

F32 = jnp.float32
BF16 = jnp.bfloat16

D_MODEL = 1024
PLE_DIM = 256
EPS = 1e-6
GDN_HEADS = 8
GDN_DK = 128
GDN_CONV = 4
GDN_CHUNK = 64
FOX_HEADS = 16
FOX_HD = 64
N_EXPERTS = 64
TOP_K = 8
D_EXPERT = 256
D_SHARED = 256
ROUTED_SCALE = 2.5

LANES = 128
COL_GQKV = 0
COL_GZ = 3072
COL_FQKV = 4096
COL_MERGE = 7168
N_BIG = 9216
LANE_A = 0
LANE_B = 8
LANE_F = 16

VMEM_LIMIT = 48 * 1024 * 1024


def _cparams(sem):
    return pltpu.CompilerParams(dimension_semantics=sem, vmem_limit_bytes=VMEM_LIMIT)


def _split3(x):
    h = x.astype(BF16)
    r = x - h.astype(F32)
    m = r.astype(BF16)
    l = (r - m.astype(F32)).astype(BF16)
    return h, m, l


def _dot(a, b, dims=(((1,), (0,)), ((), ()))):
    return lax.dot_general(a, b, dims, preferred_element_type=F32)


_NT = (((1,), (1,)), ((), ()))
_TN = (((0,), (0,)), ((), ()))


def _dot_xl(x, sel, dims=(((1,), (0,)), ((), ()))):
    h, m, l = _split3(x)
    return _dot(h, sel, dims) + _dot(m, sel, dims) + _dot(l, sel, dims)


def _dot_lx(sel, x, dims=(((1,), (0,)), ((), ()))):
    h, m, l = _split3(x)
    return _dot(sel, h, dims) + _dot(sel, m, dims) + _dot(sel, l, dims)


def _dot3(a, b):
    ah = a.astype(BF16)
    al = (a - ah.astype(F32)).astype(BF16)
    bh = b.astype(BF16)
    bl = (b - bh.astype(F32)).astype(BF16)
    return _dot(jnp.concatenate([ah, ah, al], axis=1), jnp.concatenate([bh, bl, bh], axis=0))


def _sigmoid(x):
    return 1.0 / (1.0 + jnp.exp(-x))


def _silu(x):
    return x * _sigmoid(x)


def _softplus(x):
    return jnp.maximum(x, 0.0) + jnp.log1p(jnp.exp(-jnp.abs(x)))


def _in_proj_kernel(x_ref, g_ref, wb_ref, ws_ref, ob_ref, os_ref, xn_ref):
    @pl.when(pl.program_id(1) == 0)
    def _():
        x = x_ref[...]
        ms = jnp.mean(x * x, axis=-1, keepdims=True)
        xn = (x * lax.rsqrt(ms + EPS) * g_ref[...]).astype(BF16)
        xn_ref[...] = xn
        os_ref[...] = _dot(xn, ws_ref[...])

    ob_ref[...] = _dot(xn_ref[...], wb_ref[...]).astype(ob_ref.dtype)


def _in_proj(x2, gain, w_big, w_small):
    T = x2.shape[0]
    tm = min(1024, T)
    tn = 1024
    return pl.pallas_call(
        _in_proj_kernel,
        grid=(T // tm, N_BIG // tn),
        in_specs=[
            pl.BlockSpec((tm, D_MODEL), lambda i, j: (i, 0)),
            pl.BlockSpec((1, D_MODEL), lambda i, j: (0, 0)),
            pl.BlockSpec((D_MODEL, tn), lambda i, j: (0, j)),
            pl.BlockSpec((D_MODEL, LANES), lambda i, j: (0, 0)),
        ],
        out_specs=[
            pl.BlockSpec((tm, tn), lambda i, j: (i, j)),
            pl.BlockSpec((tm, LANES), lambda i, j: (i, 0)),
        ],
        out_shape=[
            jax.ShapeDtypeStruct((T, N_BIG), BF16),
            jax.ShapeDtypeStruct((T, LANES), F32),
        ],
        scratch_shapes=[pltpu.VMEM((tm, D_MODEL), BF16)],
        compiler_params=_cparams(("parallel", "arbitrary")),
        name="in_proj",
    )(x2, gain, w_big, w_small)


def _gates_kernel(s_ref, alog_ref, bias_ref, o_ref, carry_ref, *, sb):
    @pl.when(pl.program_id(1) == 0)
    def _():
        carry_ref[...] = jnp.zeros_like(carry_ref)

    x = s_ref[...] + bias_ref[...]
    lane = lax.broadcasted_iota(jnp.int32, x.shape, 1)
    e = jnp.log1p(jnp.exp(-jnp.abs(x)))
    log_alpha = -jnp.exp(alog_ref[...]) * (jnp.maximum(x, 0.0) + e)
    beta = _sigmoid(x)
    log_f = -(jnp.maximum(-x, 0.0) + e)
    is_a = lane < LANE_B
    is_b = jnp.logical_and(lane >= LANE_B, lane < LANE_F)
    is_f = jnp.logical_and(lane >= LANE_F, lane < LANE_F + FOX_HEADS)
    val = jnp.where(is_a, log_alpha, jnp.where(is_f, log_f, 0.0))
    row = lax.broadcasted_iota(jnp.int32, (sb, sb), 0)
    col = lax.broadcasted_iota(jnp.int32, (sb, sb), 1)
    tri = row >= col
    tri_full = jnp.where(tri, 1.0, 0.0).astype(BF16)
    tri_chunk = jnp.where(jnp.logical_and(tri, row // GDN_CHUNK == col // GDN_CHUNK), 1.0, 0.0).astype(BF16)
    h, m, l = _split3(val)
    cs_full = _dot(tri_full, h) + _dot(tri_full, m) + _dot(tri_full, l) + carry_ref[0:1, :]
    cs_chunk = _dot(tri_chunk, h) + _dot(tri_chunk, m) + _dot(tri_chunk, l)
    o_ref[...] = jnp.where(is_a, cs_chunk, jnp.where(is_b, beta, jnp.where(is_f, cs_full, 0.0)))
    carry_ref[...] = jnp.broadcast_to(cs_full[sb - 1:sb, :], carry_ref.shape)


def _gates(small, alog_row, bias_row, B, S):
    sb = min(256, S)
    nsb = S // sb
    return pl.pallas_call(
        functools.partial(_gates_kernel, sb=sb),
        grid=(B, nsb),
        in_specs=[
            pl.BlockSpec((sb, LANES), lambda b, j: (b * nsb + j, 0)),
            pl.BlockSpec((1, LANES), lambda b, j: (0, 0)),
            pl.BlockSpec((1, LANES), lambda b, j: (0, 0)),
        ],
        out_specs=pl.BlockSpec((sb, LANES), lambda b, j: (b * nsb + j, 0)),
        out_shape=jax.ShapeDtypeStruct(small.shape, F32),
        scratch_shapes=[pltpu.VMEM((8, LANES), F32)],
        compiler_params=_cparams(("parallel", "arbitrary")),
        name="gates",
    )(small, alog_row, bias_row)


def _gdn_prep_kernel(x_ref, w_ref, o_ref, pad_ref, *, S, rb):
    j = pl.program_id(1)
    pad_ref[0:8, :] = jnp.zeros((8, LANES), F32)
    pad_ref[8:8 + S, :] = x_ref[...].astype(F32)
    is_qk = j < 2 * GDN_HEADS
    scale = jnp.where(j < GDN_HEADS, GDN_DK ** -0.5, 1.0).astype(F32)
    for r0 in range(0, S, rb):
        acc = jnp.zeros((rb, LANES), F32)
        for t in range(GDN_CONV):
            off = 8 - (GDN_CONV - 1) + t
            acc = acc + w_ref[t:t + 1, :] * pad_ref[r0 + off:r0 + off + rb, :]
        y = _silu(acc)
        n = y * lax.rsqrt(jnp.sum(y * y, axis=-1, keepdims=True) + EPS) * scale
        o_ref[r0:r0 + rb, :] = jnp.where(is_qk, n, y).astype(o_ref.dtype)


def _gdn_prep(proj, conv_w, B, S):
    T = B * S
    ncol = 3 * GDN_HEADS
    rb = min(256, S)
    return pl.pallas_call(
        functools.partial(_gdn_prep_kernel, S=S, rb=rb),
        grid=(B, ncol),
        in_specs=[
            pl.BlockSpec((S, LANES), lambda b, j: (b, COL_GQKV // LANES + j)),
            pl.BlockSpec((GDN_CONV, LANES), lambda b, j: (0, j)),
        ],
        out_specs=pl.BlockSpec((S, LANES), lambda b, j: (b, j)),
        out_shape=jax.ShapeDtypeStruct((T, ncol * LANES), BF16),
        scratch_shapes=[pltpu.VMEM((S + 8, LANES), F32)],
        compiler_params=_cparams(("parallel", "parallel")),
        name="gdn_prep",
    )(proj, conv_w)


_PAIR = 2 * GDN_CHUNK


def _inv_unit_lower(lows, blk16, eye):
    xs = [jnp.where(blk16, -low, 0.0) for low in lows]
    offs = [jnp.where(blk16, 0.0, low) for low in lows]
    x2 = [_dot3(x, x) for x in xs]
    x4 = [_dot3(a, a) for a in x2]
    d1 = [_dot3(eye + x, eye + a) for x, a in zip(xs, x2)]
    x8 = [_dot3(a, a) for a in x4]
    d2 = [_dot3(eye + a, eye + b) for a, b in zip(x4, x8)]
    dinv = [_dot3(a, b) for a, b in zip(d1, d2)]
    ns = [_dot3(d, o) for d, o in zip(dinv, offs)]
    n2 = [_dot3(n, n) for n in ns]
    t1 = [_dot3(eye - n, eye + m) for n, m in zip(ns, n2)]
    return [_dot3(t, d) for t, d in zip(t1, dinv)]


_GDN_HB = 4
_GDN_UNROLL = 2


def _gdn_kernel(q_ref, k_ref, v_ref, ga_ref, z_ref, gn_ref, o_ref, kw_s, c_s, qp_s, oi_s, a_s, *, S):
    hg = pl.program_id(1)
    P = _PAIR
    C = GDN_CHUNK
    row = lax.broadcasted_iota(jnp.int32, (P, P), 0)
    col = lax.broadcasted_iota(jnp.int32, (P, P), 1)
    same = row // C == col // C
    causal = jnp.logical_and(same, row >= col)
    strict = jnp.logical_and(same, row > col)
    blk16 = row // 16 == col // 16
    eye = jnp.where(row == col, 1.0, 0.0).astype(F32)
    first_half = row < C
    heads = [hg * _GDN_HB + hh for hh in range(_GDN_HB)]
    row2 = lax.broadcasted_iota(jnp.int32, (2 * LANES, 2 * LANES), 0) % LANES
    col2 = lax.broadcasted_iota(jnp.int32, (2 * LANES, 2 * LANES), 1)
    colr = lax.broadcasted_iota(jnp.int32, (P, 2 * LANES), 1) % LANES
    sel_gb = [jnp.where(row2 == jnp.where(col2 < LANES, LANE_A + h, LANE_B + h), 1.0, 0.0).astype(BF16) for h in heads]
    sel_gr = [jnp.where(colr == LANE_A + h, 1.0, 0.0).astype(BF16) for h in heads]

    def prep(it, carry):
        chains = [(it * _GDN_UNROLL + u, hh) for u in range(_GDN_UNROLL) for hh in range(_GDN_HB)]
        r0 = [pl.multiple_of(p * P, P) for p, _ in chains]
        ln = [slice(hh * LANES, (hh + 1) * LANES) for _, hh in chains]
        hh_ = [hh for _, hh in chains]
        n = len(chains)
        q = [q_ref[pl.ds(r0[i], P), ln[i]].astype(F32) for i in range(n)]
        k = [k_ref[pl.ds(r0[i], P), ln[i]].astype(F32) for i in range(n)]
        v = [v_ref[pl.ds(r0[i], P), ln[i]].astype(F32) for i in range(n)]
        ga = [ga_ref[pl.ds(r0[i], P), :] for i in range(n)]
        gah = [x.astype(BF16) for x in ga]
        gam = [(ga[i] - gah[i].astype(F32)).astype(BF16) for i in range(n)]
        ga2 = [jnp.concatenate([gah[i], gam[i]], axis=1) for i in range(n)]
        gb = [_dot(ga2[i], sel_gb[hh_[i]]) for i in range(n)]
        gcol = [x[:, :LANES] for x in gb]
        bcol = [x[:, LANES:] for x in gb]
        grow = [_dot(sel_gr[hh_[i]], ga2[i], _NT) for i in range(n)]
        kb = [x.astype(BF16) for x in k]
        kk = [_dot(x, x, _NT) for x in kb]
        qkr = [_dot(q[i].astype(BF16), kb[i], _NT) for i in range(n)]
        glast = [jnp.where(first_half, g[C - 1:C, :], g[P - 1:P, :]) for g in gcol]
        decay = [jnp.where(causal, jnp.exp(jnp.where(causal, gcol[i] - grow[i], 0.0)), 0.0) for i in range(n)]
        low = [jnp.where(strict, bcol[i] * kk[i] * decay[i], 0.0) for i in range(n)]
        tinv = _inv_unit_lower(low, blk16, eye)
        eg = [jnp.exp(g) for g in gcol]
        rhs = [jnp.concatenate([v[i] * bcol[i], k[i] * (bcol[i] * eg[i])], axis=1) for i in range(n)]
        solb = [_dot3(tinv[i], rhs[i]).astype(BF16) for i in range(n)]
        qw = [_dot((qkr[i] * decay[i]).astype(BF16), solb[i]) for i in range(n)]
        ktail = [(k[i] * jnp.exp(glast[i] - gcol[i])).astype(BF16) for i in range(n)]
        for i, (p, hh) in enumerate(chains):
            qp_s[hh, pl.ds(r0[i], P), :] = (q[i] * eg[i] - qw[i][:, LANES:]).astype(BF16)
            oi_s[hh, pl.ds(r0[i], P), :] = qw[i][:, :LANES]
            for half in range(2):
                rows = slice(half * C, (half + 1) * C)
                c = 2 * p + half
                kwc = _dot(ktail[i][rows], solb[i][rows], _TN)
                c_s[hh, c] = kwc[:, :LANES]
                kw_s[hh, c] = kwc[:, LANES:].astype(BF16)
                a_s[hh, c] = jnp.broadcast_to(jnp.exp(glast[i][half * C:half * C + 1, :]), (8, LANES))
        return carry

    lax.fori_loop(0, S // (P * _GDN_UNROLL), prep, 0)

    def step(c, states):
        r0 = pl.multiple_of(c * C, C)
        hs = range(_GDN_HB)
        sb = [states[hh].astype(BF16) for hh in hs]
        ks = [_dot(kw_s[hh, c], sb[hh]) for hh in hs]
        os_ = [_dot(qp_s[hh, pl.ds(r0, C), :], sb[hh]) for hh in hs]
        new = [a_s[hh, c][0:1, :] * states[hh] - ks[hh] + c_s[hh, c] for hh in hs]
        for hh in hs:
            lanes = slice(hh * LANES, (hh + 1) * LANES)
            o = os_[hh] + oi_s[hh, pl.ds(r0, C), :]
            on = o * lax.rsqrt(jnp.mean(o * o, axis=-1, keepdims=True) + EPS) * gn_ref[...]
            z = z_ref[pl.ds(r0, C), lanes].astype(F32)
            o_ref[pl.ds(r0, C), lanes] = (on * _silu(z)).astype(o_ref.dtype)
        return tuple(new)

    lax.fori_loop(0, S // C, step, tuple(jnp.zeros((GDN_DK, LANES), F32) for _ in range(_GDN_HB)))


def _gdn(qkvn, gact, proj, gnorm, B, S):
    T = B * S
    nc = S // GDN_CHUNK
    hb = _GDN_HB
    ng = GDN_HEADS // hb
    w = hb * LANES
    blk = lambda off: pl.BlockSpec((S, w), lambda b, h: (b, off + h))
    return pl.pallas_call(
        functools.partial(_gdn_kernel, S=S),
        grid=(B, ng),
        in_specs=[
            blk(0), blk(ng), blk(2 * ng),
            pl.BlockSpec((S, LANES), lambda b, h: (b, 0)),
            blk(COL_GZ // w),
            pl.BlockSpec((1, LANES), lambda b, h: (0, 0)),
        ],
        out_specs=pl.BlockSpec((S, w), lambda b, h: (b, h)),
        out_shape=jax.ShapeDtypeStruct((T, GDN_HEADS * LANES), BF16),
        scratch_shapes=[
            pltpu.VMEM((hb, nc, GDN_DK, LANES), BF16),
            pltpu.VMEM((hb, nc, GDN_DK, LANES), F32),
            pltpu.VMEM((hb, S, LANES), BF16),
            pltpu.VMEM((hb, S, LANES), F32),
            pltpu.VMEM((hb, nc, 8, LANES), F32),
        ],
        compiler_params=_cparams(("parallel", "parallel")),
        name="gdn",
    )(qkvn, qkvn, qkvn, gact, proj, gnorm)


LOG2E = 1.4426950408889634
AUG0 = FOX_HD


def _fox_prep_kernel(x_ref, ga_ref, g_ref, o_ref):
    j = pl.program_id(1)
    is_q = j < FOX_HEADS
    head = j % FOX_HEADS
    odd = head % 2 == 1
    x = x_ref[...].astype(F32)
    row = lax.broadcasted_iota(jnp.int32, (LANES, LANES), 0)
    col = lax.broadcasted_iota(jnp.int32, (LANES, LANES), 1)
    grp = jnp.where(row // FOX_HD == col // FOX_HD, 1.0, 0.0).astype(BF16)
    xx = x * x
    hi = xx.astype(BF16)
    lo = (xx - hi.astype(F32)).astype(BF16)
    ms = _dot(jnp.concatenate([hi, lo], axis=1), jnp.concatenate([grp, grp], axis=0)) * (1.0 / FOX_HD)
    scale = jnp.where(is_q, LOG2E * FOX_HD ** -0.5, 1.0).astype(F32)
    xn = x * lax.rsqrt(ms + EPS) * (g_ref[0] * scale)
    xn = jnp.where(odd, pltpu.roll(xn, FOX_HD, axis=1), xn)
    sel_c = jnp.where(row == LANE_F + head, 1.0, 0.0).astype(BF16)
    c = _dot_xl(ga_ref[...], sel_c) * LOG2E
    c1 = c.astype(BF16).astype(F32)
    r1 = c - c1
    c2 = r1.astype(BF16).astype(F32)
    c3 = (r1 - c2).astype(BF16).astype(F32)
    sgn = jnp.where(is_q, 1.0, -1.0).astype(F32)
    lane = lax.broadcasted_iota(jnp.int32, x.shape, 1)
    coff = jnp.where(is_q, AUG0, AUG0 + 3)
    ooff = jnp.where(is_q, AUG0 + 3, AUG0)
    aug = jnp.where(lane == coff, c1, jnp.where(lane == coff + 1, c2, jnp.where(lane == coff + 2, c3, 0.0))) * sgn
    aug = jnp.where(jnp.logical_and(lane >= ooff, lane < ooff + 3), 1.0, aug)
    o_ref[...] = jnp.where(lane < FOX_HD, xn, aug).astype(o_ref.dtype)


def _fox_prep(proj, gact, gains, T):
    tm = min(1024, T)
    ncol = 2 * FOX_HEADS
    src = lambda j: COL_FQKV // LANES + (j // FOX_HEADS) * (FOX_HEADS // 2) + (j % FOX_HEADS) // 2
    return pl.pallas_call(
        _fox_prep_kernel,
        grid=(T // tm, ncol),
        in_specs=[
            pl.BlockSpec((tm, LANES), lambda i, j: (i, src(j))),
            pl.BlockSpec((tm, LANES), lambda i, j: (i, 0)),
            pl.BlockSpec((1, 1, LANES), lambda i, j: ((j // FOX_HEADS) * (FOX_HEADS // 2) + (j % FOX_HEADS) // 2, 0, 0)),
        ],
        out_specs=pl.BlockSpec((tm, LANES), lambda i, j: (i, j)),
        out_shape=jax.ShapeDtypeStruct((T, ncol * LANES), BF16),
        compiler_params=_cparams(("parallel", "arbitrary")),
        name="fox_prep",
    )(proj, gact, gains)


def _vt_kernel(v_ref, o_ref):
    o_ref[...] = jnp.transpose(v_ref[...].astype(F32)).astype(o_ref.dtype)


def _fox_vt(proj, T):
    tm = min(512, T)
    nhp = FOX_HEADS // 2
    return pl.pallas_call(
        _vt_kernel,
        grid=(T // tm, nhp),
        in_specs=[pl.BlockSpec((tm, LANES), lambda i, h: (i, COL_FQKV // LANES + 2 * nhp + h))],
        out_specs=pl.BlockSpec((LANES, tm), lambda i, h: (h, i)),
        out_shape=jax.ShapeDtypeStruct((nhp * LANES, T), BF16),
        compiler_params=_cparams(("parallel", "parallel")),
        name="fox_vt",
    )(proj)


def _fox_kernel(q_ref, k_ref, vt_ref, o_ref, *, tq):
    i = pl.program_id(2)
    hs = range(2)
    q = [q_ref[:, hh * LANES:(hh + 1) * LANES] for hh in hs]
    keyi = lax.broadcasted_iota(jnp.int32, (tq, tq), 0)
    qryi = lax.broadcasted_iota(jnp.int32, (tq, tq), 1)
    tri = keyi <= qryi

    def kv_step(jb, carry, masked):
        ms, ls, accs = carry
        r0 = pl.multiple_of(jb * tq, tq)
        vt = vt_ref[:, pl.ds(r0, tq)]
        s = [_dot(k_ref[pl.ds(r0, tq), hh * LANES:(hh + 1) * LANES], q[hh], _NT) for hh in hs]
        if masked:
            s = [jnp.where(tri, x, -jnp.inf) for x in s]
        m_new = [jnp.maximum(ms[hh], jnp.max(s[hh], axis=0, keepdims=True)) for hh in hs]
        pr = [jnp.exp2(s[hh] - m_new[hh]) for hh in hs]
        alpha = [jnp.exp2(ms[hh] - m_new[hh]) for hh in hs]
        pv = [_dot(vt, pr[hh].astype(BF16)) for hh in hs]
        ls = [alpha[hh] * ls[hh] + jnp.sum(pr[hh], axis=0, keepdims=True) for hh in hs]
        accs = [alpha[hh] * accs[hh] + pv[hh] for hh in hs]
        return m_new, ls, accs

    init = ([jnp.full((1, tq), -jnp.inf, F32) for _ in hs], [jnp.zeros((1, tq), F32) for _ in hs],
            [jnp.zeros((LANES, tq), F32) for _ in hs])
    carry = lax.fori_loop(0, i, lambda jb, c: kv_step(jb, c, False), init)
    _, ls, accs = kv_step(i, carry, True)
    dim = lax.broadcasted_iota(jnp.int32, (LANES, tq), 0)
    ot = jnp.where(dim < FOX_HD, accs[0] / ls[0], accs[1] / ls[1])
    o_ref[...] = jnp.transpose(ot).astype(o_ref.dtype)


def _fox(qka, vt, B, S):
    T = B * S
    tq = min(512, S)
    nq = S // tq
    nhp = FOX_HEADS // 2
    return pl.pallas_call(
        functools.partial(_fox_kernel, tq=tq),
        grid=(B, nhp, nq),
        in_specs=[
            pl.BlockSpec((tq, 2 * LANES), lambda b, h, i: (b * nq + i, h)),
            pl.BlockSpec((S, 2 * LANES), lambda b, h, i: (b, nhp + h)),
            pl.BlockSpec((LANES, S), lambda b, h, i: (h, b)),
        ],
        out_specs=pl.BlockSpec((tq, LANES), lambda b, h, i: (b * nq + i, h)),
        out_shape=jax.ShapeDtypeStruct((T, nhp * LANES), BF16),
        compiler_params=_cparams(("parallel", "parallel", "arbitrary")),
        name="fox",
    )(qka, qka, vt)


HALF = D_MODEL // 2


def _rms(x, gain):
    return x * lax.rsqrt(jnp.mean(x * x, axis=-1, keepdims=True) + EPS) * gain


def _pack_halves(x):
    lo = lax.bitcast_convert_type(x[:, :HALF].astype(BF16).astype(F32), jnp.uint32)
    hi = lax.bitcast_convert_type(x[:, HALF:].astype(BF16).astype(F32), jnp.uint32)
    return lax.shift_right_logical(lo, jnp.uint32(16)) | (hi & jnp.uint32(0xFFFF0000))


def _unpack_halves(w):
    lo = lax.bitcast_convert_type(lax.shift_left(w, jnp.uint32(16)), F32)
    hi = lax.bitcast_convert_type(w & jnp.uint32(0xFFFF0000), F32)
    return lo, hi


def _mix_kernel(oa_ref, of_ref, ma_ref, mf_ref, x_ref, wa_ref, wf_ref, wo_ref, g_ref, h_ref, hp_ref):
    ya = _dot(oa_ref[...], wa_ref[...])
    yf = _dot(of_ref[...], wf_ref[...])
    m = _sigmoid(ma_ref[...].astype(F32)) * ya + _sigmoid(mf_ref[...].astype(F32)) * yf
    h = x_ref[...] + _dot(m.astype(BF16), wo_ref[...])
    h_ref[...] = h
    hp_ref[...] = _pack_halves(_rms(h, g_ref[...]))


def _mix(o_a, o_f, proj, x2, wa, wf, wo, gain):
    T = x2.shape[0]
    tm = min(512, T)
    row = lambda c: pl.BlockSpec((tm, D_MODEL), lambda i: (i, c))
    full = pl.BlockSpec((D_MODEL, D_MODEL), lambda i: (0, 0))
    return pl.pallas_call(
        _mix_kernel,
        grid=(T // tm,),
        in_specs=[row(0), row(0), row(COL_MERGE // D_MODEL), row(COL_MERGE // D_MODEL + 1), row(0),
                  full, full, full, pl.BlockSpec((1, D_MODEL), lambda i: (0, 0))],
        out_specs=[row(0), pl.BlockSpec((tm, HALF), lambda i: (i, 0))],
        out_shape=[jax.ShapeDtypeStruct((T, D_MODEL), F32), jax.ShapeDtypeStruct((T, HALF), jnp.uint32)],
        compiler_params=_cparams(("parallel",)),
        name="mix_out",
    )(o_a, o_f, proj, proj, x2, wa, wf, wo, gain)


def _router_kernel(h_ref, g_ref, wr_ref, bias_ref, gates_ref, sel_ref, cnt_ref):
    @pl.when(pl.program_id(0) == 0)
    def _():
        cnt_ref[...] = jnp.zeros_like(cnt_ref)

    hn = _rms(h_ref[...], g_ref[...])
    hh = hn.astype(BF16)
    hl = (hn - hh.astype(F32)).astype(BF16)
    logits = _dot(wr_ref[0], hh, _NT) + _dot(wr_ref[0], hl, _NT) + _dot(wr_ref[1], hh, _NT)
    scores = _sigmoid(logits)
    work = scores + bias_ref[...]
    eidx = lax.broadcasted_iota(jnp.int32, work.shape, 0)
    sel = jnp.zeros(work.shape, F32)
    for _ in range(TOP_K):
        mx = jnp.max(work, axis=0, keepdims=True)
        first = jnp.min(jnp.where(work == mx, eidx, N_EXPERTS), axis=0, keepdims=True)
        onehot = eidx == first
        sel = jnp.where(onehot, 1.0, sel)
        work = jnp.where(onehot, -jnp.inf, work)
    s = jnp.where(sel > 0.0, scores, 0.0)
    gates_ref[...] = s / jnp.sum(s, axis=0, keepdims=True) * ROUTED_SCALE
    sel_ref[...] = sel
    cnt_ref[...] += jnp.broadcast_to(jnp.sum(sel, axis=1, keepdims=True), cnt_ref.shape)


def _router(h1, gain, wr2, bias_col):
    T = h1.shape[0]
    tm = min(512, T)
    return pl.pallas_call(
        _router_kernel,
        grid=(T // tm,),
        in_specs=[
            pl.BlockSpec((tm, D_MODEL), lambda i: (i, 0)),
            pl.BlockSpec((1, D_MODEL), lambda i: (0, 0)),
            pl.BlockSpec((2, N_EXPERTS, D_MODEL), lambda i: (0, 0, 0)),
            pl.BlockSpec((N_EXPERTS, 1), lambda i: (0, 0)),
        ],
        out_specs=[
            pl.BlockSpec((N_EXPERTS, tm), lambda i: (0, i)),
            pl.BlockSpec((N_EXPERTS, tm), lambda i: (0, i)),
            pl.BlockSpec((N_EXPERTS, LANES), lambda i: (0, 0)),
        ],
        out_shape=[
            jax.ShapeDtypeStruct((N_EXPERTS, T), F32),
            jax.ShapeDtypeStruct((N_EXPERTS, T), F32),
            jax.ShapeDtypeStruct((N_EXPERTS, LANES), F32),
        ],
        compiler_params=_cparams(("arbitrary",)),
        name="router",
    )(h1, gain, wr2, bias_col)


def _pos_kernel(sel_ref, gates_ref, cnt_ref, pos_ref, w_ref, base_s, carry_s, *, tm, tme):
    @pl.when(pl.program_id(0) == 0)
    def _():
        padded = jnp.floor((cnt_ref[...] + (tme - 1)) * (1.0 / tme)) * tme
        r = lax.broadcasted_iota(jnp.int32, (N_EXPERTS, N_EXPERTS), 0)
        c = lax.broadcasted_iota(jnp.int32, (N_EXPERTS, N_EXPERTS), 1)
        base_s[...] = _dot_lx(jnp.where(r > c, 1.0, 0.0).astype(BF16), padded)
        carry_s[...] = jnp.zeros_like(carry_s)

    sel = sel_ref[...]
    gates = gates_ref[...]
    r = lax.broadcasted_iota(jnp.int32, (tm, tm), 0)
    c = lax.broadcasted_iota(jnp.int32, (tm, tm), 1)
    cs = _dot(sel.astype(BF16), jnp.where(r <= c, 1.0, 0.0).astype(BF16))
    start = base_s[...] + carry_s[...]
    posd = cs - sel + jnp.concatenate([start] * (tm // LANES), axis=1)
    carry_s[...] += jnp.broadcast_to(jnp.sum(sel, axis=1, keepdims=True), carry_s.shape)
    eidx = lax.broadcasted_iota(jnp.int32, sel.shape, 0)
    rem = sel
    ps, ws = [], []
    for _ in range(TOP_K):
        first = jnp.min(jnp.where(rem > 0.0, eidx, N_EXPERTS), axis=0, keepdims=True)
        onehot = eidx == first
        ps.append(jnp.sum(jnp.where(onehot, posd, 0.0), axis=0, keepdims=True))
        ws.append(jnp.sum(jnp.where(onehot, gates, 0.0), axis=0, keepdims=True))
        rem = jnp.where(onehot, 0.0, rem)
    pos_ref[...] = jnp.concatenate(ps, axis=0).astype(jnp.int32)
    w_ref[...] = jnp.concatenate(ws, axis=0)


def _positions(sel, gates, cnt, tme):
    T = sel.shape[1]
    tm = min(512, T)
    blk = pl.BlockSpec((N_EXPERTS, tm), lambda i: (0, i))
    return pl.pallas_call(
        functools.partial(_pos_kernel, tm=tm, tme=tme),
        grid=(T // tm,),
        in_specs=[blk, blk, pl.BlockSpec((N_EXPERTS, LANES), lambda i: (0, 0))],
        out_specs=[pl.BlockSpec((TOP_K, tm), lambda i: (0, i)), pl.BlockSpec((TOP_K, tm), lambda i: (0, i))],
        out_shape=[jax.ShapeDtypeStruct((TOP_K, T), jnp.int32), jax.ShapeDtypeStruct((TOP_K, T), F32)],
        scratch_shapes=[pltpu.VMEM((N_EXPERTS, LANES), F32), pltpu.VMEM((N_EXPERTS, LANES), F32)],
        compiler_params=_cparams(("arbitrary",)),
        name="moe_positions",
    )(sel, gates, cnt)


def _dispatch_kernel(meta_ref, pos_ref, hp_ref, xs_ref, zero_s, sem, zsem, *, tmd, tme, n_tiles):
    i = pl.program_id(0)

    @pl.when(i == 0)
    def _():
        zero_s[...] = jnp.zeros_like(zero_s)
        for e in range(N_EXPERTS):
            pltpu.make_async_copy(zero_s, xs_ref.at[pl.ds(pl.multiple_of(meta_ref[e], 8), tme)], zsem).start()
        for e in range(N_EXPERTS):
            pltpu.make_async_copy(zero_s, xs_ref.at[pl.ds(pl.multiple_of(meta_ref[e], 8), tme)], zsem).wait()

        def zero_tile(t, carry):
            cp = pltpu.make_async_copy(zero_s, xs_ref.at[pl.ds(pl.multiple_of(t * tme, tme), tme)], zsem)
            cp.start()
            cp.wait()
            return carry

        lax.fori_loop(meta_ref[N_EXPERTS], n_tiles, zero_tile, 0)

    def issue(t, carry):
        src = hp_ref.at[pl.ds(t, 1)]
        for k in range(TOP_K):
            pltpu.make_async_copy(src, xs_ref.at[pl.ds(pos_ref[0, 0, k * tmd + t], 1)], sem).start()
        return carry

    lax.fori_loop(0, tmd, issue, 0)
    pltpu.make_async_copy(xs_ref.at[pl.ds(0, TOP_K * tmd)], xs_ref.at[pl.ds(0, TOP_K * tmd)], sem).wait()


def _dispatch(meta, pos_tiles, hp, n_tiles, tmd, tme):
    T = hp.shape[0]
    return pl.pallas_call(
        functools.partial(_dispatch_kernel, tmd=tmd, tme=tme, n_tiles=n_tiles),
        grid_spec=pltpu.PrefetchScalarGridSpec(
            num_scalar_prefetch=1,
            grid=(T // tmd,),
            in_specs=[
                pl.BlockSpec((1, 1, TOP_K * tmd), lambda i, f: (i, 0, 0), memory_space=pltpu.SMEM),
                pl.BlockSpec((tmd, HALF), lambda i, f: (i, 0)),
            ],
            out_specs=pl.BlockSpec(memory_space=pl.ANY),
            scratch_shapes=[pltpu.VMEM((tme, HALF), jnp.uint32), pltpu.SemaphoreType.DMA, pltpu.SemaphoreType.DMA],
        ),
        out_shape=jax.ShapeDtypeStruct((n_tiles * tme, HALF), jnp.uint32),
        compiler_params=_cparams(("arbitrary",)),
        name="moe_dispatch",
    )(meta, pos_tiles, hp)


def _expert_kernel(te_ref, nu_ref, xs_ref, wgu_ref, wd_ref, ys_ref):
    @pl.when(pl.program_id(0) < nu_ref[0])
    def _():
        lo, hi = _unpack_halves(xs_ref[...])
        gu = _dot(lo.astype(BF16), wgu_ref[0, :HALF, :]) + _dot(hi.astype(BF16), wgu_ref[0, HALF:, :])
        hid = _silu(gu[:, :D_EXPERT]) * gu[:, D_EXPERT:]
        ys_ref[...] = _pack_halves(_dot(hid.astype(BF16), wd_ref[0]))

    @pl.when(pl.program_id(0) >= nu_ref[0])
    def _():
        ys_ref[...] = jnp.zeros_like(ys_ref)


def _experts(tile_e, n_used, xs, wgu, wd, n_tiles, tme):
    clamp = lambda i, te, nu: (jnp.minimum(i, nu[0] - 1), 0)
    return pl.pallas_call(
        _expert_kernel,
        grid_spec=pltpu.PrefetchScalarGridSpec(
            num_scalar_prefetch=2,
            grid=(n_tiles,),
            in_specs=[
                pl.BlockSpec((tme, HALF), clamp),
                pl.BlockSpec((1, D_MODEL, 2 * D_EXPERT), lambda i, te, nu: (te[i], 0, 0)),
                pl.BlockSpec((1, D_EXPERT, D_MODEL), lambda i, te, nu: (te[i], 0, 0)),
            ],
            out_specs=pl.BlockSpec((tme, HALF), lambda i, te, nu: (i, 0)),
        ),
        out_shape=jax.ShapeDtypeStruct((n_tiles * tme, HALF), jnp.uint32),
        compiler_params=_cparams(("arbitrary",)),
        name="moe_experts",
    )(tile_e, n_used, xs, wgu, wd)


def _tail_kernel(pos_ref, ys_ref, h_ref, w_ref, p_ref, gf_ref, wsgu_ref, wsd_ref, wple_ref, gple_ref, gpg_ref,
                 wpg_ref, o_ref, buf, sem, *, tmc):
    def issue(t, carry):
        for k in range(TOP_K):
            pltpu.make_async_copy(ys_ref.at[pl.ds(pos_ref[0, 0, k * tmc + t], 1)], buf.at[k, pl.ds(t, 1)], sem).start()
        return carry

    lax.fori_loop(0, tmc, issue, 0)
    h = h_ref[...]
    hn = _rms(h, gf_ref[...]).astype(BF16)
    sgu = _dot(hn, wsgu_ref[...])
    shared = _dot((_silu(sgu[:, :D_SHARED]) * sgu[:, D_SHARED:]).astype(BF16), wsd_ref[...])
    e = _rms(_dot(p_ref[...].astype(BF16), wple_ref[...]), gple_ref[...])
    pltpu.make_async_copy(buf, buf, sem).wait()
    acc_lo = jnp.zeros((tmc, HALF), F32)
    acc_hi = jnp.zeros((tmc, HALF), F32)
    for k in range(TOP_K):
        lo, hi = _unpack_halves(buf[k])
        wk = w_ref[:, k:k + 1]
        acc_lo = acc_lo + wk * lo
        acc_hi = acc_hi + wk * hi
    h2 = h + jnp.concatenate([acc_lo, acc_hi], axis=1) + shared
    g = _sigmoid(_dot(_rms(h2, gpg_ref[...]).astype(BF16), wpg_ref[...]))
    o_ref[...] = h2 + g * e


def _tail(pos_tiles, ys, h1, w_tok, p2, gf, wsgu, wsd, wple, gple, gpg, wpg, tmc):
    T = h1.shape[0]
    row = lambda n: pl.BlockSpec((tmc, n), lambda i: (i, 0))
    full = lambda a: pl.BlockSpec(a.shape, lambda i: (0,) * a.ndim)
    return pl.pallas_call(
        functools.partial(_tail_kernel, tmc=tmc),
        grid=(T // tmc,),
        in_specs=[
            pl.BlockSpec((1, 1, TOP_K * tmc), lambda i: (i, 0, 0), memory_space=pltpu.SMEM),
            pl.BlockSpec(memory_space=pl.ANY),
            row(D_MODEL), row(TOP_K), row(PLE_DIM),
            full(gf), full(wsgu), full(wsd), full(wple), full(gple), full(gpg), full(wpg),
        ],
        out_specs=row(D_MODEL),
        out_shape=jax.ShapeDtypeStruct((T, D_MODEL), F32),
        scratch_shapes=[pltpu.VMEM((TOP_K, tmc, HALF), jnp.uint32), pltpu.SemaphoreType.DMA],
        compiler_params=_cparams(("arbitrary",)),
        name="moe_combine_ple",
    )(pos_tiles, ys, h1, w_tok, p2, gf, wsgu, wsd, wple, gple, gpg, wpg)


def _stages(x, p, norm_mix, w_in, conv_w, a_log, dt_bias, gdn_norm, fox_f_bias, q_norm, k_norm,
            w_branch_gdn, w_branch_fox, w_out, norm_ffn, w_router, router_bias,
            we_gate, we_up, we_down, ws_gate, ws_up, ws_down, w_ple, ple_norm, ple_gate_norm, w_ple_gate):
    B, S, _ = x.shape
    T = B * S
    st = {}
    wi = w_in[0]
    o0 = 3 * GDN_HEADS * GDN_DK
    o1 = o0 + GDN_HEADS * GDN_DK
    o2 = o1 + GDN_HEADS
    o3 = o2 + GDN_HEADS
    o4 = o3 + 3 * FOX_HEADS * FOX_HD
    o5 = o4 + FOX_HEADS
    w_big = jnp.concatenate([wi[:, :o1], wi[:, o3:o4], wi[:, o5:]], axis=1).astype(BF16)
    w_small = jnp.concatenate([wi[:, o1:o3], wi[:, o4:o5],
                               jnp.zeros((D_MODEL, LANES - 2 * GDN_HEADS - FOX_HEADS), F32)], axis=1).astype(BF16)
    proj, small = _in_proj(x.reshape(T, D_MODEL), norm_mix[0].reshape(1, D_MODEL), w_big, w_small)
    st["proj_big"] = proj
    st["small"] = small

    pad = lambda v, off: jnp.zeros((1, LANES), F32).at[0, off:off + v.shape[0]].set(v)
    alog_row = pad(a_log[0], LANE_A)
    bias_row = pad(dt_bias[0], LANE_A) + pad(fox_f_bias[0], LANE_F)
    gact = _gates(small, alog_row, bias_row, B, S)
    st["gact"] = gact

    qkvn = _gdn_prep(proj, conv_w[0], B, S)
    st["qkvn"] = qkvn
    o_a = _gdn(qkvn, gact, proj, gdn_norm[0].reshape(1, LANES), B, S)
    st["o_a"] = o_a

    nrep = FOX_HEADS * FOX_HD // LANES
    gains = jnp.concatenate([jnp.tile(q_norm[0], (nrep, LANES // FOX_HD)),
                             jnp.tile(k_norm[0], (nrep, LANES // FOX_HD))], axis=0).reshape(2 * nrep, 1, LANES)
    qka = _fox_prep(proj, gact, gains, T)
    o_f = _fox(qka, _fox_vt(proj, T), B, S)
    st["o_f"] = o_f

    x2 = x.reshape(T, D_MODEL)
    gf = norm_ffn[0].reshape(1, D_MODEL)
    h1, hp = _mix(o_a, o_f, proj, x2, w_branch_gdn[0].astype(BF16), w_branch_fox[0].astype(BF16),
                  w_out[0].astype(BF16), gf)
    st["h1"] = h1

    wrt = w_router[0].T
    wr_hi = wrt.astype(BF16)
    wr2 = jnp.stack([wr_hi, (wrt - wr_hi.astype(F32)).astype(BF16)])
    gates, sel, cnt = _router(h1, gf, wr2, router_bias[0].reshape(N_EXPERTS, 1))
    st["gates_t"] = gates

    tme = 512 if T * TOP_K // N_EXPERTS >= 2048 else 64
    pos, wts = _positions(sel, gates, cnt, tme)
    cnt1 = cnt[:, 0].astype(jnp.int32)
    padded = (cnt1 + (tme - 1)) // tme * tme
    ends = jnp.cumsum(padded)
    n_tiles = T * TOP_K // tme + N_EXPERTS
    tile_start = jnp.arange(n_tiles, dtype=jnp.int32) * tme
    tile_e = jnp.minimum(jnp.sum((ends[None, :] <= tile_start[:, None]).astype(jnp.int32), axis=1), N_EXPERTS - 1)
    n_used = (ends[-1:] // tme).astype(jnp.int32)
    fill = ((ends - padded + cnt1) // 8 * 8).astype(jnp.int32)
    tmd = min(256, T)
    pos_tiles = pos.reshape(TOP_K, T // tmd, tmd).transpose(1, 0, 2).reshape(T // tmd, 1, TOP_K * tmd)
    xs = _dispatch(jnp.concatenate([fill, n_used]), pos_tiles, hp, n_tiles, tmd, tme)
    wgu = jnp.concatenate([we_gate[0], we_up[0]], axis=2).astype(BF16)
    ys = _experts(tile_e, n_used, xs, wgu, we_down[0].astype(BF16), n_tiles, tme)

    row = lambda v: v.reshape(1, D_MODEL)
    wsgu = jnp.concatenate([ws_gate[0], ws_up[0]], axis=1).astype(BF16)
    out = _tail(pos_tiles, ys, h1, wts.T, p[0].reshape(T, PLE_DIM), gf, wsgu, ws_down[0].astype(BF16),
                w_ple[0].astype(BF16), row(ple_norm[0]), row(ple_gate_norm[0]), w_ple_gate[0].astype(BF16), tmd)
    st["out"] = out.reshape(B, S, D_MODEL)
    return st


def kernel(x, p, norm_mix, w_in, conv_w, a_log, dt_bias, gdn_norm, fox_f_bias, q_norm, k_norm, w_branch_gdn, w_branch_fox, w_out, norm_ffn, w_router, router_bias, we_gate, we_up, we_down, ws_gate, ws_up, ws_down, w_ple, ple_norm, ple_gate_norm, w_ple_gate):
    return _stages(x, p, norm_mix, w_in, conv_w, a_log, dt_bias, gdn_norm, fox_f_bias, q_norm, k_norm,
                   w_branch_gdn, w_branch_fox, w_out, norm_ffn, w_router, router_bias,
                   we_gate, we_up, we_down, ws_gate, ws_up, ws_down, w_ple, ple_norm, ple_gate_norm,
                   w_ple_gate)["out"]
```

```python
import functools

import jax
import jax.numpy as jnp
from jax import lax
from jax.experimental import pallas as pl
from jax.experimental.pallas import tpu as pltpu

F32 = jnp.float32
BF16 = jnp.bfloat16

D_MODEL = 1024
PLE_DIM = 256
EPS = 1e-6
GDN_HEADS = 8
GDN_DK = 128
GDN_CONV = 4
GDN_CHUNK = 64
FOX_HEADS = 16
FOX_HD = 64
N_EXPERTS = 64
TOP_K = 8
D_EXPERT = 256
D_SHARED = 256
ROUTED_SCALE = 2.5

LANES = 128
COL_GQKV = 0
COL_GZ = 3072
COL_FQKV = 4096
COL_MERGE = 7168
N_BIG = 9216
LANE_A = 0
LANE_B = 8
LANE_F = 16
LANE_ONE = 127
LOG2E = 1.4426950408889634

VMEM_LIMIT = 48 * 1024 * 1024


def _cparams(sem):
    return pltpu.CompilerParams(dimension_semantics=sem, vmem_limit_bytes=VMEM_LIMIT)


def _split3(x):
    h = x.astype(BF16)
    r = x - h.astype(F32)
    m = r.astype(BF16)
    l = (r - m.astype(F32)).astype(BF16)
    return h, m, l


def _dot(a, b, dims=(((1,), (0,)), ((), ()))):
    return lax.dot_general(a, b, dims, preferred_element_type=F32)


_NT = (((1,), (1,)), ((), ()))
_TN = (((0,), (0,)), ((), ()))


def _dot_xl(x, sel, dims=(((1,), (0,)), ((), ()))):
    h, m, l = _split3(x)
    return _dot(h, sel, dims) + _dot(m, sel, dims) + _dot(l, sel, dims)


def _dot_lx(sel, x, dims=(((1,), (0,)), ((), ()))):
    h, m, l = _split3(x)
    return _dot(sel, h, dims) + _dot(sel, m, dims) + _dot(sel, l, dims)


def _dot3(a, b):
    ah = a.astype(BF16)
    al = (a - ah.astype(F32)).astype(BF16)
    bh = b.astype(BF16)
    bl = (b - bh.astype(F32)).astype(BF16)
    return _dot(jnp.concatenate([ah, ah, al], axis=1), jnp.concatenate([bh, bl, bh], axis=0))


def _sigmoid(x):
    return 1.0 / (1.0 + jnp.exp(-x))


def _silu(x):
    return x * _sigmoid(x)


def _softplus(x):
    return jnp.maximum(x, 0.0) + jnp.log1p(jnp.exp(-jnp.abs(x)))


def _in_proj_kernel(x_ref, g_ref, wb_ref, ws_ref, ob_ref, os_ref, xn_ref):
    @pl.when(pl.program_id(1) == 0)
    def _():
        x = x_ref[...]
        ms = jnp.mean(x * x, axis=-1, keepdims=True)
        xn = (x * lax.rsqrt(ms + EPS) * g_ref[...]).astype(BF16)
        xn_ref[...] = xn
        os_ref[...] = _dot(xn, ws_ref[...])

    ob_ref[...] = _dot(xn_ref[...], wb_ref[...]).astype(ob_ref.dtype)


def _in_proj(x2, gain, w_big, w_small):
    T = x2.shape[0]
    tm = min(1024, T)
    tn = 1024
    return pl.pallas_call(
        _in_proj_kernel,
        grid=(T // tm, N_BIG // tn),
        in_specs=[
            pl.BlockSpec((tm, D_MODEL), lambda i, j: (i, 0)),
            pl.BlockSpec((1, D_MODEL), lambda i, j: (0, 0)),
            pl.BlockSpec((D_MODEL, tn), lambda i, j: (0, j)),
            pl.BlockSpec((D_MODEL, LANES), lambda i, j: (0, 0)),
        ],
        out_specs=[
            pl.BlockSpec((tm, tn), lambda i, j: (i, j)),
            pl.BlockSpec((tm, LANES), lambda i, j: (i, 0)),
        ],
        out_shape=[
            jax.ShapeDtypeStruct((T, N_BIG), BF16),
            jax.ShapeDtypeStruct((T, LANES), F32),
        ],
        scratch_shapes=[pltpu.VMEM((tm, D_MODEL), BF16)],
        compiler_params=_cparams(("parallel", "arbitrary")),
        name="in_proj",
    )(x2, gain, w_big, w_small)


def _gates_kernel(s_ref, alog_ref, bias_ref, o_ref, carry_ref, *, sb):
    @pl.when(pl.program_id(1) == 0)
    def _():
        carry_ref[...] = jnp.zeros_like(carry_ref)

    x = s_ref[...] + bias_ref[...]
    lane = lax.broadcasted_iota(jnp.int32, x.shape, 1)
    e = jnp.log1p(jnp.exp(-jnp.abs(x)))
    log_alpha = -jnp.exp(alog_ref[...]) * (jnp.maximum(x, 0.0) + e)
    beta = _sigmoid(x)
    log_f = -(jnp.maximum(-x, 0.0) + e)
    is_a = lane < LANE_B
    is_b = jnp.logical_and(lane >= LANE_B, lane < LANE_F)
    is_f = jnp.logical_and(lane >= LANE_F, lane < LANE_F + FOX_HEADS)
    val = jnp.where(is_a, log_alpha, jnp.where(is_f, log_f, 0.0))
    row = lax.broadcasted_iota(jnp.int32, (sb, sb), 0)
    col = lax.broadcasted_iota(jnp.int32, (sb, sb), 1)
    tri = row >= col
    tri_full = jnp.where(tri, 1.0, 0.0).astype(BF16)
    tri_chunk = jnp.where(jnp.logical_and(tri, row // GDN_CHUNK == col // GDN_CHUNK), 1.0, 0.0).astype(BF16)
    h, m, l = _split3(val)
    cs_full = _dot(tri_full, h) + _dot(tri_full, m) + _dot(tri_full, l) + carry_ref[0:1, :]
    cs_chunk = _dot(tri_chunk, h) + _dot(tri_chunk, m) + _dot(tri_chunk, l)
    rest = jnp.where(is_f, LOG2E * cs_full, jnp.where(lane == LANE_ONE, 1.0, 0.0))
    o_ref[...] = jnp.where(is_a, cs_chunk, jnp.where(is_b, beta, rest))
    carry_ref[...] = jnp.broadcast_to(cs_full[sb - 1:sb, :], carry_ref.shape)


def _gates(small, alog_row, bias_row, B, S):
    sb = min(256, S)
    nsb = S // sb
    return pl.pallas_call(
        functools.partial(_gates_kernel, sb=sb),
        grid=(B, nsb),
        in_specs=[
            pl.BlockSpec((sb, LANES), lambda b, j: (b * nsb + j, 0)),
            pl.BlockSpec((1, LANES), lambda b, j: (0, 0)),
            pl.BlockSpec((1, LANES), lambda b, j: (0, 0)),
        ],
        out_specs=pl.BlockSpec((sb, LANES), lambda b, j: (b * nsb + j, 0)),
        out_shape=jax.ShapeDtypeStruct(small.shape, F32),
        scratch_shapes=[pltpu.VMEM((8, LANES), F32)],
        compiler_params=_cparams(("parallel", "arbitrary")),
        name="gates",
    )(small, alog_row, bias_row)


def _gdn_prep_kernel(x_ref, w_ref, o_ref, pad_ref, *, S, rb):
    j = pl.program_id(1)
    pad_ref[0:8, :] = jnp.zeros((8, LANES), F32)
    pad_ref[8:8 + S, :] = x_ref[...].astype(F32)
    is_qk = j < 2 * GDN_HEADS
    scale = jnp.where(j < GDN_HEADS, GDN_DK ** -0.5, 1.0).astype(F32)
    for r0 in range(0, S, rb):
        acc = jnp.zeros((rb, LANES), F32)
        for t in range(GDN_CONV):
            off = 8 - (GDN_CONV - 1) + t
            acc = acc + w_ref[t:t + 1, :] * pad_ref[r0 + off:r0 + off + rb, :]
        y = _silu(acc)
        n = y * lax.rsqrt(jnp.sum(y * y, axis=-1, keepdims=True) + EPS) * scale
        o_ref[r0:r0 + rb, :] = jnp.where(is_qk, n, y).astype(o_ref.dtype)


def _gdn_prep(proj, conv_w, B, S):
    T = B * S
    ncol = 3 * GDN_HEADS
    rb = min(256, S)
    return pl.pallas_call(
        functools.partial(_gdn_prep_kernel, S=S, rb=rb),
        grid=(B, ncol),
        in_specs=[
            pl.BlockSpec((S, LANES), lambda b, j: (b, COL_GQKV // LANES + j)),
            pl.BlockSpec((GDN_CONV, LANES), lambda b, j: (0, j)),
        ],
        out_specs=pl.BlockSpec((S, LANES), lambda b, j: (b, j)),
        out_shape=jax.ShapeDtypeStruct((T, ncol * LANES), BF16),
        scratch_shapes=[pltpu.VMEM((S + 8, LANES), F32)],
        compiler_params=_cparams(("parallel", "parallel")),
        name="gdn_prep",
    )(proj, conv_w)


_PAIR = 2 * GDN_CHUNK


def _inv_unit_lower(lows, blk16, eye):
    mm = lambda a, b: _dot(a.astype(BF16), b.astype(BF16))
    xs = [jnp.where(blk16, -low, 0.0) for low in lows]
    offs = [jnp.where(blk16, 0.0, low) for low in lows]
    x2 = [mm(x, x) for x in xs]
    x4 = [mm(a, a) for a in x2]
    d1 = [mm(eye + x, eye + a) for x, a in zip(xs, x2)]
    x8 = [mm(a, a) for a in x4]
    d2 = [mm(eye + a, eye + b) for a, b in zip(x4, x8)]
    dinv = [mm(a, b) for a, b in zip(d1, d2)]
    ns = [mm(d, o) for d, o in zip(dinv, offs)]
    n2 = [mm(n, n) for n in ns]
    t1 = [mm(eye - n, eye + m) for n, m in zip(ns, n2)]
    return [mm(t, d) for t, d in zip(t1, dinv)]


_GDN_HB = 4
_GDN_UNROLL = 2


def _gdn_kernel(q_ref, k_ref, v_ref, ga_ref, z_ref, gn_ref, o_ref, kw_s, c_s, qp_s, oi_s, a_s, *, S):
    hg = pl.program_id(1)
    P = _PAIR
    C = GDN_CHUNK
    row = lax.broadcasted_iota(jnp.int32, (P, P), 0)
    col = lax.broadcasted_iota(jnp.int32, (P, P), 1)
    same = row // C == col // C
    causal = jnp.logical_and(same, row >= col)
    strict = jnp.logical_and(same, row > col)
    blk16 = row // 16 == col // 16
    eye = jnp.where(row == col, 1.0, 0.0).astype(F32)
    first_half = row < C
    heads = [hg * _GDN_HB + hh for hh in range(_GDN_HB)]
    row2 = lax.broadcasted_iota(jnp.int32, (2 * LANES, 2 * LANES), 0) % LANES
    col2 = lax.broadcasted_iota(jnp.int32, (2 * LANES, 2 * LANES), 1)
    colr = lax.broadcasted_iota(jnp.int32, (P, 2 * LANES), 1) % LANES
    sel_gb = [jnp.where(row2 == jnp.where(col2 < LANES, LANE_A + h, LANE_B + h), 1.0, 0.0).astype(BF16) for h in heads]
    sel_gr = [jnp.where(colr == LANE_A + h, 1.0, 0.0).astype(BF16) for h in heads]

    def prep(it, carry):
        chains = [(it * _GDN_UNROLL + u, hh) for u in range(_GDN_UNROLL) for hh in range(_GDN_HB)]
        r0 = [pl.multiple_of(p * P, P) for p, _ in chains]
        ln = [slice(hh * LANES, (hh + 1) * LANES) for _, hh in chains]
        hh_ = [hh for _, hh in chains]
        n = len(chains)
        q = [q_ref[pl.ds(r0[i], P), ln[i]].astype(F32) for i in range(n)]
        k = [k_ref[pl.ds(r0[i], P), ln[i]].astype(F32) for i in range(n)]
        v = [v_ref[pl.ds(r0[i], P), ln[i]].astype(F32) for i in range(n)]
        ga = [ga_ref[pl.ds(r0[i], P), :] for i in range(n)]
        gah = [x.astype(BF16) for x in ga]
        gam = [(ga[i] - gah[i].astype(F32)).astype(BF16) for i in range(n)]
        ga2 = [jnp.concatenate([gah[i], gam[i]], axis=1) for i in range(n)]
        gb = [_dot(ga2[i], sel_gb[hh_[i]]) for i in range(n)]
        gcol = [x[:, :LANES] for x in gb]
        bcol = [x[:, LANES:] for x in gb]
        grow = [_dot(sel_gr[hh_[i]], ga2[i], _NT) for i in range(n)]
        kb = [x.astype(BF16) for x in k]
        kk = [_dot(x, x, _NT) for x in kb]
        qkr = [_dot(q[i].astype(BF16), kb[i], _NT) for i in range(n)]
        glast = [jnp.where(first_half, g[C - 1:C, :], g[P - 1:P, :]) for g in gcol]
        decay = [jnp.where(causal, jnp.exp(jnp.where(causal, gcol[i] - grow[i], 0.0)), 0.0) for i in range(n)]
        low = [jnp.where(strict, bcol[i] * kk[i] * decay[i], 0.0) for i in range(n)]
        tinv = _inv_unit_lower(low, blk16, eye)
        eg = [jnp.exp(g) for g in gcol]
        rhs = [jnp.concatenate([v[i] * bcol[i], k[i] * (bcol[i] * eg[i])], axis=1) for i in range(n)]
        solb = [_dot3(tinv[i], rhs[i]).astype(BF16) for i in range(n)]
        qw = [_dot((qkr[i] * decay[i]).astype(BF16), solb[i]) for i in range(n)]
        ktail = [(k[i] * jnp.exp(glast[i] - gcol[i])).astype(BF16) for i in range(n)]
        for i, (p, hh) in enumerate(chains):
            qp_s[hh, pl.ds(r0[i], P), :] = (q[i] * eg[i] - qw[i][:, LANES:]).astype(BF16)
            oi_s[hh, pl.ds(r0[i], P), :] = qw[i][:, :LANES]
            for half in range(2):
                rows = slice(half * C, (half + 1) * C)
                c = 2 * p + half
                kwc = _dot(ktail[i][rows], solb[i][rows], _TN)
                c_s[hh, c] = kwc[:, :LANES]
                kw_s[hh, c] = kwc[:, LANES:].astype(BF16)
                a_s[hh, c] = jnp.broadcast_to(jnp.exp(glast[i][half * C:half * C + 1, :]), (8, LANES))
        return carry

    lax.fori_loop(0, S // (P * _GDN_UNROLL), prep, 0)

    def step(c, states):
        r0 = pl.multiple_of(c * C, C)
        hs = range(_GDN_HB)
        sb = [states[hh].astype(BF16) for hh in hs]
        ks = [_dot(kw_s[hh, c], sb[hh]) for hh in hs]
        os_ = [_dot(qp_s[hh, pl.ds(r0, C), :], sb[hh]) for hh in hs]
        new = [a_s[hh, c][0:1, :] * states[hh] - ks[hh] + c_s[hh, c] for hh in hs]
        for hh in hs:
            lanes = slice(hh * LANES, (hh + 1) * LANES)
            o = os_[hh] + oi_s[hh, pl.ds(r0, C), :]
            on = o * lax.rsqrt(jnp.mean(o * o, axis=-1, keepdims=True) + EPS) * gn_ref[...]
            z = z_ref[pl.ds(r0, C), lanes].astype(F32)
            o_ref[pl.ds(r0, C), lanes] = (on * _silu(z)).astype(o_ref.dtype)
        return tuple(new)

    lax.fori_loop(0, S // C, step, tuple(jnp.zeros((GDN_DK, LANES), F32) for _ in range(_GDN_HB)))


def _gdn(qkvn, gact, proj, gnorm, B, S):
    T = B * S
    nc = S // GDN_CHUNK
    hb = _GDN_HB
    ng = GDN_HEADS // hb
    w = hb * LANES
    blk = lambda off: pl.BlockSpec((S, w), lambda b, h: (b, off + h))
    return pl.pallas_call(
        functools.partial(_gdn_kernel, S=S),
        grid=(B, ng),
        in_specs=[
            blk(0), blk(ng), blk(2 * ng),
            pl.BlockSpec((S, LANES), lambda b, h: (b, 0)),
            blk(COL_GZ // w),
            pl.BlockSpec((1, LANES), lambda b, h: (0, 0)),
        ],
        out_specs=pl.BlockSpec((S, w), lambda b, h: (b, h)),
        out_shape=jax.ShapeDtypeStruct((T, GDN_HEADS * LANES), BF16),
        scratch_shapes=[
            pltpu.VMEM((hb, nc, GDN_DK, LANES), BF16),
            pltpu.VMEM((hb, nc, GDN_DK, LANES), F32),
            pltpu.VMEM((hb, S, LANES), BF16),
            pltpu.VMEM((hb, S, LANES), F32),
            pltpu.VMEM((hb, nc, 8, LANES), F32),
        ],
        compiler_params=_cparams(("parallel", "parallel")),
        name="gdn",
    )(qkvn, qkvn, qkvn, gact, proj, gnorm)


AUG0 = FOX_HD
_NPAIR = FOX_HEADS * FOX_HD // LANES


def _fox_aug_selectors():
    import numpy as np
    sel = np.zeros((2 * _NPAIR, 3 * LANES, 2 * LANES), np.float32)
    for j in range(2 * _NPAIR):
        is_q = j < _NPAIR
        for slot in range(2):
            head = 2 * (j % _NPAIR) + slot
            c_lane, one_lane, sign = (AUG0, AUG0 + 3, 1.0) if is_q else (AUG0 + 3, AUG0, -1.0)
            for piece in range(3):
                sel[j, piece * LANES + LANE_F + head, slot * LANES + c_lane + piece] = sign
                sel[j, LANE_ONE, slot * LANES + one_lane + piece] = 1.0
    return jnp.asarray(sel, BF16)


def _fox_prep_kernel(x_ref, ga_ref, g_ref, sel_ref, o_ref):
    is_q = pl.program_id(1) < _NPAIR
    x = x_ref[...].astype(F32)
    row = lax.broadcasted_iota(jnp.int32, (LANES, LANES), 0)
    col = lax.broadcasted_iota(jnp.int32, (LANES, LANES), 1)
    grp = jnp.where(row // FOX_HD == col // FOX_HD, 1.0, 0.0).astype(BF16)
    xx = x * x
    hi = xx.astype(BF16)
    lo = (xx - hi.astype(F32)).astype(BF16)
    ms = _dot(jnp.concatenate([hi, lo], axis=1), jnp.concatenate([grp, grp], axis=0)) * (1.0 / FOX_HD)
    scale = jnp.where(is_q, LOG2E * FOX_HD ** -0.5, 1.0).astype(F32)
    xn = x * lax.rsqrt(ms + EPS) * (g_ref[0] * scale)
    xr = pltpu.roll(xn, FOX_HD, axis=1)
    aug = _dot(jnp.concatenate(_split3(ga_ref[...]), axis=1), sel_ref[0])
    lane = lax.broadcasted_iota(jnp.int32, x.shape, 1)
    o_ref[:, :LANES] = jnp.where(lane < FOX_HD, xn, aug[:, :LANES]).astype(o_ref.dtype)
    o_ref[:, LANES:] = jnp.where(lane < FOX_HD, xr, aug[:, LANES:]).astype(o_ref.dtype)


def _fox_prep(proj, gact, gains, T):
    tm = min(1024, T)
    ncol = 2 * _NPAIR
    return pl.pallas_call(
        _fox_prep_kernel,
        grid=(T // tm, ncol),
        in_specs=[
            pl.BlockSpec((tm, LANES), lambda i, j: (i, COL_FQKV // LANES + j)),
            pl.BlockSpec((tm, LANES), lambda i, j: (i, 0)),
            pl.BlockSpec((1, 1, LANES), lambda i, j: (j, 0, 0)),
            pl.BlockSpec((1, 3 * LANES, 2 * LANES), lambda i, j: (j, 0, 0)),
        ],
        out_specs=pl.BlockSpec((tm, 2 * LANES), lambda i, j: (i, j)),
        out_shape=jax.ShapeDtypeStruct((T, ncol * 2 * LANES), BF16),
        compiler_params=_cparams(("parallel", "arbitrary")),
        name="fox_prep",
    )(proj, gact, gains, _fox_aug_selectors())


def _vt_kernel(v_ref, o_ref):
    o_ref[...] = jnp.transpose(v_ref[...].astype(F32)).astype(o_ref.dtype)


def _fox_vt(proj, T):
    tm = min(2048, T)
    nhp = FOX_HEADS // 2
    return pl.pallas_call(
        _vt_kernel,
        grid=(T // tm, nhp),
        in_specs=[pl.BlockSpec((tm, LANES), lambda i, h: (i, COL_FQKV // LANES + 2 * nhp + h))],
        out_specs=pl.BlockSpec((LANES, tm), lambda i, h: (h, i)),
        out_shape=jax.ShapeDtypeStruct((nhp * LANES, T), BF16),
        compiler_params=_cparams(("parallel", "parallel")),
        name="fox_vt",
    )(proj)


def _fox_kernel(q_ref, k_ref, vt_ref, o_ref, *, tq):
    i = pl.program_id(2)
    hs = range(2)
    q = [q_ref[:, hh * LANES:(hh + 1) * LANES] for hh in hs]
    keyi = lax.broadcasted_iota(jnp.int32, (tq, tq), 0)
    qryi = lax.broadcasted_iota(jnp.int32, (tq, tq), 1)
    tri = keyi <= qryi

    def kv_step(jb, carry, masked):
        ms, ls, accs = carry
        r0 = pl.multiple_of(jb * tq, tq)
        vt = [vt_ref[hh * FOX_HD:(hh + 1) * FOX_HD, pl.ds(r0, tq)] for hh in hs]
        s = [_dot(k_ref[pl.ds(r0, tq), hh * LANES:(hh + 1) * LANES], q[hh], _NT) for hh in hs]
        if masked:
            s = [jnp.where(tri, x, -jnp.inf) for x in s]
        m_new = [jnp.maximum(ms[hh], jnp.max(s[hh], axis=0, keepdims=True)) for hh in hs]
        pr = [jnp.exp2(s[hh] - m_new[hh]) for hh in hs]
        alpha = [jnp.exp2(ms[hh] - m_new[hh]) for hh in hs]
        pv = [_dot(vt[hh], pr[hh].astype(BF16)) for hh in hs]
        ls = [alpha[hh] * ls[hh] + jnp.sum(pr[hh], axis=0, keepdims=True) for hh in hs]
        accs = [alpha[hh] * accs[hh] + pv[hh] for hh in hs]
        return m_new, ls, accs

    init = ([jnp.full((1, tq), -jnp.inf, F32) for _ in hs], [jnp.zeros((1, tq), F32) for _ in hs],
            [jnp.zeros((FOX_HD, tq), F32) for _ in hs])
    carry = lax.fori_loop(0, i, lambda jb, c: kv_step(jb, c, False), init)
    _, ls, accs = kv_step(i, carry, True)
    ot = jnp.concatenate([accs[hh] / ls[hh] for hh in hs], axis=0)
    o_ref[...] = jnp.transpose(ot).astype(o_ref.dtype)


def _fox(qka, vt, B, S):
    T = B * S
    tq = min(1024, S)
    nq = S // tq
    nhp = FOX_HEADS // 2
    return pl.pallas_call(
        functools.partial(_fox_kernel, tq=tq),
        grid=(B, nhp, nq),
        in_specs=[
            pl.BlockSpec((tq, 2 * LANES), lambda b, h, i: (b * nq + i, h)),
            pl.BlockSpec((S, 2 * LANES), lambda b, h, i: (b, nhp + h)),
            pl.BlockSpec((LANES, S), lambda b, h, i: (h, b)),
        ],
        out_specs=pl.BlockSpec((tq, LANES), lambda b, h, i: (b * nq + i, h)),
        out_shape=jax.ShapeDtypeStruct((T, nhp * LANES), BF16),
        compiler_params=_cparams(("parallel", "parallel", "arbitrary")),
        name="fox",
    )(qka, qka, vt)


HALF = D_MODEL // 2


def _rms(x, gain):
    return x * lax.rsqrt(jnp.mean(x * x, axis=-1, keepdims=True) + EPS) * gain


def _pack_halves(x):
    lo = lax.bitcast_convert_type(x[:, :HALF].astype(BF16).astype(F32), jnp.uint32)
    hi = lax.bitcast_convert_type(x[:, HALF:].astype(BF16).astype(F32), jnp.uint32)
    return lax.shift_right_logical(lo, jnp.uint32(16)) | (hi & jnp.uint32(0xFFFF0000))


def _unpack_halves(w):
    lo = lax.bitcast_convert_type(lax.shift_left(w, jnp.uint32(16)), F32)
    hi = lax.bitcast_convert_type(w & jnp.uint32(0xFFFF0000), F32)
    return lo, hi


def _mix_kernel(oa_ref, of_ref, ma_ref, mf_ref, x_ref, wa_ref, wf_ref, wo_ref, g_ref, h_ref, hp_ref):
    ya = _dot(oa_ref[...], wa_ref[...])
    yf = _dot(of_ref[...], wf_ref[...])
    m = _sigmoid(ma_ref[...].astype(F32)) * ya + _sigmoid(mf_ref[...].astype(F32)) * yf
    h = x_ref[...] + _dot(m.astype(BF16), wo_ref[...])
    h_ref[...] = h
    hp_ref[...] = _pack_halves(_rms(h, g_ref[...]))


def _mix(o_a, o_f, proj, x2, wa, wf, wo, gain):
    T = x2.shape[0]
    tm = min(512, T)
    row = lambda c: pl.BlockSpec((tm, D_MODEL), lambda i: (i, c))
    full = pl.BlockSpec((D_MODEL, D_MODEL), lambda i: (0, 0))
    return pl.pallas_call(
        _mix_kernel,
        grid=(T // tm,),
        in_specs=[row(0), row(0), row(COL_MERGE // D_MODEL), row(COL_MERGE // D_MODEL + 1), row(0),
                  full, full, full, pl.BlockSpec((1, D_MODEL), lambda i: (0, 0))],
        out_specs=[row(0), pl.BlockSpec((tm, HALF), lambda i: (i, 0))],
        out_shape=[jax.ShapeDtypeStruct((T, D_MODEL), F32), jax.ShapeDtypeStruct((T, HALF), jnp.uint32)],
        compiler_params=_cparams(("parallel",)),
        name="mix_out",
    )(o_a, o_f, proj, proj, x2, wa, wf, wo, gain)


def _router_kernel(h_ref, g_ref, wr_ref, bias_ref, gates_ref, sel_ref, cnt_ref):
    @pl.when(pl.program_id(0) == 0)
    def _():
        cnt_ref[...] = jnp.zeros_like(cnt_ref)

    hn = _rms(h_ref[...], g_ref[...])
    hh = hn.astype(BF16)
    hl = (hn - hh.astype(F32)).astype(BF16)
    logits = _dot(wr_ref[0], hh, _NT) + _dot(wr_ref[0], hl, _NT) + _dot(wr_ref[1], hh, _NT)
    scores = _sigmoid(logits)
    work = scores + bias_ref[...]
    eidx = lax.broadcasted_iota(jnp.int32, work.shape, 0)
    sel = jnp.zeros(work.shape, F32)
    for _ in range(TOP_K):
        mx = jnp.max(work, axis=0, keepdims=True)
        first = jnp.min(jnp.where(work == mx, eidx, N_EXPERTS), axis=0, keepdims=True)
        onehot = eidx == first
        sel = jnp.where(onehot, 1.0, sel)
        work = jnp.where(onehot, -jnp.inf, work)
    s = jnp.where(sel > 0.0, scores, 0.0)
    gates_ref[...] = s / jnp.sum(s, axis=0, keepdims=True) * ROUTED_SCALE
    sel_ref[...] = sel
    cnt_ref[...] += jnp.broadcast_to(jnp.sum(sel, axis=1, keepdims=True), cnt_ref.shape)


def _router(h1, gain, wr2, bias_col):
    T = h1.shape[0]
    tm = min(512, T)
    return pl.pallas_call(
        _router_kernel,
        grid=(T // tm,),
        in_specs=[
            pl.BlockSpec((tm, D_MODEL), lambda i: (i, 0)),
            pl.BlockSpec((1, D_MODEL), lambda i: (0, 0)),
            pl.BlockSpec((2, N_EXPERTS, D_MODEL), lambda i: (0, 0, 0)),
            pl.BlockSpec((N_EXPERTS, 1), lambda i: (0, 0)),
        ],
        out_specs=[
            pl.BlockSpec((N_EXPERTS, tm), lambda i: (0, i)),
            pl.BlockSpec((N_EXPERTS, tm), lambda i: (0, i)),
            pl.BlockSpec((N_EXPERTS, LANES), lambda i: (0, 0)),
        ],
        out_shape=[
            jax.ShapeDtypeStruct((N_EXPERTS, T), F32),
            jax.ShapeDtypeStruct((N_EXPERTS, T), F32),
            jax.ShapeDtypeStruct((N_EXPERTS, LANES), F32),
        ],
        compiler_params=_cparams(("arbitrary",)),
        name="router",
    )(h1, gain, wr2, bias_col)


def _pos_kernel(sel_ref, gates_ref, cnt_ref, pos_ref, w_ref, base_s, carry_s, *, tm, tme):
    @pl.when(pl.program_id(0) == 0)
    def _():
        padded = jnp.floor((cnt_ref[...] + (tme - 1)) * (1.0 / tme)) * tme
        r = lax.broadcasted_iota(jnp.int32, (N_EXPERTS, N_EXPERTS), 0)
        c = lax.broadcasted_iota(jnp.int32, (N_EXPERTS, N_EXPERTS), 1)
        base_s[...] = _dot_lx(jnp.where(r > c, 1.0, 0.0).astype(BF16), padded)
        carry_s[...] = jnp.zeros_like(carry_s)

    sel = sel_ref[...]
    gates = gates_ref[...]
    r = lax.broadcasted_iota(jnp.int32, (tm, tm), 0)
    c = lax.broadcasted_iota(jnp.int32, (tm, tm), 1)
    cs = _dot(sel.astype(BF16), jnp.where(r <= c, 1.0, 0.0).astype(BF16))
    start = base_s[...] + carry_s[...]
    posd = cs - sel + jnp.concatenate([start] * (tm // LANES), axis=1)
    carry_s[...] += jnp.broadcast_to(jnp.sum(sel, axis=1, keepdims=True), carry_s.shape)
    eidx = lax.broadcasted_iota(jnp.int32, sel.shape, 0)
    rem = sel
    ps, ws = [], []
    for _ in range(TOP_K):
        first = jnp.min(jnp.where(rem > 0.0, eidx, N_EXPERTS), axis=0, keepdims=True)
        onehot = eidx == first
        ps.append(jnp.sum(jnp.where(onehot, posd, 0.0), axis=0, keepdims=True))
        ws.append(jnp.sum(jnp.where(onehot, gates, 0.0), axis=0, keepdims=True))
        rem = jnp.where(onehot, 0.0, rem)
    pos_ref[...] = jnp.concatenate(ps, axis=0).astype(jnp.int32)
    w_ref[...] = jnp.concatenate(ws, axis=0)


def _positions(sel, gates, cnt, tme):
    T = sel.shape[1]
    tm = min(512, T)
    blk = pl.BlockSpec((N_EXPERTS, tm), lambda i: (0, i))
    return pl.pallas_call(
        functools.partial(_pos_kernel, tm=tm, tme=tme),
        grid=(T // tm,),
        in_specs=[blk, blk, pl.BlockSpec((N_EXPERTS, LANES), lambda i: (0, 0))],
        out_specs=[pl.BlockSpec((TOP_K, tm), lambda i: (0, i)), pl.BlockSpec((TOP_K, tm), lambda i: (0, i))],
        out_shape=[jax.ShapeDtypeStruct((TOP_K, T), jnp.int32), jax.ShapeDtypeStruct((TOP_K, T), F32)],
        scratch_shapes=[pltpu.VMEM((N_EXPERTS, LANES), F32), pltpu.VMEM((N_EXPERTS, LANES), F32)],
        compiler_params=_cparams(("arbitrary",)),
        name="moe_positions",
    )(sel, gates, cnt)


def _dispatch_kernel(meta_ref, pos_ref, hp_ref, xs_ref, zero_s, sem, zsem, *, tmd, tme, n_tiles):
    i = pl.program_id(0)

    @pl.when(i == 0)
    def _():
        zero_s[...] = jnp.zeros_like(zero_s)
        for e in range(N_EXPERTS):
            pltpu.make_async_copy(zero_s, xs_ref.at[pl.ds(pl.multiple_of(meta_ref[e], 8), tme)], zsem).start()
        for e in range(N_EXPERTS):
            pltpu.make_async_copy(zero_s, xs_ref.at[pl.ds(pl.multiple_of(meta_ref[e], 8), tme)], zsem).wait()

        def zero_tile(t, carry):
            cp = pltpu.make_async_copy(zero_s, xs_ref.at[pl.ds(pl.multiple_of(t * tme, tme), tme)], zsem)
            cp.start()
            cp.wait()
            return carry

        lax.fori_loop(meta_ref[N_EXPERTS], n_tiles, zero_tile, 0)

    def issue(tt, carry):
        base = pl.multiple_of(tt * 8, 8)
        for j in range(8):
            src = hp_ref.at[pl.ds(base + j, 1)]
            for k in range(TOP_K):
                pltpu.make_async_copy(src, xs_ref.at[pl.ds(pos_ref[0, 0, k * tmd + base + j], 1)], sem).start()
        return carry

    lax.fori_loop(0, tmd // 8, issue, 0)
    pltpu.make_async_copy(xs_ref.at[pl.ds(0, TOP_K * tmd)], xs_ref.at[pl.ds(0, TOP_K * tmd)], sem).wait()


def _dispatch(meta, pos_tiles, hp, n_tiles, tmd, tme):
    T = hp.shape[0]
    return pl.pallas_call(
        functools.partial(_dispatch_kernel, tmd=tmd, tme=tme, n_tiles=n_tiles),
        grid_spec=pltpu.PrefetchScalarGridSpec(
            num_scalar_prefetch=1,
            grid=(T // tmd,),
            in_specs=[
                pl.BlockSpec((1, 1, TOP_K * tmd), lambda i, f: (i, 0, 0), memory_space=pltpu.SMEM),
                pl.BlockSpec((tmd, HALF), lambda i, f: (i, 0)),
            ],
            out_specs=pl.BlockSpec(memory_space=pl.ANY),
            scratch_shapes=[pltpu.VMEM((tme, HALF), jnp.uint32), pltpu.SemaphoreType.DMA, pltpu.SemaphoreType.DMA],
        ),
        out_shape=jax.ShapeDtypeStruct((n_tiles * tme, HALF), jnp.uint32),
        compiler_params=_cparams(("arbitrary",)),
        name="moe_dispatch",
    )(meta, pos_tiles, hp)


def _expert_kernel(te_ref, nu_ref, xs_ref, wgu_ref, wd_ref, ys_ref):
    @pl.when(pl.program_id(0) < nu_ref[0])
    def _():
        lo, hi = _unpack_halves(xs_ref[...])
        gu = _dot(lo.astype(BF16), wgu_ref[0, :HALF, :]) + _dot(hi.astype(BF16), wgu_ref[0, HALF:, :])
        hid = _silu(gu[:, :D_EXPERT]) * gu[:, D_EXPERT:]
        ys_ref[...] = _pack_halves(_dot(hid.astype(BF16), wd_ref[0]))

    @pl.when(pl.program_id(0) >= nu_ref[0])
    def _():
        ys_ref[...] = jnp.zeros_like(ys_ref)


def _experts(tile_e, n_used, xs, wgu, wd, n_tiles, tme):
    clamp = lambda i, te, nu: (jnp.minimum(i, nu[0] - 1), 0)
    return pl.pallas_call(
        _expert_kernel,
        grid_spec=pltpu.PrefetchScalarGridSpec(
            num_scalar_prefetch=2,
            grid=(n_tiles,),
            in_specs=[
                pl.BlockSpec((tme, HALF), clamp),
                pl.BlockSpec((1, D_MODEL, 2 * D_EXPERT), lambda i, te, nu: (te[i], 0, 0)),
                pl.BlockSpec((1, D_EXPERT, D_MODEL), lambda i, te, nu: (te[i], 0, 0)),
            ],
            out_specs=pl.BlockSpec((tme, HALF), lambda i, te, nu: (i, 0)),
        ),
        out_shape=jax.ShapeDtypeStruct((n_tiles * tme, HALF), jnp.uint32),
        compiler_params=_cparams(("arbitrary",)),
        name="moe_experts",
    )(tile_e, n_used, xs, wgu, wd)


def _tail_kernel(pos_ref, ys_ref, h_ref, w_ref, p_ref, gf_ref, wsgu_ref, wsd_ref, wple_ref, gple_ref, gpg_ref,
                 wpg_ref, o_ref, buf, sem, *, tmc):
    def issue(tt, carry):
        base = pl.multiple_of(tt * 8, 8)
        for j in range(8):
            for k in range(TOP_K):
                pltpu.make_async_copy(ys_ref.at[pl.ds(pos_ref[0, 0, k * tmc + base + j], 1)],
                                      buf.at[k, pl.ds(base + j, 1)], sem).start()
        return carry

    lax.fori_loop(0, tmc // 8, issue, 0)
    h = h_ref[...]
    hn = _rms(h, gf_ref[...]).astype(BF16)
    sgu = _dot(hn, wsgu_ref[...])
    shared = _dot((_silu(sgu[:, :D_SHARED]) * sgu[:, D_SHARED:]).astype(BF16), wsd_ref[...])
    e = _rms(_dot(p_ref[...].astype(BF16), wple_ref[...]), gple_ref[...])
    pltpu.make_async_copy(buf, buf, sem).wait()
    acc_lo = jnp.zeros((tmc, HALF), F32)
    acc_hi = jnp.zeros((tmc, HALF), F32)
    for k in range(TOP_K):
        lo, hi = _unpack_halves(buf[k])
        wk = w_ref[:, k:k + 1]
        acc_lo = acc_lo + wk * lo
        acc_hi = acc_hi + wk * hi
    h2 = h + jnp.concatenate([acc_lo, acc_hi], axis=1) + shared
    g = _sigmoid(_dot(_rms(h2, gpg_ref[...]).astype(BF16), wpg_ref[...]))
    o_ref[...] = h2 + g * e


def _tail(pos_tiles, ys, h1, w_tok, p2, gf, wsgu, wsd, wple, gple, gpg, wpg, tmc):
    T = h1.shape[0]
    row = lambda n: pl.BlockSpec((tmc, n), lambda i: (i, 0))
    full = lambda a: pl.BlockSpec(a.shape, lambda i: (0,) * a.ndim)
    return pl.pallas_call(
        functools.partial(_tail_kernel, tmc=tmc),
        grid=(T // tmc,),
        in_specs=[
            pl.BlockSpec((1, 1, TOP_K * tmc), lambda i: (i, 0, 0), memory_space=pltpu.SMEM),
            pl.BlockSpec(memory_space=pl.ANY),
            row(D_MODEL), row(TOP_K), row(PLE_DIM),
            full(gf), full(wsgu), full(wsd), full(wple), full(gple), full(gpg), full(wpg),
        ],
        out_specs=row(D_MODEL),
        out_shape=jax.ShapeDtypeStruct((T, D_MODEL), F32),
        scratch_shapes=[pltpu.VMEM((TOP_K, tmc, HALF), jnp.uint32), pltpu.SemaphoreType.DMA],
        compiler_params=_cparams(("arbitrary",)),
        name="moe_combine_ple",
    )(pos_tiles, ys, h1, w_tok, p2, gf, wsgu, wsd, wple, gple, gpg, wpg)


def _stages(x, p, norm_mix, w_in, conv_w, a_log, dt_bias, gdn_norm, fox_f_bias, q_norm, k_norm,
            w_branch_gdn, w_branch_fox, w_out, norm_ffn, w_router, router_bias,
            we_gate, we_up, we_down, ws_gate, ws_up, ws_down, w_ple, ple_norm, ple_gate_norm, w_ple_gate):
    B, S, _ = x.shape
    T = B * S
    st = {}
    wi = w_in[0]
    o0 = 3 * GDN_HEADS * GDN_DK
    o1 = o0 + GDN_HEADS * GDN_DK
    o2 = o1 + GDN_HEADS
    o3 = o2 + GDN_HEADS
    o4 = o3 + 3 * FOX_HEADS * FOX_HD
    o5 = o4 + FOX_HEADS
    w_big = jnp.concatenate([wi[:, :o1], wi[:, o3:o4], wi[:, o5:]], axis=1).astype(BF16)
    w_small = jnp.concatenate([wi[:, o1:o3], wi[:, o4:o5],
                               jnp.zeros((D_MODEL, LANES - 2 * GDN_HEADS - FOX_HEADS), F32)], axis=1).astype(BF16)
    proj, small = _in_proj(x.reshape(T, D_MODEL), norm_mix[0].reshape(1, D_MODEL), w_big, w_small)
    st["proj_big"] = proj
    st["small"] = small

    pad = lambda v, off: jnp.zeros((1, LANES), F32).at[0, off:off + v.shape[0]].set(v)
    alog_row = pad(a_log[0], LANE_A)
    bias_row = pad(dt_bias[0], LANE_A) + pad(fox_f_bias[0], LANE_F)
    gact = _gates(small, alog_row, bias_row, B, S)
    st["gact"] = gact

    qkvn = _gdn_prep(proj, conv_w[0], B, S)
    st["qkvn"] = qkvn
    o_a = _gdn(qkvn, gact, proj, gdn_norm[0].reshape(1, LANES), B, S)
    st["o_a"] = o_a

    nrep = FOX_HEADS * FOX_HD // LANES
    gains = jnp.concatenate([jnp.tile(q_norm[0], (nrep, LANES // FOX_HD)),
                             jnp.tile(k_norm[0], (nrep, LANES // FOX_HD))], axis=0).reshape(2 * nrep, 1, LANES)
    qka = _fox_prep(proj, gact, gains, T)
    o_f = _fox(qka, _fox_vt(proj, T), B, S)
    st["o_f"] = o_f

    x2 = x.reshape(T, D_MODEL)
    gf = norm_ffn[0].reshape(1, D_MODEL)
    h1, hp = _mix(o_a, o_f, proj, x2, w_branch_gdn[0].astype(BF16), w_branch_fox[0].astype(BF16),
                  w_out[0].astype(BF16), gf)
    st["h1"] = h1

    wrt = w_router[0].T
    wr_hi = wrt.astype(BF16)
    wr2 = jnp.stack([wr_hi, (wrt - wr_hi.astype(F32)).astype(BF16)])
    gates, sel, cnt = _router(h1, gf, wr2, router_bias[0].reshape(N_EXPERTS, 1))
    st["gates_t"] = gates

    tme = 512 if T * TOP_K // N_EXPERTS >= 2048 else 64
    pos, wts = _positions(sel, gates, cnt, tme)
    cnt1 = cnt[:, 0].astype(jnp.int32)
    padded = (cnt1 + (tme - 1)) // tme * tme
    ends = jnp.cumsum(padded)
    n_tiles = T * TOP_K // tme + N_EXPERTS
    tile_start = jnp.arange(n_tiles, dtype=jnp.int32) * tme
    tile_e = jnp.minimum(jnp.sum((ends[None, :] <= tile_start[:, None]).astype(jnp.int32), axis=1), N_EXPERTS - 1)
    n_used = (ends[-1:] // tme).astype(jnp.int32)
    fill = ((ends - padded + cnt1) // 8 * 8).astype(jnp.int32)
    tmd = min(256, T)
    pos_tiles = pos.reshape(TOP_K, T // tmd, tmd).transpose(1, 0, 2).reshape(T // tmd, 1, TOP_K * tmd)
    xs = _dispatch(jnp.concatenate([fill, n_used]), pos_tiles, hp, n_tiles, tmd, tme)
    wgu = jnp.concatenate([we_gate[0], we_up[0]], axis=2).astype(BF16)
    ys = _experts(tile_e, n_used, xs, wgu, we_down[0].astype(BF16), n_tiles, tme)

    row = lambda v: v.reshape(1, D_MODEL)
    wsgu = jnp.concatenate([ws_gate[0], ws_up[0]], axis=1).astype(BF16)
    out = _tail(pos_tiles, ys, h1, wts.T, p[0].reshape(T, PLE_DIM), gf, wsgu, ws_down[0].astype(BF16),
                w_ple[0].astype(BF16), row(ple_norm[0]), row(ple_gate_norm[0]), w_ple_gate[0].astype(BF16), tmd)
    st["out"] = out.reshape(B, S, D_MODEL)
    return st


def kernel(x, p, norm_mix, w_in, conv_w, a_log, dt_bias, gdn_norm, fox_f_bias, q_norm, k_norm, w_branch_gdn, w_branch_fox, w_out, norm_ffn, w_router, router_bias, we_gate, we_up, we_down, ws_gate, ws_up, ws_down, w_ple, ple_norm, ple_gate_norm, w_ple_gate):
    return _stages(x, p, norm_mix, w_in, conv_w, a_log, dt_bias, gdn_norm, fox_f_bias, q_norm, k_norm,
                   w_branch_gdn, w_branch_fox, w_out, norm_ffn, w_router, router_bias,
                   we_gate, we_up, we_down, ws_gate, ws_up, ws_down, w_ple, ple_norm, ple_gate_norm,
                   w_ple_gate)["out"]
```

```python
import functools

import jax
import jax.numpy as jnp
from jax import lax
from jax.experimental import pallas as pl
from jax.experimental.pallas import tpu as pltpu

F32 = jnp.float32
BF16 = jnp.bfloat16

D_MODEL = 1024
PLE_DIM = 256
EPS = 1e-6
GDN_HEADS = 8
GDN_DK = 128
GDN_CONV = 4
GDN_CHUNK = 64
FOX_HEADS = 16
FOX_HD = 64
N_EXPERTS = 64
TOP_K = 8
D_EXPERT = 256
D_SHARED = 256
ROUTED_SCALE = 2.5

LANES = 128
COL_GQKV = 0
COL_GZ = 3072
COL_FQKV = 4096
COL_MERGE = 7168
N_BIG = 9216
LANE_A = 0
LANE_B = 8
LANE_F = 16
LANE_ONE = 127
LOG2E = 1.4426950408889634

VMEM_LIMIT = 48 * 1024 * 1024


def _cparams(sem):
    return pltpu.CompilerParams(dimension_semantics=sem, vmem_limit_bytes=VMEM_LIMIT)


def _split3(x):
    h = x.astype(BF16)
    r = x - h.astype(F32)
    m = r.astype(BF16)
    l = (r - m.astype(F32)).astype(BF16)
    return h, m, l


def _dot(a, b, dims=(((1,), (0,)), ((), ()))):
    return lax.dot_general(a, b, dims, preferred_element_type=F32)


_NT = (((1,), (1,)), ((), ()))
_TN = (((0,), (0,)), ((), ()))


def _dot_xl(x, sel, dims=(((1,), (0,)), ((), ()))):
    h, m, l = _split3(x)
    return _dot(h, sel, dims) + _dot(m, sel, dims) + _dot(l, sel, dims)


def _dot_lx(sel, x, dims=(((1,), (0,)), ((), ()))):
    h, m, l = _split3(x)
    return _dot(sel, h, dims) + _dot(sel, m, dims) + _dot(sel, l, dims)


def _dot3(a, b):
    ah = a.astype(BF16)
    al = (a - ah.astype(F32)).astype(BF16)
    bh = b.astype(BF16)
    bl = (b - bh.astype(F32)).astype(BF16)
    return _dot(jnp.concatenate([ah, ah, al], axis=1), jnp.concatenate([bh, bl, bh], axis=0))


def _sigmoid(x):
    return 1.0 / (1.0 + jnp.exp(-x))


def _silu(x):
    return x * _sigmoid(x)


def _softplus(x):
    return jnp.maximum(x, 0.0) + jnp.log1p(jnp.exp(-jnp.abs(x)))


def _in_proj_kernel(x_ref, g_ref, wb_ref, ws_ref, ob_ref, os_ref, xn_ref):
    @pl.when(pl.program_id(1) == 0)
    def _():
        x = x_ref[...]
        ms = jnp.mean(x * x, axis=-1, keepdims=True)
        xn = (x * lax.rsqrt(ms + EPS) * g_ref[...]).astype(BF16)
        xn_ref[...] = xn
        os_ref[...] = _dot(xn, ws_ref[...])

    ob_ref[...] = _dot(xn_ref[...], wb_ref[...]).astype(ob_ref.dtype)


def _in_proj(x2, gain, w_big, w_small):
    T = x2.shape[0]
    tm = min(1024, T)
    tn = 1024
    return pl.pallas_call(
        _in_proj_kernel,
        grid=(T // tm, N_BIG // tn),
        in_specs=[
            pl.BlockSpec((tm, D_MODEL), lambda i, j: (i, 0)),
            pl.BlockSpec((1, D_MODEL), lambda i, j: (0, 0)),
            pl.BlockSpec((D_MODEL, tn), lambda i, j: (0, j)),
            pl.BlockSpec((D_MODEL, LANES), lambda i, j: (0, 0)),
        ],
        out_specs=[
            pl.BlockSpec((tm, tn), lambda i, j: (i, j)),
            pl.BlockSpec((tm, LANES), lambda i, j: (i, 0)),
        ],
        out_shape=[
            jax.ShapeDtypeStruct((T, N_BIG), BF16),
            jax.ShapeDtypeStruct((T, LANES), F32),
        ],
        scratch_shapes=[pltpu.VMEM((tm, D_MODEL), BF16)],
        compiler_params=_cparams(("parallel", "arbitrary")),
        name="in_proj",
    )(x2, gain, w_big, w_small)


def _gates_kernel(s_ref, alog_ref, bias_ref, o_ref, carry_ref, *, sb):
    @pl.when(pl.program_id(1) == 0)
    def _():
        carry_ref[...] = jnp.zeros_like(carry_ref)

    x = s_ref[...] + bias_ref[...]
    lane = lax.broadcasted_iota(jnp.int32, x.shape, 1)
    e = jnp.log1p(jnp.exp(-jnp.abs(x)))
    log_alpha = -jnp.exp(alog_ref[...]) * (jnp.maximum(x, 0.0) + e)
    beta = _sigmoid(x)
    log_f = -(jnp.maximum(-x, 0.0) + e)
    is_a = lane < LANE_B
    is_b = jnp.logical_and(lane >= LANE_B, lane < LANE_F)
    is_f = jnp.logical_and(lane >= LANE_F, lane < LANE_F + FOX_HEADS)
    val = jnp.where(is_a, log_alpha, jnp.where(is_f, log_f, 0.0))
    row = lax.broadcasted_iota(jnp.int32, (sb, sb), 0)
    col = lax.broadcasted_iota(jnp.int32, (sb, sb), 1)
    tri = row >= col
    tri_full = jnp.where(tri, 1.0, 0.0).astype(BF16)
    tri_chunk = jnp.where(jnp.logical_and(tri, row // GDN_CHUNK == col // GDN_CHUNK), 1.0, 0.0).astype(BF16)
    h, m, l = _split3(val)
    cs_full = _dot(tri_full, h) + _dot(tri_full, m) + _dot(tri_full, l) + carry_ref[0:1, :]
    cs_chunk = _dot(tri_chunk, h) + _dot(tri_chunk, m) + _dot(tri_chunk, l)
    rest = jnp.where(is_f, LOG2E * cs_full, jnp.where(lane == LANE_ONE, 1.0, 0.0))
    o_ref[...] = jnp.where(is_a, cs_chunk, jnp.where(is_b, beta, rest))
    carry_ref[...] = jnp.broadcast_to(cs_full[sb - 1:sb, :], carry_ref.shape)


def _gates(small, alog_row, bias_row, B, S):
    sb = min(256, S)
    nsb = S // sb
    return pl.pallas_call(
        functools.partial(_gates_kernel, sb=sb),
        grid=(B, nsb),
        in_specs=[
            pl.BlockSpec((sb, LANES), lambda b, j: (b * nsb + j, 0)),
            pl.BlockSpec((1, LANES), lambda b, j: (0, 0)),
            pl.BlockSpec((1, LANES), lambda b, j: (0, 0)),
        ],
        out_specs=pl.BlockSpec((sb, LANES), lambda b, j: (b * nsb + j, 0)),
        out_shape=jax.ShapeDtypeStruct(small.shape, F32),
        scratch_shapes=[pltpu.VMEM((8, LANES), F32)],
        compiler_params=_cparams(("parallel", "arbitrary")),
        name="gates",
    )(small, alog_row, bias_row)


def _gdn_prep_kernel(x_ref, w_ref, o_ref, pad_ref, *, S, rb):
    j = pl.program_id(1)
    pad_ref[0:8, :] = jnp.zeros((8, LANES), F32)
    pad_ref[8:8 + S, :] = x_ref[...].astype(F32)
    is_qk = j < 2 * GDN_HEADS
    scale = jnp.where(j < GDN_HEADS, GDN_DK ** -0.5, 1.0).astype(F32)
    for r0 in range(0, S, rb):
        acc = jnp.zeros((rb, LANES), F32)
        for t in range(GDN_CONV):
            off = 8 - (GDN_CONV - 1) + t
            acc = acc + w_ref[t:t + 1, :] * pad_ref[r0 + off:r0 + off + rb, :]
        y = _silu(acc)
        n = y * lax.rsqrt(jnp.sum(y * y, axis=-1, keepdims=True) + EPS) * scale
        o_ref[r0:r0 + rb, :] = jnp.where(is_qk, n, y).astype(o_ref.dtype)


def _gdn_prep(proj, conv_w, B, S):
    T = B * S
    ncol = 3 * GDN_HEADS
    rb = min(256, S)
    return pl.pallas_call(
        functools.partial(_gdn_prep_kernel, S=S, rb=rb),
        grid=(B, ncol),
        in_specs=[
            pl.BlockSpec((S, LANES), lambda b, j: (b, COL_GQKV // LANES + j)),
            pl.BlockSpec((GDN_CONV, LANES), lambda b, j: (0, j)),
        ],
        out_specs=pl.BlockSpec((S, LANES), lambda b, j: (b, j)),
        out_shape=jax.ShapeDtypeStruct((T, ncol * LANES), BF16),
        scratch_shapes=[pltpu.VMEM((S + 8, LANES), F32)],
        compiler_params=_cparams(("parallel", "parallel")),
        name="gdn_prep",
    )(proj, conv_w)


_PAIR = 2 * GDN_CHUNK


def _inv_unit_lower(lows, blk16, eye):
    mm = lambda a, b: _dot(a.astype(BF16), b.astype(BF16))
    xs = [jnp.where(blk16, -low, 0.0) for low in lows]
    offs = [jnp.where(blk16, 0.0, low) for low in lows]
    x2 = [mm(x, x) for x in xs]
    x4 = [mm(a, a) for a in x2]
    d1 = [mm(eye + x, eye + a) for x, a in zip(xs, x2)]
    x8 = [mm(a, a) for a in x4]
    d2 = [mm(eye + a, eye + b) for a, b in zip(x4, x8)]
    dinv = [mm(a, b) for a, b in zip(d1, d2)]
    ns = [mm(d, o) for d, o in zip(dinv, offs)]
    n2 = [mm(n, n) for n in ns]
    t1 = [mm(eye - n, eye + m) for n, m in zip(ns, n2)]
    return [mm(t, d) for t, d in zip(t1, dinv)]


_GDN_HB = 4
_GDN_UNROLL = 2


def _gdn_kernel(q_ref, k_ref, v_ref, ga_ref, z_ref, gn_ref, o_ref, kw_s, c_s, qp_s, oi_s, a_s, *, S):
    hg = pl.program_id(1)
    P = _PAIR
    C = GDN_CHUNK
    row = lax.broadcasted_iota(jnp.int32, (P, P), 0)
    col = lax.broadcasted_iota(jnp.int32, (P, P), 1)
    same = row // C == col // C
    causal = jnp.logical_and(same, row >= col)
    strict = jnp.logical_and(same, row > col)
    blk16 = row // 16 == col // 16
    eye = jnp.where(row == col, 1.0, 0.0).astype(F32)
    first_half = row < C
    heads = [hg * _GDN_HB + hh for hh in range(_GDN_HB)]
    row2 = lax.broadcasted_iota(jnp.int32, (2 * LANES, 2 * LANES), 0) % LANES
    col2 = lax.broadcasted_iota(jnp.int32, (2 * LANES, 2 * LANES), 1)
    colr = lax.broadcasted_iota(jnp.int32, (P, 2 * LANES), 1) % LANES
    sel_gb = [jnp.where(row2 == jnp.where(col2 < LANES, LANE_A + h, LANE_B + h), 1.0, 0.0).astype(BF16) for h in heads]
    sel_gr = [jnp.where(colr == LANE_A + h, 1.0, 0.0).astype(BF16) for h in heads]

    def prep(it, carry):
        chains = [(it * _GDN_UNROLL + u, hh) for u in range(_GDN_UNROLL) for hh in range(_GDN_HB)]
        r0 = [pl.multiple_of(p * P, P) for p, _ in chains]
        ln = [slice(hh * LANES, (hh + 1) * LANES) for _, hh in chains]
        hh_ = [hh for _, hh in chains]
        n = len(chains)
        q = [q_ref[pl.ds(r0[i], P), ln[i]].astype(F32) for i in range(n)]
        k = [k_ref[pl.ds(r0[i], P), ln[i]].astype(F32) for i in range(n)]
        v = [v_ref[pl.ds(r0[i], P), ln[i]].astype(F32) for i in range(n)]
        ga = [ga_ref[pl.ds(r0[i], P), :] for i in range(n)]
        gah = [x.astype(BF16) for x in ga]
        gam = [(ga[i] - gah[i].astype(F32)).astype(BF16) for i in range(n)]
        ga2 = [jnp.concatenate([gah[i], gam[i]], axis=1) for i in range(n)]
        gb = [_dot(ga2[i], sel_gb[hh_[i]]) for i in range(n)]
        gcol = [x[:, :LANES] for x in gb]
        bcol = [x[:, LANES:] for x in gb]
        grow = [_dot(sel_gr[hh_[i]], ga2[i], _NT) for i in range(n)]
        kb = [x.astype(BF16) for x in k]
        kk = [_dot(x, x, _NT) for x in kb]
        qkr = [_dot(q[i].astype(BF16), kb[i], _NT) for i in range(n)]
        glast = [jnp.where(first_half, g[C - 1:C, :], g[P - 1:P, :]) for g in gcol]
        decay = [jnp.where(causal, jnp.exp(jnp.where(causal, gcol[i] - grow[i], 0.0)), 0.0) for i in range(n)]
        low = [jnp.where(strict, bcol[i] * kk[i] * decay[i], 0.0) for i in range(n)]
        tinv = _inv_unit_lower(low, blk16, eye)
        eg = [jnp.exp(g) for g in gcol]
        rhs = [jnp.concatenate([v[i] * bcol[i], k[i] * (bcol[i] * eg[i])], axis=1) for i in range(n)]
        solb = [_dot3(tinv[i], rhs[i]).astype(BF16) for i in range(n)]
        qw = [_dot((qkr[i] * decay[i]).astype(BF16), solb[i]) for i in range(n)]
        ktail = [(k[i] * jnp.exp(glast[i] - gcol[i])).astype(BF16) for i in range(n)]
        for i, (p, hh) in enumerate(chains):
            qp_s[hh, pl.ds(r0[i], P), :] = (q[i] * eg[i] - qw[i][:, LANES:]).astype(BF16)
            oi_s[hh, pl.ds(r0[i], P), :] = qw[i][:, :LANES]
            for half in range(2):
                rows = slice(half * C, (half + 1) * C)
                c = 2 * p + half
                kwc = _dot(ktail[i][rows], solb[i][rows], _TN)
                c_s[hh, c] = kwc[:, :LANES]
                kw_s[hh, c] = kwc[:, LANES:].astype(BF16)
                a_s[hh, c] = jnp.broadcast_to(jnp.exp(glast[i][half * C:half * C + 1, :]), (8, LANES))
        return carry

    lax.fori_loop(0, S // (P * _GDN_UNROLL), prep, 0)

    def step(c, states):
        r0 = pl.multiple_of(c * C, C)
        hs = range(_GDN_HB)
        sb = [states[hh].astype(BF16) for hh in hs]
        ks = [_dot(kw_s[hh, c], sb[hh]) for hh in hs]
        os_ = [_dot(qp_s[hh, pl.ds(r0, C), :], sb[hh]) for hh in hs]
        new = [a_s[hh, c][0:1, :] * states[hh] - ks[hh] + c_s[hh, c] for hh in hs]
        for hh in hs:
            lanes = slice(hh * LANES, (hh + 1) * LANES)
            o = os_[hh] + oi_s[hh, pl.ds(r0, C), :]
            on = o * lax.rsqrt(jnp.mean(o * o, axis=-1, keepdims=True) + EPS) * gn_ref[...]
            z = z_ref[pl.ds(r0, C), lanes].astype(F32)
            o_ref[pl.ds(r0, C), lanes] = (on * _silu(z)).astype(o_ref.dtype)
        return tuple(new)

    lax.fori_loop(0, S // C, step, tuple(jnp.zeros((GDN_DK, LANES), F32) for _ in range(_GDN_HB)))


def _gdn(qkvn, gact, proj, gnorm, B, S):
    T = B * S
    nc = S // GDN_CHUNK
    hb = _GDN_HB
    ng = GDN_HEADS // hb
    w = hb * LANES
    blk = lambda off: pl.BlockSpec((S, w), lambda b, h: (b, off + h))
    return pl.pallas_call(
        functools.partial(_gdn_kernel, S=S),
        grid=(B, ng),
        in_specs=[
            blk(0), blk(ng), blk(2 * ng),
            pl.BlockSpec((S, LANES), lambda b, h: (b, 0)),
            blk(COL_GZ // w),
            pl.BlockSpec((1, LANES), lambda b, h: (0, 0)),
        ],
        out_specs=pl.BlockSpec((S, w), lambda b, h: (b, h)),
        out_shape=jax.ShapeDtypeStruct((T, GDN_HEADS * LANES), BF16),
        scratch_shapes=[
            pltpu.VMEM((hb, nc, GDN_DK, LANES), BF16),
            pltpu.VMEM((hb, nc, GDN_DK, LANES), F32),
            pltpu.VMEM((hb, S, LANES), BF16),
            pltpu.VMEM((hb, S, LANES), F32),
            pltpu.VMEM((hb, nc, 8, LANES), F32),
        ],
        compiler_params=_cparams(("parallel", "parallel")),
        name="gdn",
    )(qkvn, qkvn, qkvn, gact, proj, gnorm)


AUG0 = FOX_HD
_NPAIR = FOX_HEADS * FOX_HD // LANES


def _fox_aug_selectors():
    import numpy as np
    sel = np.zeros((2 * _NPAIR, 3 * LANES, 2 * LANES), np.float32)
    for j in range(2 * _NPAIR):
        is_q = j < _NPAIR
        for slot in range(2):
            head = 2 * (j % _NPAIR) + slot
            c_lane, one_lane, sign = (AUG0, AUG0 + 3, 1.0) if is_q else (AUG0 + 3, AUG0, -1.0)
            for piece in range(3):
                sel[j, piece * LANES + LANE_F + head, slot * LANES + c_lane + piece] = sign
                sel[j, LANE_ONE, slot * LANES + one_lane + piece] = 1.0
    return jnp.asarray(sel, BF16)


def _fox_prep_kernel(x_ref, ga_ref, g_ref, sel_ref, o_ref):
    is_q = pl.program_id(1) < _NPAIR
    x = x_ref[...].astype(F32)
    row = lax.broadcasted_iota(jnp.int32, (LANES, LANES), 0)
    col = lax.broadcasted_iota(jnp.int32, (LANES, LANES), 1)
    grp = jnp.where(row // FOX_HD == col // FOX_HD, 1.0, 0.0).astype(BF16)
    xx = x * x
    hi = xx.astype(BF16)
    lo = (xx - hi.astype(F32)).astype(BF16)
    ms = _dot(jnp.concatenate([hi, lo], axis=1), jnp.concatenate([grp, grp], axis=0)) * (1.0 / FOX_HD)
    scale = jnp.where(is_q, LOG2E * FOX_HD ** -0.5, 1.0).astype(F32)
    xn = x * lax.rsqrt(ms + EPS) * (g_ref[0] * scale)
    xr = pltpu.roll(xn, FOX_HD, axis=1)
    aug = _dot(jnp.concatenate(_split3(ga_ref[...]), axis=1), sel_ref[0])
    lane = lax.broadcasted_iota(jnp.int32, x.shape, 1)
    o_ref[:, :LANES] = jnp.where(lane < FOX_HD, xn, aug[:, :LANES]).astype(o_ref.dtype)
    o_ref[:, LANES:] = jnp.where(lane < FOX_HD, xr, aug[:, LANES:]).astype(o_ref.dtype)


def _fox_prep(proj, gact, gains, T):
    tm = min(1024, T)
    ncol = 2 * _NPAIR
    return pl.pallas_call(
        _fox_prep_kernel,
        grid=(T // tm, ncol),
        in_specs=[
            pl.BlockSpec((tm, LANES), lambda i, j: (i, COL_FQKV // LANES + j)),
            pl.BlockSpec((tm, LANES), lambda i, j: (i, 0)),
            pl.BlockSpec((1, 1, LANES), lambda i, j: (j, 0, 0)),
            pl.BlockSpec((1, 3 * LANES, 2 * LANES), lambda i, j: (j, 0, 0)),
        ],
        out_specs=pl.BlockSpec((tm, 2 * LANES), lambda i, j: (i, j)),
        out_shape=jax.ShapeDtypeStruct((T, ncol * 2 * LANES), BF16),
        compiler_params=_cparams(("parallel", "arbitrary")),
        name="fox_prep",
    )(proj, gact, gains, _fox_aug_selectors())


def _vt_kernel(v_ref, o_ref):
    o_ref[...] = jnp.transpose(v_ref[...].astype(F32)).astype(o_ref.dtype)


def _fox_vt(proj, T):
    tm = min(2048, T)
    nhp = FOX_HEADS // 2
    return pl.pallas_call(
        _vt_kernel,
        grid=(T // tm, nhp),
        in_specs=[pl.BlockSpec((tm, LANES), lambda i, h: (i, COL_FQKV // LANES + 2 * nhp + h))],
        out_specs=pl.BlockSpec((LANES, tm), lambda i, h: (h, i)),
        out_shape=jax.ShapeDtypeStruct((nhp * LANES, T), BF16),
        compiler_params=_cparams(("parallel", "parallel")),
        name="fox_vt",
    )(proj)


def _fox_kernel(q_ref, k_ref, vt_ref, o_ref, *, tq):
    i = pl.program_id(2)
    hs = range(2)
    q = [q_ref[:, hh * LANES:(hh + 1) * LANES] for hh in hs]
    keyi = lax.broadcasted_iota(jnp.int32, (tq, tq), 0)
    qryi = lax.broadcasted_iota(jnp.int32, (tq, tq), 1)
    tri = keyi <= qryi

    def kv_step(jb, carry, masked):
        ms, ls, accs = carry
        r0 = pl.multiple_of(jb * tq, tq)
        vt = [vt_ref[hh * FOX_HD:(hh + 1) * FOX_HD, pl.ds(r0, tq)] for hh in hs]
        s = [_dot(k_ref[pl.ds(r0, tq), hh * LANES:(hh + 1) * LANES], q[hh], _NT) for hh in hs]
        if masked:
            s = [jnp.where(tri, x, -jnp.inf) for x in s]
        m_new = [jnp.maximum(ms[hh], jnp.max(s[hh], axis=0, keepdims=True)) for hh in hs]
        pr = [jnp.exp2(s[hh] - m_new[hh]) for hh in hs]
        alpha = [jnp.exp2(ms[hh] - m_new[hh]) for hh in hs]
        pv = [_dot(vt[hh], pr[hh].astype(BF16)) for hh in hs]
        ls = [alpha[hh] * ls[hh] + jnp.sum(pr[hh], axis=0, keepdims=True) for hh in hs]
        accs = [alpha[hh] * accs[hh] + pv[hh] for hh in hs]
        return m_new, ls, accs

    init = ([jnp.full((1, tq), -jnp.inf, F32) for _ in hs], [jnp.zeros((1, tq), F32) for _ in hs],
            [jnp.zeros((FOX_HD, tq), F32) for _ in hs])
    carry = lax.fori_loop(0, i, lambda jb, c: kv_step(jb, c, False), init)
    _, ls, accs = kv_step(i, carry, True)
    ot = jnp.concatenate([accs[hh] / ls[hh] for hh in hs], axis=0)
    o_ref[...] = jnp.transpose(ot).astype(o_ref.dtype)


def _fox(qka, vt, B, S):
    T = B * S
    tq = min(1024, S)
    nq = S // tq
    nhp = FOX_HEADS // 2
    return pl.pallas_call(
        functools.partial(_fox_kernel, tq=tq),
        grid=(B, nhp, nq),
        in_specs=[
            pl.BlockSpec((tq, 2 * LANES), lambda b, h, i: (b * nq + i, h)),
            pl.BlockSpec((S, 2 * LANES), lambda b, h, i: (b, nhp + h)),
            pl.BlockSpec((LANES, S), lambda b, h, i: (h, b)),
        ],
        out_specs=pl.BlockSpec((tq, LANES), lambda b, h, i: (b * nq + i, h)),
        out_shape=jax.ShapeDtypeStruct((T, nhp * LANES), BF16),
        compiler_params=_cparams(("parallel", "parallel", "arbitrary")),
        name="fox",
    )(qka, qka, vt)


HALF = D_MODEL // 2


def _rms(x, gain):
    return x * lax.rsqrt(jnp.mean(x * x, axis=-1, keepdims=True) + EPS) * gain


def _pack_halves(x):
    lo = lax.bitcast_convert_type(x[:, :HALF].astype(BF16).astype(F32), jnp.uint32)
    hi = lax.bitcast_convert_type(x[:, HALF:].astype(BF16).astype(F32), jnp.uint32)
    return lax.shift_right_logical(lo, jnp.uint32(16)) | (hi & jnp.uint32(0xFFFF0000))


def _unpack_halves(w):
    lo = lax.bitcast_convert_type(lax.shift_left(w, jnp.uint32(16)), F32)
    hi = lax.bitcast_convert_type(w & jnp.uint32(0xFFFF0000), F32)
    return lo, hi


def _mix_kernel(oa_ref, of_ref, ma_ref, mf_ref, x_ref, wa_ref, wf_ref, wo_ref, g_ref, h_ref, hp_ref):
    ya = _dot(oa_ref[...], wa_ref[...])
    yf = _dot(of_ref[...], wf_ref[...])
    m = _sigmoid(ma_ref[...].astype(F32)) * ya + _sigmoid(mf_ref[...].astype(F32)) * yf
    h = x_ref[...] + _dot(m.astype(BF16), wo_ref[...])
    h_ref[...] = h
    hp_ref[...] = _pack_halves(_rms(h, g_ref[...]))


def _mix(o_a, o_f, proj, x2, wa, wf, wo, gain):
    T = x2.shape[0]
    tm = min(512, T)
    row = lambda c: pl.BlockSpec((tm, D_MODEL), lambda i: (i, c))
    full = pl.BlockSpec((D_MODEL, D_MODEL), lambda i: (0, 0))
    return pl.pallas_call(
        _mix_kernel,
        grid=(T // tm,),
        in_specs=[row(0), row(0), row(COL_MERGE // D_MODEL), row(COL_MERGE // D_MODEL + 1), row(0),
                  full, full, full, pl.BlockSpec((1, D_MODEL), lambda i: (0, 0))],
        out_specs=[row(0), pl.BlockSpec((tm, HALF), lambda i: (i, 0))],
        out_shape=[jax.ShapeDtypeStruct((T, D_MODEL), F32), jax.ShapeDtypeStruct((T, HALF), jnp.uint32)],
        compiler_params=_cparams(("parallel",)),
        name="mix_out",
    )(o_a, o_f, proj, proj, x2, wa, wf, wo, gain)


def _router_kernel(h_ref, g_ref, wr_ref, bias_ref, gates_ref, sel_ref, cnt_ref):
    @pl.when(pl.program_id(0) == 0)
    def _():
        cnt_ref[...] = jnp.zeros_like(cnt_ref)

    hn = _rms(h_ref[...], g_ref[...])
    hh = hn.astype(BF16)
    hl = (hn - hh.astype(F32)).astype(BF16)
    logits = _dot(wr_ref[0], hh, _NT) + _dot(wr_ref[0], hl, _NT) + _dot(wr_ref[1], hh, _NT)
    scores = _sigmoid(logits)
    work = scores + bias_ref[...]
    eidx = lax.broadcasted_iota(jnp.int32, work.shape, 0)
    sel = jnp.zeros(work.shape, F32)
    for _ in range(TOP_K):
        mx = jnp.max(work, axis=0, keepdims=True)
        first = jnp.min(jnp.where(work == mx, eidx, N_EXPERTS), axis=0, keepdims=True)
        onehot = eidx == first
        sel = jnp.where(onehot, 1.0, sel)
        work = jnp.where(onehot, -jnp.inf, work)
    s = jnp.where(sel > 0.0, scores, 0.0)
    gates_ref[...] = s / jnp.sum(s, axis=0, keepdims=True) * ROUTED_SCALE
    sel_ref[...] = sel
    tot = jnp.zeros((N_EXPERTS, 1), F32)
    for t0 in range(0, sel.shape[1], MOE_TT):
        n = jnp.sum(sel[:, t0:t0 + MOE_TT], axis=1, keepdims=True)
        tot = tot + jnp.floor((n + (SUB - 1)) * (1.0 / SUB)) * SUB
    cnt_ref[...] += jnp.broadcast_to(tot, cnt_ref.shape)


def _router(h1, gain, wr2, bias_col):
    T = h1.shape[0]
    tm = min(2 * MOE_TT, T)
    return pl.pallas_call(
        _router_kernel,
        grid=(T // tm,),
        in_specs=[
            pl.BlockSpec((tm, D_MODEL), lambda i: (i, 0)),
            pl.BlockSpec((1, D_MODEL), lambda i: (0, 0)),
            pl.BlockSpec((2, N_EXPERTS, D_MODEL), lambda i: (0, 0, 0)),
            pl.BlockSpec((N_EXPERTS, 1), lambda i: (0, 0)),
        ],
        out_specs=[
            pl.BlockSpec((N_EXPERTS, tm), lambda i: (0, i)),
            pl.BlockSpec((N_EXPERTS, tm), lambda i: (0, i)),
            pl.BlockSpec((N_EXPERTS, LANES), lambda i: (0, 0)),
        ],
        out_shape=[
            jax.ShapeDtypeStruct((N_EXPERTS, T), F32),
            jax.ShapeDtypeStruct((N_EXPERTS, T), F32),
            jax.ShapeDtypeStruct((N_EXPERTS, LANES), F32),
        ],
        compiler_params=_cparams(("arbitrary",)),
        name="router",
    )(h1, gain, wr2, bias_col)


MOE_TT = 256
SUB = 8
MOE_R = MOE_TT * TOP_K + N_EXPERTS * SUB


def _pos_kernel(sel_ref, gates_ref, cnt_ref, lpos_ref, w_ref, meta_ref, base_s, carry_s, *, tt, tme):
    r = lax.broadcasted_iota(jnp.int32, (N_EXPERTS, N_EXPERTS), 0)
    c = lax.broadcasted_iota(jnp.int32, (N_EXPERTS, N_EXPERTS), 1)
    below = jnp.where(r > c, 1.0, 0.0).astype(BF16)

    @pl.when(pl.program_id(0) == 0)
    def _():
        padded = jnp.floor((cnt_ref[...] + (tme - 1)) * (1.0 / tme)) * tme
        base_s[...] = _dot_lx(below, padded)
        carry_s[...] = jnp.zeros_like(carry_s)

    sel = sel_ref[...]
    gates = gates_ref[...]
    rr = lax.broadcasted_iota(jnp.int32, (tt, tt), 0)
    cc = lax.broadcasted_iota(jnp.int32, (tt, tt), 1)
    cs = _dot(sel.astype(BF16), jnp.where(rr <= cc, 1.0, 0.0).astype(BF16))
    n = jnp.sum(sel, axis=1, keepdims=True)
    npad = jnp.broadcast_to(jnp.floor((n + (SUB - 1)) * (1.0 / SUB)) * SUB, (N_EXPERTS, LANES))
    loff = _dot_lx(below, npad)
    meta_ref[0, 0] = base_s[...] + carry_s[...]
    meta_ref[0, 1] = npad
    meta_ref[0, 2] = loff
    carry_s[...] += npad
    lposd = cs - sel + jnp.concatenate([loff] * (tt // LANES), axis=1)
    eidx = lax.broadcasted_iota(jnp.int32, sel.shape, 0)
    rem = sel
    ps, ws = [], []
    for _ in range(TOP_K):
        first = jnp.min(jnp.where(rem > 0.0, eidx, N_EXPERTS), axis=0, keepdims=True)
        onehot = eidx == first
        ps.append(jnp.sum(jnp.where(onehot, lposd, 0.0), axis=0, keepdims=True))
        ws.append(jnp.sum(jnp.where(onehot, gates, 0.0), axis=0, keepdims=True))
        rem = jnp.where(onehot, 0.0, rem)
    lpos_ref[...] = jnp.concatenate(ps, axis=0).astype(jnp.int32)
    w_ref[...] = jnp.concatenate(ws, axis=0)


def _positions(sel, gates, cnt, tme):
    T = sel.shape[1]
    tt = MOE_TT
    blk = pl.BlockSpec((N_EXPERTS, tt), lambda i: (0, i))
    return pl.pallas_call(
        functools.partial(_pos_kernel, tt=tt, tme=tme),
        grid=(T // tt,),
        in_specs=[blk, blk, pl.BlockSpec((N_EXPERTS, LANES), lambda i: (0, 0))],
        out_specs=[pl.BlockSpec((TOP_K, tt), lambda i: (0, i)), pl.BlockSpec((TOP_K, tt), lambda i: (0, i)),
                   pl.BlockSpec((1, 3, N_EXPERTS, LANES), lambda i: (i, 0, 0, 0))],
        out_shape=[jax.ShapeDtypeStruct((TOP_K, T), jnp.int32), jax.ShapeDtypeStruct((TOP_K, T), F32),
                   jax.ShapeDtypeStruct((T // tt, 3, N_EXPERTS, LANES), F32)],
        scratch_shapes=[pltpu.VMEM((N_EXPERTS, LANES), F32), pltpu.VMEM((N_EXPERTS, LANES), F32)],
        compiler_params=_cparams(("arbitrary",)),
        name="moe_positions",
    )(sel, gates, cnt)


def _rows_used(seg_ref):
    return seg_ref[0, 0, 3 * N_EXPERTS - 1] + seg_ref[0, 0, 2 * N_EXPERTS - 1]


def _wait_segment_copies(seg_ref, src_of, dst_of, sem):
    used = _rows_used(seg_ref)

    def unit(rows):
        def body(i, carry):
            pltpu.make_async_copy(src_of(0, 0, rows), dst_of(0, 0, rows), sem).wait()
            return carry
        return body

    lax.fori_loop(0, lax.shift_right_logical(used, 7), unit(LANES), 0)
    lax.fori_loop(0, lax.shift_right_logical(used, 3) & (LANES // SUB - 1), unit(SUB), 0)


def _segment_copies(seg_ref, src_of, dst_of, sem):
    def per_expert(e, carry):
        g = seg_ref[0, 0, e]
        n = seg_ref[0, 0, N_EXPERTS + e]
        l = seg_ref[0, 0, 2 * N_EXPERTS + e]

        def piece(off, rows):
            pltpu.make_async_copy(src_of(pl.multiple_of(g + off, SUB), pl.multiple_of(l + off, SUB), rows),
                                  dst_of(pl.multiple_of(g + off, SUB), pl.multiple_of(l + off, SUB), rows), sem).start()

        def full(cidx, c2):
            piece(cidx * (2 * SUB), 2 * SUB)
            return c2

        nfull = lax.shift_right_logical(n, 4)
        lax.fori_loop(0, nfull, full, 0)

        @pl.when((n & SUB) != 0)
        def _():
            piece(nfull * (2 * SUB), SUB)

        return carry

    lax.fori_loop(0, N_EXPERTS, per_expert, 0)


_ROW_CHUNK = MOE_R // 4


def _pack2(lo, hi):
    lo = lax.bitcast_convert_type(lo.astype(BF16).astype(F32), jnp.uint32)
    hi = lax.bitcast_convert_type(hi.astype(BF16).astype(F32), jnp.uint32)
    return lax.shift_right_logical(lo, jnp.uint32(16)) | (hi & jnp.uint32(0xFFFF0000))


def _dispatch_kernel(meta_ref, seg_ref, lpos_ref, hp_ref, xs_ref, zero_s, sort_s, sem, zsem, *, tt, tme, n_tiles):
    @pl.when(pl.program_id(0) == 0)
    def _():
        zero_s[...] = jnp.zeros_like(zero_s)
        for e in range(N_EXPERTS):
            pltpu.make_async_copy(zero_s, xs_ref.at[pl.ds(pl.multiple_of(meta_ref[e], SUB), tme)], zsem).start()
        for e in range(N_EXPERTS):
            pltpu.make_async_copy(zero_s, xs_ref.at[pl.ds(pl.multiple_of(meta_ref[e], SUB), tme)], zsem).wait()

        def zero_tile(t, carry):
            cp = pltpu.make_async_copy(zero_s, xs_ref.at[pl.ds(pl.multiple_of(t * tme, tme), tme)], zsem)
            cp.start()
            cp.wait()
            return carry

        lax.fori_loop(meta_ref[N_EXPERTS], n_tiles, zero_tile, 0)

    lo, hi = _unpack_halves(hp_ref[...])
    lo = lo.astype(BF16)
    hi = hi.astype(BF16)
    lp = lpos_ref[...].astype(jnp.int16)
    one = jnp.ones((1, 1), BF16)
    for rc in range(MOE_R // _ROW_CHUNK):
        rowi = (lax.broadcasted_iota(jnp.int32, (_ROW_CHUNK, tt), 0) + rc * _ROW_CHUNK).astype(jnp.int16)
        pm = jnp.zeros((_ROW_CHUNK, tt), BF16)
        for k in range(TOP_K):
            pm = jnp.where(rowi == lp[k:k + 1, :], one, pm)
        sort_s[rc * _ROW_CHUNK:(rc + 1) * _ROW_CHUNK, :] = _pack2(_dot(pm, lo), _dot(pm, hi))

    src_of = lambda g, l, rows: sort_s.at[pl.ds(l, rows)]
    dst_of = lambda g, l, rows: xs_ref.at[pl.ds(g, rows)]
    _segment_copies(seg_ref, src_of, dst_of, sem)
    _wait_segment_copies(seg_ref, src_of, dst_of, sem)


def _dispatch(meta, seg, lpos, hp, n_tiles, tme):
    T = hp.shape[0]
    tt = MOE_TT
    return pl.pallas_call(
        functools.partial(_dispatch_kernel, tt=tt, tme=tme, n_tiles=n_tiles),
        grid_spec=pltpu.PrefetchScalarGridSpec(
            num_scalar_prefetch=1,
            grid=(T // tt,),
            in_specs=[
                pl.BlockSpec((1, 1, 3 * N_EXPERTS), lambda i, f: (i, 0, 0), memory_space=pltpu.SMEM),
                pl.BlockSpec((TOP_K, tt), lambda i, f: (0, i)),
                pl.BlockSpec((tt, HALF), lambda i, f: (i, 0)),
            ],
            out_specs=pl.BlockSpec(memory_space=pl.ANY),
            scratch_shapes=[pltpu.VMEM((tme, HALF), jnp.uint32), pltpu.VMEM((MOE_R, HALF), jnp.uint32),
                            pltpu.SemaphoreType.DMA, pltpu.SemaphoreType.DMA],
        ),
        out_shape=jax.ShapeDtypeStruct((n_tiles * tme, HALF), jnp.uint32),
        compiler_params=_cparams(("arbitrary",)),
        name="moe_dispatch",
    )(meta, seg, lpos, hp)


def _expert_kernel(te_ref, nu_ref, xs_ref, wgu_ref, wd_ref, ys_ref):
    @pl.when(pl.program_id(0) < nu_ref[0])
    def _():
        lo, hi = _unpack_halves(xs_ref[...])
        gu = _dot(lo.astype(BF16), wgu_ref[0, :HALF, :]) + _dot(hi.astype(BF16), wgu_ref[0, HALF:, :])
        hid = _silu(gu[:, :D_EXPERT]) * gu[:, D_EXPERT:]
        ys_ref[...] = _pack_halves(_dot(hid.astype(BF16), wd_ref[0]))

    @pl.when(pl.program_id(0) >= nu_ref[0])
    def _():
        ys_ref[...] = jnp.zeros_like(ys_ref)


def _experts(tile_e, n_used, xs, wgu, wd, n_tiles, tme):
    clamp = lambda i, te, nu: (jnp.minimum(i, nu[0] - 1), 0)
    return pl.pallas_call(
        _expert_kernel,
        grid_spec=pltpu.PrefetchScalarGridSpec(
            num_scalar_prefetch=2,
            grid=(n_tiles,),
            in_specs=[
                pl.BlockSpec((tme, HALF), clamp),
                pl.BlockSpec((1, D_MODEL, 2 * D_EXPERT), lambda i, te, nu: (te[i], 0, 0)),
                pl.BlockSpec((1, D_EXPERT, D_MODEL), lambda i, te, nu: (te[i], 0, 0)),
            ],
            out_specs=pl.BlockSpec((tme, HALF), lambda i, te, nu: (i, 0)),
        ),
        out_shape=jax.ShapeDtypeStruct((n_tiles * tme, HALF), jnp.uint32),
        compiler_params=_cparams(("arbitrary",)),
        name="moe_experts",
    )(tile_e, n_used, xs, wgu, wd)


def _tail_kernel(seg_ref, ys_ref, lpos_ref, w_ref, h_ref, p_ref, gf_ref, wsgu_ref, wsd_ref, wple_ref, gple_ref,
                 gpg_ref, wpg_ref, o_ref, buf, sem, *, tt):
    src_of = lambda g, l, rows: ys_ref.at[pl.ds(g, rows)]
    dst_of = lambda g, l, rows: buf.at[pl.ds(l, rows)]
    _segment_copies(seg_ref, src_of, dst_of, sem)

    used = _rows_used(seg_ref)

    def clear(j, carry):
        buf[pl.ds(pl.multiple_of(j * SUB, SUB), SUB), :] = jnp.zeros((SUB, HALF), jnp.uint32)
        return carry

    lax.fori_loop(lax.shift_right_logical(used, 3), MOE_R // SUB, clear, 0)

    h = h_ref[...]
    hn = _rms(h, gf_ref[...]).astype(BF16)
    sgu = _dot(hn, wsgu_ref[...])
    shared = _dot((_silu(sgu[:, :D_SHARED]) * sgu[:, D_SHARED:]).astype(BF16), wsd_ref[...])
    e = _rms(_dot(p_ref[...].astype(BF16), wple_ref[...]), gple_ref[...])
    _wait_segment_copies(seg_ref, src_of, dst_of, sem)

    lp = lpos_ref[...].astype(jnp.int16)
    wt = w_ref[...].astype(BF16)
    acc_lo = jnp.zeros((tt, HALF), F32)
    acc_hi = jnp.zeros((tt, HALF), F32)
    for rc in range(MOE_R // _ROW_CHUNK):
        coli = (lax.broadcasted_iota(jnp.int32, (tt, _ROW_CHUNK), 1) + rc * _ROW_CHUNK).astype(jnp.int16)
        g = jnp.zeros((tt, _ROW_CHUNK), BF16)
        for k in range(TOP_K):
            g = jnp.where(coli == lp[:, k:k + 1], wt[:, k:k + 1], g)
        lo, hi = _unpack_halves(buf[rc * _ROW_CHUNK:(rc + 1) * _ROW_CHUNK, :])
        acc_lo = acc_lo + _dot(g, lo.astype(BF16))
        acc_hi = acc_hi + _dot(g, hi.astype(BF16))
    h2 = h + jnp.concatenate([acc_lo, acc_hi], axis=1) + shared
    gte = _sigmoid(_dot(_rms(h2, gpg_ref[...]).astype(BF16), wpg_ref[...]))
    o_ref[...] = h2 + gte * e


def _tail(seg, ys, lpos_tok, w_tok, h1, p2, gf, wsgu, wsd, wple, gple, gpg, wpg):
    T = h1.shape[0]
    tt = MOE_TT
    row = lambda n: pl.BlockSpec((tt, n), lambda i: (i, 0))
    full = lambda a: pl.BlockSpec(a.shape, lambda i: (0,) * a.ndim)
    return pl.pallas_call(
        functools.partial(_tail_kernel, tt=tt),
        grid=(T // tt,),
        in_specs=[
            pl.BlockSpec((1, 1, 3 * N_EXPERTS), lambda i: (i, 0, 0), memory_space=pltpu.SMEM),
            pl.BlockSpec(memory_space=pl.ANY),
            row(TOP_K), row(TOP_K), row(D_MODEL), row(PLE_DIM),
            full(gf), full(wsgu), full(wsd), full(wple), full(gple), full(gpg), full(wpg),
        ],
        out_specs=row(D_MODEL),
        out_shape=jax.ShapeDtypeStruct((T, D_MODEL), F32),
        scratch_shapes=[pltpu.VMEM((MOE_R, HALF), jnp.uint32), pltpu.SemaphoreType.DMA],
        compiler_params=_cparams(("arbitrary",)),
        name="moe_combine_ple",
    )(seg, ys, lpos_tok, w_tok, h1, p2, gf, wsgu, wsd, wple, gple, gpg, wpg)


def _stages(x, p, norm_mix, w_in, conv_w, a_log, dt_bias, gdn_norm, fox_f_bias, q_norm, k_norm,
            w_branch_gdn, w_branch_fox, w_out, norm_ffn, w_router, router_bias,
            we_gate, we_up, we_down, ws_gate, ws_up, ws_down, w_ple, ple_norm, ple_gate_norm, w_ple_gate):
    B, S, _ = x.shape
    T = B * S
    st = {}
    wi = w_in[0]
    o0 = 3 * GDN_HEADS * GDN_DK
    o1 = o0 + GDN_HEADS * GDN_DK
    o2 = o1 + GDN_HEADS
    o3 = o2 + GDN_HEADS
    o4 = o3 + 3 * FOX_HEADS * FOX_HD
    o5 = o4 + FOX_HEADS
    w_big = jnp.concatenate([wi[:, :o1], wi[:, o3:o4], wi[:, o5:]], axis=1).astype(BF16)
    w_small = jnp.concatenate([wi[:, o1:o3], wi[:, o4:o5],
                               jnp.zeros((D_MODEL, LANES - 2 * GDN_HEADS - FOX_HEADS), F32)], axis=1).astype(BF16)
    proj, small = _in_proj(x.reshape(T, D_MODEL), norm_mix[0].reshape(1, D_MODEL), w_big, w_small)
    st["proj_big"] = proj
    st["small"] = small

    pad = lambda v, off: jnp.zeros((1, LANES), F32).at[0, off:off + v.shape[0]].set(v)
    alog_row = pad(a_log[0], LANE_A)
    bias_row = pad(dt_bias[0], LANE_A) + pad(fox_f_bias[0], LANE_F)
    gact = _gates(small, alog_row, bias_row, B, S)
    st["gact"] = gact

    qkvn = _gdn_prep(proj, conv_w[0], B, S)
    st["qkvn"] = qkvn
    o_a = _gdn(qkvn, gact, proj, gdn_norm[0].reshape(1, LANES), B, S)
    st["o_a"] = o_a

    nrep = FOX_HEADS * FOX_HD // LANES
    gains = jnp.concatenate([jnp.tile(q_norm[0], (nrep, LANES // FOX_HD)),
                             jnp.tile(k_norm[0], (nrep, LANES // FOX_HD))], axis=0).reshape(2 * nrep, 1, LANES)
    qka = _fox_prep(proj, gact, gains, T)
    o_f = _fox(qka, _fox_vt(proj, T), B, S)
    st["o_f"] = o_f

    x2 = x.reshape(T, D_MODEL)
    gf = norm_ffn[0].reshape(1, D_MODEL)
    h1, hp = _mix(o_a, o_f, proj, x2, w_branch_gdn[0].astype(BF16), w_branch_fox[0].astype(BF16),
                  w_out[0].astype(BF16), gf)
    st["h1"] = h1

    wrt = w_router[0].T
    wr_hi = wrt.astype(BF16)
    wr2 = jnp.stack([wr_hi, (wrt - wr_hi.astype(F32)).astype(BF16)])
    gates, sel, cnt = _router(h1, gf, wr2, router_bias[0].reshape(N_EXPERTS, 1))
    st["gates_t"] = gates

    tme = 512 if T * TOP_K // N_EXPERTS >= 2048 else 64
    lpos, wts, segf = _positions(sel, gates, cnt, tme)
    nt_tok = T // MOE_TT
    seg = segf[:, :, :, 0].astype(jnp.int32).reshape(nt_tok, 1, 3 * N_EXPERTS)
    cnt1 = cnt[:, 0].astype(jnp.int32)
    padded = (cnt1 + (tme - 1)) // tme * tme
    ends = jnp.cumsum(padded)
    n_tiles = (T * TOP_K + (SUB - 1) * N_EXPERTS * nt_tok) // tme + N_EXPERTS
    tile_start = jnp.arange(n_tiles, dtype=jnp.int32) * tme
    tile_e = jnp.minimum(jnp.sum((ends[None, :] <= tile_start[:, None]).astype(jnp.int32), axis=1), N_EXPERTS - 1)
    n_used = (ends[-1:] // tme).astype(jnp.int32)
    fill = (ends - padded + cnt1).astype(jnp.int32)
    xs = _dispatch(jnp.concatenate([fill, n_used]), seg, lpos, hp, n_tiles, tme)
    wgu = jnp.concatenate([we_gate[0], we_up[0]], axis=2).astype(BF16)
    ys = _experts(tile_e, n_used, xs, wgu, we_down[0].astype(BF16), n_tiles, tme)

    row = lambda v: v.reshape(1, D_MODEL)
    wsgu = jnp.concatenate([ws_gate[0], ws_up[0]], axis=1).astype(BF16)
    out = _tail(seg, ys, lpos.T, wts.T, h1, p[0].reshape(T, PLE_DIM), gf, wsgu, ws_down[0].astype(BF16),
                w_ple[0].astype(BF16), row(ple_norm[0]), row(ple_gate_norm[0]), w_ple_gate[0].astype(BF16))
    st["out"] = out.reshape(B, S, D_MODEL)
    return st


def kernel(x, p, norm_mix, w_in, conv_w, a_log, dt_bias, gdn_norm, fox_f_bias, q_norm, k_norm, w_branch_gdn, w_branch_fox, w_out, norm_ffn, w_router, router_bias, we_gate, we_up, we_down, ws_gate, ws_up, ws_down, w_ple, ple_norm, ple_gate_norm, w_ple_gate):
    return _stages(x, p, norm_mix, w_in, conv_w, a_log, dt_bias, gdn_norm, fox_f_bias, q_norm, k_norm,
                   w_branch_gdn, w_branch_fox, w_out, norm_ffn, w_router, router_bias,
                   we_gate, we_up, we_down, ws_gate, ws_up, ws_down, w_ple, ple_norm, ple_gate_norm,
                   w_ple_gate)["out"]
```

```python
import functools

import jax
import jax.numpy as jnp
from jax import lax
from jax.experimental import pallas as pl
from jax.experimental.pallas import tpu as pltpu

F32 = jnp.float32
BF16 = jnp.bfloat16

D_MODEL = 1024
PLE_DIM = 256
EPS = 1e-6
GDN_HEADS = 8
GDN_DK = 128
GDN_CONV = 4
GDN_CHUNK = 64
FOX_HEADS = 16
FOX_HD = 64
N_EXPERTS = 64
TOP_K = 8
D_EXPERT = 256
D_SHARED = 256
ROUTED_SCALE = 2.5

LANES = 128
COL_GQKV = 0
COL_GZ = 3072
COL_FQKV = 4096
COL_MERGE = 7168
N_BIG = 9216
LANE_A = 0
LANE_B = 8
LANE_F = 16
LANE_ONE = 127
LOG2E = 1.4426950408889634

VMEM_LIMIT = 48 * 1024 * 1024
VMEM_LIMIT_GDN = 56 * 1024 * 1024


def _cparams(sem, vmem=VMEM_LIMIT):
    return pltpu.CompilerParams(dimension_semantics=sem, vmem_limit_bytes=vmem)


def _split3(x):
    h = x.astype(BF16)
    r = x - h.astype(F32)
    m = r.astype(BF16)
    l = (r - m.astype(F32)).astype(BF16)
    return h, m, l


def _dot(a, b, dims=(((1,), (0,)), ((), ()))):
    return lax.dot_general(a, b, dims, preferred_element_type=F32)


_NT = (((1,), (1,)), ((), ()))
_TN = (((0,), (0,)), ((), ()))


def _dot_xl(x, sel, dims=(((1,), (0,)), ((), ()))):
    h, m, l = _split3(x)
    return _dot(h, sel, dims) + _dot(m, sel, dims) + _dot(l, sel, dims)


def _dot_lx(sel, x, dims=(((1,), (0,)), ((), ()))):
    h, m, l = _split3(x)
    return _dot(sel, h, dims) + _dot(sel, m, dims) + _dot(sel, l, dims)


def _dot3(a, b):
    ah = a.astype(BF16)
    al = (a - ah.astype(F32)).astype(BF16)
    bh = b.astype(BF16)
    bl = (b - bh.astype(F32)).astype(BF16)
    return _dot(jnp.concatenate([ah, ah, al], axis=1), jnp.concatenate([bh, bl, bh], axis=0))


def _sigmoid(x):
    return 1.0 / (1.0 + jnp.exp(-x))


def _silu(x):
    return x * _sigmoid(x)


def _softplus(x):
    return jnp.maximum(x, 0.0) + jnp.log1p(jnp.exp(-jnp.abs(x)))


def _in_proj_kernel(x_ref, g_ref, wb_ref, ws_ref, ob_ref, os_ref, xn_ref):
    @pl.when(pl.program_id(1) == 0)
    def _():
        x = x_ref[...]
        ms = jnp.mean(x * x, axis=-1, keepdims=True)
        xn = (x * lax.rsqrt(ms + EPS) * g_ref[...]).astype(BF16)
        xn_ref[...] = xn
        os_ref[...] = _dot(xn, ws_ref[...])

    ob_ref[...] = _dot(xn_ref[...], wb_ref[...]).astype(ob_ref.dtype)


def _in_proj(x2, gain, w_big, w_small):
    T = x2.shape[0]
    tm = min(1024, T)
    tn = 1024
    return pl.pallas_call(
        _in_proj_kernel,
        grid=(T // tm, N_BIG // tn),
        in_specs=[
            pl.BlockSpec((tm, D_MODEL), lambda i, j: (i, 0)),
            pl.BlockSpec((1, D_MODEL), lambda i, j: (0, 0)),
            pl.BlockSpec((D_MODEL, tn), lambda i, j: (0, j)),
            pl.BlockSpec((D_MODEL, LANES), lambda i, j: (0, 0)),
        ],
        out_specs=[
            pl.BlockSpec((tm, tn), lambda i, j: (i, j)),
            pl.BlockSpec((tm, LANES), lambda i, j: (i, 0)),
        ],
        out_shape=[
            jax.ShapeDtypeStruct((T, N_BIG), BF16),
            jax.ShapeDtypeStruct((T, LANES), F32),
        ],
        scratch_shapes=[pltpu.VMEM((tm, D_MODEL), BF16)],
        compiler_params=_cparams(("parallel", "arbitrary")),
        name="in_proj",
    )(x2, gain, w_big, w_small)


def _gates_kernel(s_ref, alog_ref, bias_ref, o_ref, carry_ref, *, sb):
    @pl.when(pl.program_id(1) == 0)
    def _():
        carry_ref[...] = jnp.zeros_like(carry_ref)

    x = s_ref[...] + bias_ref[...]
    lane = lax.broadcasted_iota(jnp.int32, x.shape, 1)
    e = jnp.log1p(jnp.exp(-jnp.abs(x)))
    log_alpha = -jnp.exp(alog_ref[...]) * (jnp.maximum(x, 0.0) + e)
    beta = _sigmoid(x)
    log_f = -(jnp.maximum(-x, 0.0) + e)
    is_a = lane < LANE_B
    is_b = jnp.logical_and(lane >= LANE_B, lane < LANE_F)
    is_f = jnp.logical_and(lane >= LANE_F, lane < LANE_F + FOX_HEADS)
    val = jnp.where(is_a, log_alpha, jnp.where(is_f, log_f, 0.0))
    row = lax.broadcasted_iota(jnp.int32, (sb, sb), 0)
    col = lax.broadcasted_iota(jnp.int32, (sb, sb), 1)
    tri = row >= col
    tri_full = jnp.where(tri, 1.0, 0.0).astype(BF16)
    tri_chunk = jnp.where(jnp.logical_and(tri, row // GDN_CHUNK == col // GDN_CHUNK), 1.0, 0.0).astype(BF16)
    h, m, l = _split3(val)
    cs_full = _dot(tri_full, h) + _dot(tri_full, m) + _dot(tri_full, l) + carry_ref[0:1, :]
    cs_chunk = _dot(tri_chunk, h) + _dot(tri_chunk, m) + _dot(tri_chunk, l)
    rest = jnp.where(is_f, LOG2E * cs_full, jnp.where(lane == LANE_ONE, 1.0, 0.0))
    o_ref[...] = jnp.where(is_a, cs_chunk, jnp.where(is_b, beta, rest))
    carry_ref[...] = jnp.broadcast_to(cs_full[sb - 1:sb, :], carry_ref.shape)


def _gates(small, alog_row, bias_row, B, S):
    sb = min(256, S)
    nsb = S // sb
    return pl.pallas_call(
        functools.partial(_gates_kernel, sb=sb),
        grid=(B, nsb),
        in_specs=[
            pl.BlockSpec((sb, LANES), lambda b, j: (b * nsb + j, 0)),
            pl.BlockSpec((1, LANES), lambda b, j: (0, 0)),
            pl.BlockSpec((1, LANES), lambda b, j: (0, 0)),
        ],
        out_specs=pl.BlockSpec((sb, LANES), lambda b, j: (b * nsb + j, 0)),
        out_shape=jax.ShapeDtypeStruct(small.shape, F32),
        scratch_shapes=[pltpu.VMEM((8, LANES), F32)],
        compiler_params=_cparams(("parallel", "arbitrary")),
        name="gates",
    )(small, alog_row, bias_row)


def _gdn_prep_kernel(x_ref, w_ref, o_ref, pad_ref, *, S, rb):
    j = pl.program_id(1)
    pad_ref[0:8, :] = jnp.zeros((8, LANES), F32)
    pad_ref[8:8 + S, :] = x_ref[...].astype(F32)
    is_qk = j < 2 * GDN_HEADS
    scale = jnp.where(j < GDN_HEADS, GDN_DK ** -0.5, 1.0).astype(F32)
    for r0 in range(0, S, rb):
        acc = jnp.zeros((rb, LANES), F32)
        for t in range(GDN_CONV):
            off = 8 - (GDN_CONV - 1) + t
            acc = acc + w_ref[t:t + 1, :] * pad_ref[r0 + off:r0 + off + rb, :]
        y = _silu(acc)
        n = y * lax.rsqrt(jnp.sum(y * y, axis=-1, keepdims=True) + EPS) * scale
        o_ref[r0:r0 + rb, :] = jnp.where(is_qk, n, y).astype(o_ref.dtype)


def _gdn_prep(proj, conv_w, B, S):
    T = B * S
    ncol = 3 * GDN_HEADS
    rb = min(256, S)
    return pl.pallas_call(
        functools.partial(_gdn_prep_kernel, S=S, rb=rb),
        grid=(B, ncol),
        in_specs=[
            pl.BlockSpec((S, LANES), lambda b, j: (b, COL_GQKV // LANES + j)),
            pl.BlockSpec((GDN_CONV, LANES), lambda b, j: (0, j)),
        ],
        out_specs=pl.BlockSpec((S, LANES), lambda b, j: (b, j)),
        out_shape=jax.ShapeDtypeStruct((T, ncol * LANES), BF16),
        scratch_shapes=[pltpu.VMEM((S + 8, LANES), F32)],
        compiler_params=_cparams(("parallel", "parallel")),
        name="gdn_prep",
    )(proj, conv_w)


_PAIR = 2 * GDN_CHUNK


def _inv_unit_lower(lows, blk16, eye, between=lambda k: None):
    mm = lambda a, b: _dot(a.astype(BF16), b.astype(BF16))
    xs = [jnp.where(blk16, -low, 0.0) for low in lows]
    offs = [jnp.where(blk16, 0.0, low) for low in lows]
    x2 = [mm(x, x) for x in xs]
    x4 = [mm(a, a) for a in x2]
    d1 = [mm(eye + x, eye + a) for x, a in zip(xs, x2)]
    between(1)
    x8 = [mm(a, a) for a in x4]
    d2 = [mm(eye + a, eye + b) for a, b in zip(x4, x8)]
    dinv = [mm(a, b) for a, b in zip(d1, d2)]
    ns = [mm(d, o) for d, o in zip(dinv, offs)]
    between(2)
    n2 = [mm(n, n) for n in ns]
    t1 = [mm(eye - n, eye + m) for n, m in zip(ns, n2)]
    return [mm(t, d) for t, d in zip(t1, dinv)]


_GDN_HB = 4
_GDN_UNROLL = 2


def _gdn_kernel(q_ref, k_ref, v_ref, ga_ref, z_ref, gn_ref, o_ref, kw_s, c_s, qp_s, oi_s, a_s, *, S):
    hg = pl.program_id(1)
    P = _PAIR
    C = GDN_CHUNK
    row = lax.broadcasted_iota(jnp.int32, (P, P), 0)
    col = lax.broadcasted_iota(jnp.int32, (P, P), 1)
    same = row // C == col // C
    causal = jnp.logical_and(same, row >= col)
    strict = jnp.logical_and(same, row > col)
    blk16 = row // 16 == col // 16
    eye = jnp.where(row == col, 1.0, 0.0).astype(F32)
    first_half = row < C
    heads = [hg * _GDN_HB + hh for hh in range(_GDN_HB)]
    row2 = lax.broadcasted_iota(jnp.int32, (2 * LANES, 2 * LANES), 0) % LANES
    col2 = lax.broadcasted_iota(jnp.int32, (2 * LANES, 2 * LANES), 1)
    colr = lax.broadcasted_iota(jnp.int32, (P, 2 * LANES), 1) % LANES
    sel_gb = [jnp.where(row2 == jnp.where(col2 < LANES, LANE_A + h, LANE_B + h), 1.0, 0.0).astype(BF16) for h in heads]
    sel_gr = [jnp.where(colr == LANE_A + h, 1.0, 0.0).astype(BF16) for h in heads]

    def prep(it, between=lambda k: None):
        it = jnp.asarray(it, jnp.int32)
        chains =[(it * _GDN_UNROLL + u, hh) for u in range(_GDN_UNROLL) for hh in range(_GDN_HB)]
        r0 = [pl.multiple_of(p * P, P) for p, _ in chains]
        ln = [slice(hh * LANES, (hh + 1) * LANES) for _, hh in chains]
        hh_ = [hh for _, hh in chains]
        n = len(chains)
        q = [q_ref[pl.ds(r0[i], P), ln[i]].astype(F32) for i in range(n)]
        k = [k_ref[pl.ds(r0[i], P), ln[i]].astype(F32) for i in range(n)]
        v = [v_ref[pl.ds(r0[i], P), ln[i]].astype(F32) for i in range(n)]
        ga = [ga_ref[pl.ds(r0[i], P), :] for i in range(n)]
        gah = [x.astype(BF16) for x in ga]
        gam = [(ga[i] - gah[i].astype(F32)).astype(BF16) for i in range(n)]
        ga2 = [jnp.concatenate([gah[i], gam[i]], axis=1) for i in range(n)]
        gb = [_dot(ga2[i], sel_gb[hh_[i]]) for i in range(n)]
        gcol = [x[:, :LANES] for x in gb]
        bcol = [x[:, LANES:] for x in gb]
        grow = [_dot(sel_gr[hh_[i]], ga2[i], _NT) for i in range(n)]
        kb = [x.astype(BF16) for x in k]
        kk = [_dot(x, x, _NT) for x in kb]
        qkr = [_dot(q[i].astype(BF16), kb[i], _NT) for i in range(n)]
        between(0)
        glast = [jnp.where(first_half, g[C - 1:C, :], g[P - 1:P, :]) for g in gcol]
        decay = [jnp.where(causal, jnp.exp(jnp.where(causal, gcol[i] - grow[i], 0.0)), 0.0) for i in range(n)]
        low = [jnp.where(strict, bcol[i] * kk[i] * decay[i], 0.0) for i in range(n)]
        tinv = _inv_unit_lower(low, blk16, eye, between)
        eg = [jnp.exp(g) for g in gcol]
        rhs = [jnp.concatenate([v[i] * bcol[i], k[i] * (bcol[i] * eg[i])], axis=1) for i in range(n)]
        solb = [_dot3(tinv[i], rhs[i]).astype(BF16) for i in range(n)]
        between(3)
        qw = [_dot((qkr[i] * decay[i]).astype(BF16), solb[i]) for i in range(n)]
        ktail = [(k[i] * jnp.exp(glast[i] - gcol[i])).astype(BF16) for i in range(n)]
        for i, (p, hh) in enumerate(chains):
            qp_s[hh, pl.ds(r0[i], P), :] = (q[i] * eg[i] - qw[i][:, LANES:]).astype(BF16)
            oi_s[hh, pl.ds(r0[i], P), :] = qw[i][:, :LANES]
            for half in range(2):
                rows = slice(half * C, (half + 1) * C)
                c = 2 * p + half
                kwc = _dot(ktail[i][rows], solb[i][rows], _TN)
                c_s[hh, c] = kwc[:, :LANES]
                kw_s[hh, c] = kwc[:, LANES:].astype(BF16)
                a_s[hh, c] = jnp.broadcast_to(jnp.exp(glast[i][half * C:half * C + 1, :]), (8, LANES))

    def step(c, states):
        r0 = pl.multiple_of(c * C, C)
        hs = range(_GDN_HB)
        sb = [states[hh].astype(BF16) for hh in hs]
        ks = [_dot(kw_s[hh, c], sb[hh]) for hh in hs]
        os_ = [_dot(qp_s[hh, pl.ds(r0, C), :], sb[hh]) for hh in hs]
        new = [a_s[hh, c][0:1, :] * states[hh] - ks[hh] + c_s[hh, c] for hh in hs]
        for hh in hs:
            lanes = slice(hh * LANES, (hh + 1) * LANES)
            o = os_[hh] + oi_s[hh, pl.ds(r0, C), :]
            on = o * lax.rsqrt(jnp.mean(o * o, axis=-1, keepdims=True) + EPS) * gn_ref[...]
            z = z_ref[pl.ds(r0, C), lanes].astype(F32)
            o_ref[pl.ds(r0, C), lanes] = (on * _silu(z)).astype(o_ref.dtype)
        return tuple(new)

    per_it = 2 * _GDN_UNROLL
    n_it = S // (P * _GDN_UNROLL)

    def fused(it, states):
        st = [states]

        def between(k):
            st[0] = step((it - 1) * per_it + k, st[0])

        prep(it, between)
        return st[0]

    prep(0)
    states = lax.fori_loop(1, n_it, fused, tuple(jnp.zeros((GDN_DK, LANES), F32) for _ in range(_GDN_HB)))
    for k in range(per_it):
        states = step((n_it - 1) * per_it + k, states)


def _gdn(qkvn, gact, proj, gnorm, B, S):
    T = B * S
    nc = S // GDN_CHUNK
    hb = _GDN_HB
    ng = GDN_HEADS // hb
    w = hb * LANES
    blk = lambda off: pl.BlockSpec((S, w), lambda b, h: (b, off + h))
    return pl.pallas_call(
        functools.partial(_gdn_kernel, S=S),
        grid=(B, ng),
        in_specs=[
            blk(0), blk(ng), blk(2 * ng),
            pl.BlockSpec((S, LANES), lambda b, h: (b, 0)),
            blk(COL_GZ // w),
            pl.BlockSpec((1, LANES), lambda b, h: (0, 0)),
        ],
        out_specs=pl.BlockSpec((S, w), lambda b, h: (b, h)),
        out_shape=jax.ShapeDtypeStruct((T, GDN_HEADS * LANES), BF16),
        scratch_shapes=[
            pltpu.VMEM((hb, nc, GDN_DK, LANES), BF16),
            pltpu.VMEM((hb, nc, GDN_DK, LANES), F32),
            pltpu.VMEM((hb, S, LANES), BF16),
            pltpu.VMEM((hb, S, LANES), F32),
            pltpu.VMEM((hb, nc, 8, LANES), F32),
        ],
        compiler_params=_cparams(("parallel", "parallel"), VMEM_LIMIT_GDN),
        name="gdn",
    )(qkvn, qkvn, qkvn, gact, proj, gnorm)


AUG0 = FOX_HD
_NPAIR = FOX_HEADS * FOX_HD // LANES


def _fox_aug_selectors():
    import numpy as np
    sel = np.zeros((2 * _NPAIR, 3 * LANES, 2 * LANES), np.float32)
    for j in range(2 * _NPAIR):
        is_q = j < _NPAIR
        for slot in range(2):
            head = 2 * (j % _NPAIR) + slot
            c_lane, one_lane, sign = (AUG0, AUG0 + 3, 1.0) if is_q else (AUG0 + 3, AUG0, -1.0)
            for piece in range(3):
                sel[j, piece * LANES + LANE_F + head, slot * LANES + c_lane + piece] = sign
                sel[j, LANE_ONE, slot * LANES + one_lane + piece] = 1.0
    return jnp.asarray(sel, BF16)


def _fox_prep_kernel(x_ref, ga_ref, g_ref, sel_ref, o_ref):
    is_q = pl.program_id(1) < _NPAIR
    x = x_ref[...].astype(F32)
    row = lax.broadcasted_iota(jnp.int32, (LANES, LANES), 0)
    col = lax.broadcasted_iota(jnp.int32, (LANES, LANES), 1)
    grp = jnp.where(row // FOX_HD == col // FOX_HD, 1.0, 0.0).astype(BF16)
    xx = x * x
    hi = xx.astype(BF16)
    lo = (xx - hi.astype(F32)).astype(BF16)
    ms = _dot(jnp.concatenate([hi, lo], axis=1), jnp.concatenate([grp, grp], axis=0)) * (1.0 / FOX_HD)
    scale = jnp.where(is_q, LOG2E * FOX_HD ** -0.5, 1.0).astype(F32)
    xn = x * lax.rsqrt(ms + EPS) * (g_ref[0] * scale)
    xr = pltpu.roll(xn, FOX_HD, axis=1)
    aug = _dot(jnp.concatenate(_split3(ga_ref[...]), axis=1), sel_ref[0])
    lane = lax.broadcasted_iota(jnp.int32, x.shape, 1)
    o_ref[:, :LANES] = jnp.where(lane < FOX_HD, xn, aug[:, :LANES]).astype(o_ref.dtype)
    o_ref[:, LANES:] = jnp.where(lane < FOX_HD, xr, aug[:, LANES:]).astype(o_ref.dtype)


def _fox_prep(proj, gact, gains, T):
    tm = min(1024, T)
    ncol = 2 * _NPAIR
    return pl.pallas_call(
        _fox_prep_kernel,
        grid=(T // tm, ncol),
        in_specs=[
            pl.BlockSpec((tm, LANES), lambda i, j: (i, COL_FQKV // LANES + j)),
            pl.BlockSpec((tm, LANES), lambda i, j: (i, 0)),
            pl.BlockSpec((1, 1, LANES), lambda i, j: (j, 0, 0)),
            pl.BlockSpec((1, 3 * LANES, 2 * LANES), lambda i, j: (j, 0, 0)),
        ],
        out_specs=pl.BlockSpec((tm, 2 * LANES), lambda i, j: (i, j)),
        out_shape=jax.ShapeDtypeStruct((T, ncol * 2 * LANES), BF16),
        compiler_params=_cparams(("parallel", "arbitrary")),
        name="fox_prep",
    )(proj, gact, gains, _fox_aug_selectors())


def _vt_kernel(v_ref, o_ref):
    o_ref[...] = jnp.transpose(v_ref[...].astype(F32)).astype(o_ref.dtype)


def _fox_vt(proj, T):
    tm = min(2048, T)
    nhp = FOX_HEADS // 2
    return pl.pallas_call(
        _vt_kernel,
        grid=(T // tm, nhp),
        in_specs=[pl.BlockSpec((tm, LANES), lambda i, h: (i, COL_FQKV // LANES + 2 * nhp + h))],
        out_specs=pl.BlockSpec((LANES, tm), lambda i, h: (h, i)),
        out_shape=jax.ShapeDtypeStruct((nhp * LANES, T), BF16),
        compiler_params=_cparams(("parallel", "parallel")),
        name="fox_vt",
    )(proj)


def _fox_kernel(q_ref, k_ref, vt_ref, o_ref, *, tq):
    i = pl.program_id(2)
    hs = range(2)
    q = [q_ref[:, hh * LANES:(hh + 1) * LANES] for hh in hs]
    th = tq // 2
    keyi = lax.broadcasted_iota(jnp.int32, (th, th), 0)
    qryi = lax.broadcasted_iota(jnp.int32, (th, th), 1)
    tri = keyi <= qryi

    def kv_step(r0, nk, q0, carry, masked):
        ms, ls, accs = carry
        vt = [vt_ref[hh * FOX_HD:(hh + 1) * FOX_HD, pl.ds(r0, nk)] for hh in hs]
        s = [_dot(k_ref[pl.ds(r0, nk), hh * LANES:(hh + 1) * LANES], q[hh][q0:, :], _NT) for hh in hs]
        if masked:
            s = [jnp.where(tri, x, -jnp.inf) if x.shape[1] == th else
                 jnp.concatenate([jnp.where(tri, x[:, :th], -jnp.inf), x[:, th:]], axis=1) for x in s]
        m_old = [ms[hh][:, q0:] for hh in hs]
        m_new = [jnp.maximum(m_old[hh], jnp.max(s[hh], axis=0, keepdims=True)) for hh in hs]
        pr = [jnp.exp2(s[hh] - m_new[hh]) for hh in hs]
        alpha = [jnp.exp2(m_old[hh] - m_new[hh]) for hh in hs]
        pv = [_dot(vt[hh], pr[hh].astype(BF16)) for hh in hs]
        l_new = [alpha[hh] * ls[hh][:, q0:] + jnp.sum(pr[hh], axis=0, keepdims=True) for hh in hs]
        a_new = [alpha[hh] * accs[hh][:, q0:] + pv[hh] for hh in hs]
        keep = lambda old, new: [jnp.concatenate([old[hh][:, :q0], new[hh]], axis=1) if q0 else new[hh] for hh in hs]
        return keep(ms, m_new), keep(ls, l_new), keep(accs, a_new)

    init = ([jnp.full((1, tq), -jnp.inf, F32) for _ in hs], [jnp.zeros((1, tq), F32) for _ in hs],
            [jnp.zeros((FOX_HD, tq), F32) for _ in hs])
    carry = lax.fori_loop(0, i, lambda jb, c: kv_step(pl.multiple_of(jb * tq, tq), tq, 0, c, False), init)
    d0 = pl.multiple_of(i * tq, tq)
    carry = kv_step(d0, th, 0, carry, True)
    _, ls, accs = kv_step(pl.multiple_of(d0 + th, th), th, th, carry, True)
    ot = jnp.concatenate([accs[hh] / ls[hh] for hh in hs], axis=0)
    o_ref[...] = jnp.transpose(ot).astype(o_ref.dtype)


def _fox(qka, vt, B, S):
    T = B * S
    tq = min(1024, S)
    nq = S // tq
    nhp = FOX_HEADS // 2
    return pl.pallas_call(
        functools.partial(_fox_kernel, tq=tq),
        grid=(B, nhp, nq),
        in_specs=[
            pl.BlockSpec((tq, 2 * LANES), lambda b, h, i: (b * nq + i, h)),
            pl.BlockSpec((S, 2 * LANES), lambda b, h, i: (b, nhp + h)),
            pl.BlockSpec((LANES, S), lambda b, h, i: (h, b)),
        ],
        out_specs=pl.BlockSpec((tq, LANES), lambda b, h, i: (b * nq + i, h)),
        out_shape=jax.ShapeDtypeStruct((T, nhp * LANES), BF16),
        compiler_params=_cparams(("parallel", "parallel", "arbitrary")),
        name="fox",
    )(qka, qka, vt)


HALF = D_MODEL // 2


def _rms(x, gain):
    return x * lax.rsqrt(jnp.mean(x * x, axis=-1, keepdims=True) + EPS) * gain


def _pack_halves(x):
    lo = lax.bitcast_convert_type(x[:, :HALF].astype(BF16).astype(F32), jnp.uint32)
    hi = lax.bitcast_convert_type(x[:, HALF:].astype(BF16).astype(F32), jnp.uint32)
    return lax.shift_right_logical(lo, jnp.uint32(16)) | (hi & jnp.uint32(0xFFFF0000))


def _unpack_halves(w):
    lo = lax.bitcast_convert_type(lax.shift_left(w, jnp.uint32(16)), F32)
    hi = lax.bitcast_convert_type(w & jnp.uint32(0xFFFF0000), F32)
    return lo, hi


def _mix_kernel(oa_ref, of_ref, ma_ref, mf_ref, x_ref, wa_ref, wf_ref, wo_ref, g_ref, h_ref, hp_ref):
    ya = _dot(oa_ref[...], wa_ref[...])
    yf = _dot(of_ref[...], wf_ref[...])
    m = _sigmoid(ma_ref[...].astype(F32)) * ya + _sigmoid(mf_ref[...].astype(F32)) * yf
    h = x_ref[...] + _dot(m.astype(BF16), wo_ref[...])
    h_ref[...] = h
    hp_ref[...] = _pack_halves(_rms(h, g_ref[...]))


def _mix(o_a, o_f, proj, x2, wa, wf, wo, gain):
    T = x2.shape[0]
    tm = min(512, T)
    row = lambda c: pl.BlockSpec((tm, D_MODEL), lambda i: (i, c))
    full = pl.BlockSpec((D_MODEL, D_MODEL), lambda i: (0, 0))
    return pl.pallas_call(
        _mix_kernel,
        grid=(T // tm,),
        in_specs=[row(0), row(0), row(COL_MERGE // D_MODEL), row(COL_MERGE // D_MODEL + 1), row(0),
                  full, full, full, pl.BlockSpec((1, D_MODEL), lambda i: (0, 0))],
        out_specs=[row(0), pl.BlockSpec((tm, HALF), lambda i: (i, 0))],
        out_shape=[jax.ShapeDtypeStruct((T, D_MODEL), F32), jax.ShapeDtypeStruct((T, HALF), jnp.uint32)],
        compiler_params=_cparams(("parallel",)),
        name="mix_out",
    )(o_a, o_f, proj, proj, x2, wa, wf, wo, gain)


def _router_kernel(h_ref, g_ref, wr_ref, bias_ref, gates_ref, sel_ref, cnt_ref):
    @pl.when(pl.program_id(0) == 0)
    def _():
        cnt_ref[...] = jnp.zeros_like(cnt_ref)

    hn = _rms(h_ref[...], g_ref[...])
    hh = hn.astype(BF16)
    hl = (hn - hh.astype(F32)).astype(BF16)
    logits = _dot(wr_ref[0], hh, _NT) + _dot(wr_ref[0], hl, _NT) + _dot(wr_ref[1], hh, _NT)
    scores = _sigmoid(logits)
    work = scores + bias_ref[...]
    eidx = lax.broadcasted_iota(jnp.int32, work.shape, 0)
    sel = jnp.zeros(work.shape, F32)
    for _ in range(TOP_K):
        mx = jnp.max(work, axis=0, keepdims=True)
        first = jnp.min(jnp.where(work == mx, eidx, N_EXPERTS), axis=0, keepdims=True)
        onehot = eidx == first
        sel = jnp.where(onehot, 1.0, sel)
        work = jnp.where(onehot, -jnp.inf, work)
    s = jnp.where(sel > 0.0, scores, 0.0)
    gates_ref[...] = s / jnp.sum(s, axis=0, keepdims=True) * ROUTED_SCALE
    sel_ref[...] = sel
    tot = jnp.zeros((N_EXPERTS, 1), F32)
    for t0 in range(0, sel.shape[1], MOE_TT):
        n = jnp.sum(sel[:, t0:t0 + MOE_TT], axis=1, keepdims=True)
        tot = tot + jnp.floor((n + (SUB - 1)) * (1.0 / SUB)) * SUB
    cnt_ref[...] += jnp.broadcast_to(tot, cnt_ref.shape)


def _router(h1, gain, wr2, bias_col):
    T = h1.shape[0]
    tm = min(2 * MOE_TT, T)
    return pl.pallas_call(
        _router_kernel,
        grid=(T // tm,),
        in_specs=[
            pl.BlockSpec((tm, D_MODEL), lambda i: (i, 0)),
            pl.BlockSpec((1, D_MODEL), lambda i: (0, 0)),
            pl.BlockSpec((2, N_EXPERTS, D_MODEL), lambda i: (0, 0, 0)),
            pl.BlockSpec((N_EXPERTS, 1), lambda i: (0, 0)),
        ],
        out_specs=[
            pl.BlockSpec((N_EXPERTS, tm), lambda i: (0, i)),
            pl.BlockSpec((N_EXPERTS, tm), lambda i: (0, i)),
            pl.BlockSpec((N_EXPERTS, LANES), lambda i: (0, 0)),
        ],
        out_shape=[
            jax.ShapeDtypeStruct((N_EXPERTS, T), F32),
            jax.ShapeDtypeStruct((N_EXPERTS, T), F32),
            jax.ShapeDtypeStruct((N_EXPERTS, LANES), F32),
        ],
        compiler_params=_cparams(("arbitrary",)),
        name="router",
    )(h1, gain, wr2, bias_col)


MOE_TT = 256
SUB = 8
MOE_R = MOE_TT * TOP_K + N_EXPERTS * SUB


def _pos_kernel(sel_ref, gates_ref, cnt_ref, lpos_ref, w_ref, meta_ref, base_s, carry_s, *, tt, tme):
    r = lax.broadcasted_iota(jnp.int32, (N_EXPERTS, N_EXPERTS), 0)
    c = lax.broadcasted_iota(jnp.int32, (N_EXPERTS, N_EXPERTS), 1)
    below = jnp.where(r > c, 1.0, 0.0).astype(BF16)

    @pl.when(pl.program_id(0) == 0)
    def _():
        padded = jnp.floor((cnt_ref[...] + (tme - 1)) * (1.0 / tme)) * tme
        base_s[...] = _dot_lx(below, padded)
        carry_s[...] = jnp.zeros_like(carry_s)

    sel = sel_ref[...]
    gates = gates_ref[...]
    rr = lax.broadcasted_iota(jnp.int32, (tt, tt), 0)
    cc = lax.broadcasted_iota(jnp.int32, (tt, tt), 1)
    cs = _dot(sel.astype(BF16), jnp.where(rr <= cc, 1.0, 0.0).astype(BF16))
    n = jnp.sum(sel, axis=1, keepdims=True)
    npad = jnp.broadcast_to(jnp.floor((n + (SUB - 1)) * (1.0 / SUB)) * SUB, (N_EXPERTS, LANES))
    loff = _dot_lx(below, npad)
    meta_ref[0, 0] = base_s[...] + carry_s[...]
    meta_ref[0, 1] = npad
    meta_ref[0, 2] = loff
    carry_s[...] += npad
    lposd = cs - sel + jnp.concatenate([loff] * (tt // LANES), axis=1)
    eidx = lax.broadcasted_iota(jnp.int32, sel.shape, 0)
    rem = sel
    ps, ws = [], []
    for _ in range(TOP_K):
        first = jnp.min(jnp.where(rem > 0.0, eidx, N_EXPERTS), axis=0, keepdims=True)
        onehot = eidx == first
        ps.append(jnp.sum(jnp.where(onehot, lposd, 0.0), axis=0, keepdims=True))
        ws.append(jnp.sum(jnp.where(onehot, gates, 0.0), axis=0, keepdims=True))
        rem = jnp.where(onehot, 0.0, rem)
    lpos_ref[...] = jnp.concatenate(ps, axis=0).astype(jnp.int32)
    w_ref[...] = jnp.concatenate(ws, axis=0)


def _positions(sel, gates, cnt, tme):
    T = sel.shape[1]
    tt = MOE_TT
    blk = pl.BlockSpec((N_EXPERTS, tt), lambda i: (0, i))
    return pl.pallas_call(
        functools.partial(_pos_kernel, tt=tt, tme=tme),
        grid=(T // tt,),
        in_specs=[blk, blk, pl.BlockSpec((N_EXPERTS, LANES), lambda i: (0, 0))],
        out_specs=[pl.BlockSpec((TOP_K, tt), lambda i: (0, i)), pl.BlockSpec((TOP_K, tt), lambda i: (0, i)),
                   pl.BlockSpec((1, 3, N_EXPERTS, LANES), lambda i: (i, 0, 0, 0))],
        out_shape=[jax.ShapeDtypeStruct((TOP_K, T), jnp.int32), jax.ShapeDtypeStruct((TOP_K, T), F32),
                   jax.ShapeDtypeStruct((T // tt, 3, N_EXPERTS, LANES), F32)],
        scratch_shapes=[pltpu.VMEM((N_EXPERTS, LANES), F32), pltpu.VMEM((N_EXPERTS, LANES), F32)],
        compiler_params=_cparams(("arbitrary",)),
        name="moe_positions",
    )(sel, gates, cnt)


PIECE = 2 * SUB
NP16 = MOE_R // PIECE
TAB_G16, TAB_L16 = 0, NP16
TAB_G8, TAB_L8 = 2 * NP16, 2 * NP16 + N_EXPERTS
TAB_N16 = 2 * NP16 + 2 * N_EXPERTS
TAB_N8, TAB_USED = TAB_N16 + 1, TAB_N16 + 2
TAB_LEN = TAB_N16 + 8


def _piece_table(seg):
    g, n, l = seg[:, 0], seg[:, 1], seg[:, 2]
    take = lambda a, idx: jnp.take_along_axis(a, idx, axis=1)
    nfull = n // PIECE
    cend = jnp.cumsum(nfull, axis=1)
    j16 = jnp.arange(NP16, dtype=jnp.int32)
    e16 = jnp.minimum(jnp.sum((cend[:, None, :] <= j16[None, :, None]).astype(jnp.int32), axis=2), N_EXPERTS - 1)
    off16 = PIECE * (j16[None, :] - take(cend - nfull, e16))
    has8 = (n // SUB) % 2
    hend = jnp.cumsum(has8, axis=1)
    j8 = jnp.arange(N_EXPERTS, dtype=jnp.int32)
    e8 = jnp.minimum(jnp.sum((hend[:, None, :] <= j8[None, :, None]).astype(jnp.int32), axis=2), N_EXPERTS - 1)
    off8 = PIECE * take(nfull, e8)
    used = l[:, -1:] + n[:, -1:]
    cols = [take(g, e16) + off16, take(l, e16) + off16, take(g, e8) + off8, take(l, e8) + off8,
            cend[:, -1:], hend[:, -1:], used, jnp.zeros((seg.shape[0], TAB_LEN - TAB_USED - 1), jnp.int32)]
    return jnp.concatenate(cols, axis=1).astype(jnp.int32).reshape(seg.shape[0], 1, TAB_LEN)


def _segment_copies(tab_ref, src_of, dst_of, sem):
    def pieces(g0, l0, rows):
        def body(j, carry):
            g = pl.multiple_of(tab_ref[0, 0, g0 + j], SUB)
            l = pl.multiple_of(tab_ref[0, 0, l0 + j], SUB)
            pltpu.make_async_copy(src_of(g, l, rows), dst_of(g, l, rows), sem).start()
            return carry
        return body

    lax.fori_loop(0, tab_ref[0, 0, TAB_N16], pieces(TAB_G16, TAB_L16, PIECE), 0)
    lax.fori_loop(0, tab_ref[0, 0, TAB_N8], pieces(TAB_G8, TAB_L8, SUB), 0)


def _wait_segment_copies(tab_ref, src_of, dst_of, sem):
    used = tab_ref[0, 0, TAB_USED]

    def unit(rows):
        def body(i, carry):
            pltpu.make_async_copy(src_of(0, 0, rows), dst_of(0, 0, rows), sem).wait()
            return carry
        return body

    lax.fori_loop(0, lax.shift_right_logical(used, 7), unit(LANES), 0)
    lax.fori_loop(0, lax.shift_right_logical(used, 3) & (LANES // SUB - 1), unit(SUB), 0)


_ROW_CHUNK = MOE_R // 4


def _pack2(lo, hi):
    lo = lax.bitcast_convert_type(lo.astype(BF16).astype(F32), jnp.uint32)
    hi = lax.bitcast_convert_type(hi.astype(BF16).astype(F32), jnp.uint32)
    return lax.shift_right_logical(lo, jnp.uint32(16)) | (hi & jnp.uint32(0xFFFF0000))


def _dispatch_kernel(meta_ref, seg_ref, lpos_ref, hp_ref, xs_ref, zero_s, sort_s, sem, zsem, *, tt, tme, n_tiles):
    @pl.when(pl.program_id(0) == 0)
    def _():
        zero_s[...] = jnp.zeros_like(zero_s)
        for e in range(N_EXPERTS):
            pltpu.make_async_copy(zero_s, xs_ref.at[pl.ds(pl.multiple_of(meta_ref[e], SUB), tme)], zsem).start()
        for e in range(N_EXPERTS):
            pltpu.make_async_copy(zero_s, xs_ref.at[pl.ds(pl.multiple_of(meta_ref[e], SUB), tme)], zsem).wait()

        def zero_tile(t, carry):
            cp = pltpu.make_async_copy(zero_s, xs_ref.at[pl.ds(pl.multiple_of(t * tme, tme), tme)], zsem)
            cp.start()
            cp.wait()
            return carry

        lax.fori_loop(meta_ref[N_EXPERTS], n_tiles, zero_tile, 0)

    lo, hi = _unpack_halves(hp_ref[...])
    lo = lo.astype(BF16)
    hi = hi.astype(BF16)
    lp = lpos_ref[...].astype(jnp.int16)
    one = jnp.ones((1, 1), BF16)
    for rc in range(MOE_R // _ROW_CHUNK):
        rowi = (lax.broadcasted_iota(jnp.int32, (_ROW_CHUNK, tt), 0) + rc * _ROW_CHUNK).astype(jnp.int16)
        pm = jnp.zeros((_ROW_CHUNK, tt), BF16)
        for k in range(TOP_K):
            pm = jnp.where(rowi == lp[k:k + 1, :], one, pm)
        sort_s[rc * _ROW_CHUNK:(rc + 1) * _ROW_CHUNK, :] = _pack2(_dot(pm, lo), _dot(pm, hi))

    src_of = lambda g, l, rows: sort_s.at[pl.ds(l, rows)]
    dst_of = lambda g, l, rows: xs_ref.at[pl.ds(g, rows)]
    _segment_copies(seg_ref, src_of, dst_of, sem)
    _wait_segment_copies(seg_ref, src_of, dst_of, sem)


def _dispatch(meta, seg, lpos, hp, n_tiles, tme):
    T = hp.shape[0]
    tt = MOE_TT
    return pl.pallas_call(
        functools.partial(_dispatch_kernel, tt=tt, tme=tme, n_tiles=n_tiles),
        grid_spec=pltpu.PrefetchScalarGridSpec(
            num_scalar_prefetch=1,
            grid=(T // tt,),
            in_specs=[
                pl.BlockSpec((1, 1, TAB_LEN), lambda i, f: (i, 0, 0), memory_space=pltpu.SMEM),
                pl.BlockSpec((TOP_K, tt), lambda i, f: (0, i)),
                pl.BlockSpec((tt, HALF), lambda i, f: (i, 0)),
            ],
            out_specs=pl.BlockSpec(memory_space=pl.ANY),
            scratch_shapes=[pltpu.VMEM((tme, HALF), jnp.uint32), pltpu.VMEM((MOE_R, HALF), jnp.uint32),
                            pltpu.SemaphoreType.DMA, pltpu.SemaphoreType.DMA],
        ),
        out_shape=jax.ShapeDtypeStruct((n_tiles * tme, HALF), jnp.uint32),
        compiler_params=_cparams(("arbitrary",)),
        name="moe_dispatch",
    )(meta, seg, lpos, hp)


def _expert_kernel(te_ref, nu_ref, xs_ref, wgu_ref, wd_ref, ys_ref):
    @pl.when(pl.program_id(0) < nu_ref[0])
    def _():
        lo, hi = _unpack_halves(xs_ref[...])
        gu = _dot(lo.astype(BF16), wgu_ref[0, :HALF, :]) + _dot(hi.astype(BF16), wgu_ref[0, HALF:, :])
        hid = _silu(gu[:, :D_EXPERT]) * gu[:, D_EXPERT:]
        ys_ref[...] = _pack_halves(_dot(hid.astype(BF16), wd_ref[0]))

    @pl.when(pl.program_id(0) >= nu_ref[0])
    def _():
        ys_ref[...] = jnp.zeros_like(ys_ref)


def _experts(tile_e, n_used, xs, wgu, wd, n_tiles, tme):
    clamp = lambda i, te, nu: (jnp.minimum(i, nu[0] - 1), 0)
    return pl.pallas_call(
        _expert_kernel,
        grid_spec=pltpu.PrefetchScalarGridSpec(
            num_scalar_prefetch=2,
            grid=(n_tiles,),
            in_specs=[
                pl.BlockSpec((tme, HALF), clamp),
                pl.BlockSpec((1, D_MODEL, 2 * D_EXPERT), lambda i, te, nu: (te[i], 0, 0)),
                pl.BlockSpec((1, D_EXPERT, D_MODEL), lambda i, te, nu: (te[i], 0, 0)),
            ],
            out_specs=pl.BlockSpec((tme, HALF), lambda i, te, nu: (i, 0)),
        ),
        out_shape=jax.ShapeDtypeStruct((n_tiles * tme, HALF), jnp.uint32),
        compiler_params=_cparams(("arbitrary",)),
        name="moe_experts",
    )(tile_e, n_used, xs, wgu, wd)


def _tail_kernel(seg_ref, ys_ref, lpos_ref, w_ref, h_ref, p_ref, gf_ref, wsgu_ref, wsd_ref, wple_ref, gple_ref,
                 gpg_ref, wpg_ref, o_ref, buf, sem, *, tt):
    src_of = lambda g, l, rows: ys_ref.at[pl.ds(g, rows)]
    dst_of = lambda g, l, rows: buf.at[pl.ds(l, rows)]
    _segment_copies(seg_ref, src_of, dst_of, sem)

    used = seg_ref[0, 0, TAB_USED]

    def clear(j, carry):
        buf[pl.ds(pl.multiple_of(j * SUB, SUB), SUB), :] = jnp.zeros((SUB, HALF), jnp.uint32)
        return carry

    lax.fori_loop(lax.shift_right_logical(used, 3), MOE_R // SUB, clear, 0)

    h = h_ref[...]
    hn = _rms(h, gf_ref[...]).astype(BF16)
    sgu = _dot(hn, wsgu_ref[...])
    shared = _dot((_silu(sgu[:, :D_SHARED]) * sgu[:, D_SHARED:]).astype(BF16), wsd_ref[...])
    e = _rms(_dot(p_ref[...].astype(BF16), wple_ref[...]), gple_ref[...])
    _wait_segment_copies(seg_ref, src_of, dst_of, sem)

    lp = lpos_ref[...].astype(jnp.int16)
    wt = w_ref[...].astype(BF16)
    acc_lo = jnp.zeros((tt, HALF), F32)
    acc_hi = jnp.zeros((tt, HALF), F32)
    for rc in range(MOE_R // _ROW_CHUNK):
        coli = (lax.broadcasted_iota(jnp.int32, (tt, _ROW_CHUNK), 1) + rc * _ROW_CHUNK).astype(jnp.int16)
        g = jnp.zeros((tt, _ROW_CHUNK), BF16)
        for k in range(TOP_K):
            g = jnp.where(coli == lp[:, k:k + 1], wt[:, k:k + 1], g)
        lo, hi = _unpack_halves(buf[rc * _ROW_CHUNK:(rc + 1) * _ROW_CHUNK, :])
        acc_lo = acc_lo + _dot(g, lo.astype(BF16))
        acc_hi = acc_hi + _dot(g, hi.astype(BF16))
    h2 = h + jnp.concatenate([acc_lo, acc_hi], axis=1) + shared
    gte = _sigmoid(_dot(_rms(h2, gpg_ref[...]).astype(BF16), wpg_ref[...]))
    o_ref[...] = h2 + gte * e


def _tail(seg, ys, lpos_tok, w_tok, h1, p2, gf, wsgu, wsd, wple, gple, gpg, wpg):
    T = h1.shape[0]
    tt = MOE_TT
    row = lambda n: pl.BlockSpec((tt, n), lambda i: (i, 0))
    full = lambda a: pl.BlockSpec(a.shape, lambda i: (0,) * a.ndim)
    return pl.pallas_call(
        functools.partial(_tail_kernel, tt=tt),
        grid=(T // tt,),
        in_specs=[
            pl.BlockSpec((1, 1, TAB_LEN), lambda i: (i, 0, 0), memory_space=pltpu.SMEM),
            pl.BlockSpec(memory_space=pl.ANY),
            row(TOP_K), row(TOP_K), row(D_MODEL), row(PLE_DIM),
            full(gf), full(wsgu), full(wsd), full(wple), full(gple), full(gpg), full(wpg),
        ],
        out_specs=row(D_MODEL),
        out_shape=jax.ShapeDtypeStruct((T, D_MODEL), F32),
        scratch_shapes=[pltpu.VMEM((MOE_R, HALF), jnp.uint32), pltpu.SemaphoreType.DMA],
        compiler_params=_cparams(("arbitrary",)),
        name="moe_combine_ple",
    )(seg, ys, lpos_tok, w_tok, h1, p2, gf, wsgu, wsd, wple, gple, gpg, wpg)


def _stages(x, p, norm_mix, w_in, conv_w, a_log, dt_bias, gdn_norm, fox_f_bias, q_norm, k_norm,
            w_branch_gdn, w_branch_fox, w_out, norm_ffn, w_router, router_bias,
            we_gate, we_up, we_down, ws_gate, ws_up, ws_down, w_ple, ple_norm, ple_gate_norm, w_ple_gate):
    B, S, _ = x.shape
    T = B * S
    st = {}
    wi = w_in[0]
    o0 = 3 * GDN_HEADS * GDN_DK
    o1 = o0 + GDN_HEADS * GDN_DK
    o2 = o1 + GDN_HEADS
    o3 = o2 + GDN_HEADS
    o4 = o3 + 3 * FOX_HEADS * FOX_HD
    o5 = o4 + FOX_HEADS
    w_big = jnp.concatenate([wi[:, :o1], wi[:, o3:o4], wi[:, o5:]], axis=1).astype(BF16)
    w_small = jnp.concatenate([wi[:, o1:o3], wi[:, o4:o5],
                               jnp.zeros((D_MODEL, LANES - 2 * GDN_HEADS - FOX_HEADS), F32)], axis=1).astype(BF16)
    proj, small = _in_proj(x.reshape(T, D_MODEL), norm_mix[0].reshape(1, D_MODEL), w_big, w_small)
    st["proj_big"] = proj
    st["small"] = small

    pad = lambda v, off: jnp.zeros((1, LANES), F32).at[0, off:off + v.shape[0]].set(v)
    alog_row = pad(a_log[0], LANE_A)
    bias_row = pad(dt_bias[0], LANE_A) + pad(fox_f_bias[0], LANE_F)
    gact = _gates(small, alog_row, bias_row, B, S)
    st["gact"] = gact

    qkvn = _gdn_prep(proj, conv_w[0], B, S)
    st["qkvn"] = qkvn
    o_a = _gdn(qkvn, gact, proj, gdn_norm[0].reshape(1, LANES), B, S)
    st["o_a"] = o_a

    nrep = FOX_HEADS * FOX_HD // LANES
    gains = jnp.concatenate([jnp.tile(q_norm[0], (nrep, LANES // FOX_HD)),
                             jnp.tile(k_norm[0], (nrep, LANES // FOX_HD))], axis=0).reshape(2 * nrep, 1, LANES)
    qka = _fox_prep(proj, gact, gains, T)
    o_f = _fox(qka, _fox_vt(proj, T), B, S)
    st["o_f"] = o_f

    x2 = x.reshape(T, D_MODEL)
    gf = norm_ffn[0].reshape(1, D_MODEL)
    h1, hp = _mix(o_a, o_f, proj, x2, w_branch_gdn[0].astype(BF16), w_branch_fox[0].astype(BF16),
                  w_out[0].astype(BF16), gf)
    st["h1"] = h1

    wrt = w_router[0].T
    wr_hi = wrt.astype(BF16)
    wr2 = jnp.stack([wr_hi, (wrt - wr_hi.astype(F32)).astype(BF16)])
    gates, sel, cnt = _router(h1, gf, wr2, router_bias[0].reshape(N_EXPERTS, 1))
    st["gates_t"] = gates

    tme = 512 if T * TOP_K // N_EXPERTS >= 2048 else 64
    lpos, wts, segf = _positions(sel, gates, cnt, tme)
    nt_tok = T // MOE_TT
    seg = _piece_table(segf[:, :, :, 0].astype(jnp.int32))
    cnt1 = cnt[:, 0].astype(jnp.int32)
    padded = (cnt1 + (tme - 1)) // tme * tme
    ends = jnp.cumsum(padded)
    n_tiles = (T * TOP_K + (SUB - 1) * N_EXPERTS * nt_tok) // tme + N_EXPERTS
    tile_start = jnp.arange(n_tiles, dtype=jnp.int32) * tme
    tile_e = jnp.minimum(jnp.sum((ends[None, :] <= tile_start[:, None]).astype(jnp.int32), axis=1), N_EXPERTS - 1)
    n_used = (ends[-1:] // tme).astype(jnp.int32)
    fill = (ends - padded + cnt1).astype(jnp.int32)
    xs = _dispatch(jnp.concatenate([fill, n_used]), seg, lpos, hp, n_tiles, tme)
    wgu = jnp.concatenate([we_gate[0], we_up[0]], axis=2).astype(BF16)
    ys = _experts(tile_e, n_used, xs, wgu, we_down[0].astype(BF16), n_tiles, tme)

    row = lambda v: v.reshape(1, D_MODEL)
    wsgu = jnp.concatenate([ws_gate[0], ws_up[0]], axis=1).astype(BF16)
    out = _tail(seg, ys, lpos.T, wts.T, h1, p[0].reshape(T, PLE_DIM), gf, wsgu, ws_down[0].astype(BF16),
                w_ple[0].astype(BF16), row(ple_norm[0]), row(ple_gate_norm[0]), w_ple_gate[0].astype(BF16))
    st["out"] = out.reshape(B, S, D_MODEL)
    return st


def kernel(x, p, norm_mix, w_in, conv_w, a_log, dt_bias, gdn_norm, fox_f_bias, q_norm, k_norm, w_branch_gdn, w_branch_fox, w_out, norm_ffn, w_router, router_bias, we_gate, we_up, we_down, ws_gate, ws_up, ws_down, w_ple, ple_norm, ple_gate_norm, w_ple_gate):
    return _stages(x, p, norm_mix, w_in, conv_w, a_log, dt_bias, gdn_norm, fox_f_bias, q_norm, k_norm,
                   w_branch_gdn, w_branch_fox, w_out, norm_ffn, w_router, router_bias,
                   we_gate, we_up, we_down, ws_gate, ws_up, ws_down, w_ple, ple_norm, ple_gate_norm,
                   w_ple_gate)["out"]
```

```python
import functools

import jax
import jax.numpy as jnp
from jax import lax
from jax.experimental import pallas as pl
from jax.experimental.pallas import tpu as pltpu

F32 = jnp.float32
BF16 = jnp.bfloat16

D_MODEL = 1024
PLE_DIM = 256
EPS = 1e-6
GDN_HEADS = 8
GDN_DK = 128
GDN_CONV = 4
GDN_CHUNK = 64
FOX_HEADS = 16
FOX_HD = 64
N_EXPERTS = 64
TOP_K = 8
D_EXPERT = 256
D_SHARED = 256
ROUTED_SCALE = 2.5

LANES = 128
COL_GQKV = 0
COL_GZ = 3072
COL_FQKV = 4096
COL_MERGE = 7168
N_BIG = 9216
LANE_A = 0
LANE_B = 8
LANE_F = 16
LANE_ONE = 127
LOG2E = 1.4426950408889634

VMEM_LIMIT = 48 * 1024 * 1024
VMEM_LIMIT_GDN = 56 * 1024 * 1024


def _cparams(sem, vmem=VMEM_LIMIT):
    return pltpu.CompilerParams(dimension_semantics=sem, vmem_limit_bytes=vmem)


def _split3(x):
    h = x.astype(BF16)
    r = x - h.astype(F32)
    m = r.astype(BF16)
    l = (r - m.astype(F32)).astype(BF16)
    return h, m, l


def _dot(a, b, dims=(((1,), (0,)), ((), ()))):
    return lax.dot_general(a, b, dims, preferred_element_type=F32)


_NT = (((1,), (1,)), ((), ()))
_TN = (((0,), (0,)), ((), ()))


def _dot_xl(x, sel, dims=(((1,), (0,)), ((), ()))):
    h, m, l = _split3(x)
    return _dot(h, sel, dims) + _dot(m, sel, dims) + _dot(l, sel, dims)


def _dot_lx(sel, x, dims=(((1,), (0,)), ((), ()))):
    h, m, l = _split3(x)
    return _dot(sel, h, dims) + _dot(sel, m, dims) + _dot(sel, l, dims)


def _dot3(a, b):
    ah = a.astype(BF16)
    al = (a - ah.astype(F32)).astype(BF16)
    bh = b.astype(BF16)
    bl = (b - bh.astype(F32)).astype(BF16)
    return _dot(jnp.concatenate([ah, ah, al], axis=1), jnp.concatenate([bh, bl, bh], axis=0))


def _sigmoid(x):
    return 1.0 / (1.0 + jnp.exp(-x))


def _silu(x):
    return x * _sigmoid(x)


def _softplus(x):
    return jnp.maximum(x, 0.0) + jnp.log1p(jnp.exp(-jnp.abs(x)))


def _in_proj_kernel(x_ref, g_ref, wb_ref, ws_ref, ob_ref, os_ref, xn_ref):
    @pl.when(pl.program_id(1) == 0)
    def _():
        x = x_ref[...]
        ms = jnp.mean(x * x, axis=-1, keepdims=True)
        xn = (x * lax.rsqrt(ms + EPS) * g_ref[...]).astype(BF16)
        xn_ref[...] = xn
        os_ref[...] = _dot(xn, ws_ref[...])

    ob_ref[...] = _dot(xn_ref[...], wb_ref[...]).astype(ob_ref.dtype)


def _in_proj(x2, gain, w_big, w_small):
    T = x2.shape[0]
    tm = min(1024, T)
    tn = 1024
    return pl.pallas_call(
        _in_proj_kernel,
        grid=(T // tm, N_BIG // tn),
        in_specs=[
            pl.BlockSpec((tm, D_MODEL), lambda i, j: (i, 0)),
            pl.BlockSpec((1, D_MODEL), lambda i, j: (0, 0)),
            pl.BlockSpec((D_MODEL, tn), lambda i, j: (0, j)),
            pl.BlockSpec((D_MODEL, LANES), lambda i, j: (0, 0)),
        ],
        out_specs=[
            pl.BlockSpec((tm, tn), lambda i, j: (i, j)),
            pl.BlockSpec((tm, LANES), lambda i, j: (i, 0)),
        ],
        out_shape=[
            jax.ShapeDtypeStruct((T, N_BIG), BF16),
            jax.ShapeDtypeStruct((T, LANES), F32),
        ],
        scratch_shapes=[pltpu.VMEM((tm, D_MODEL), BF16)],
        compiler_params=_cparams(("parallel", "arbitrary")),
        name="in_proj",
    )(x2, gain, w_big, w_small)


def _gates_kernel(s_ref, alog_ref, bias_ref, o_ref, carry_ref, *, sb):
    @pl.when(pl.program_id(1) == 0)
    def _():
        carry_ref[...] = jnp.zeros_like(carry_ref)

    x = s_ref[...] + bias_ref[...]
    lane = lax.broadcasted_iota(jnp.int32, x.shape, 1)
    e = jnp.log1p(jnp.exp(-jnp.abs(x)))
    log_alpha = -jnp.exp(alog_ref[...]) * (jnp.maximum(x, 0.0) + e)
    beta = _sigmoid(x)
    log_f = -(jnp.maximum(-x, 0.0) + e)
    is_a = lane < LANE_B
    is_b = jnp.logical_and(lane >= LANE_B, lane < LANE_F)
    is_f = jnp.logical_and(lane >= LANE_F, lane < LANE_F + FOX_HEADS)
    val = jnp.where(is_a, log_alpha, jnp.where(is_f, log_f, 0.0))
    row = lax.broadcasted_iota(jnp.int32, (sb, sb), 0)
    col = lax.broadcasted_iota(jnp.int32, (sb, sb), 1)
    tri = row >= col
    tri_full = jnp.where(tri, 1.0, 0.0).astype(BF16)
    tri_chunk = jnp.where(jnp.logical_and(tri, row // GDN_CHUNK == col // GDN_CHUNK), 1.0, 0.0).astype(BF16)
    h, m, l = _split3(val)
    cs_full = _dot(tri_full, h) + _dot(tri_full, m) + _dot(tri_full, l) + carry_ref[0:1, :]
    cs_chunk = _dot(tri_chunk, h) + _dot(tri_chunk, m) + _dot(tri_chunk, l)
    rest = jnp.where(is_f, LOG2E * cs_full, jnp.where(lane == LANE_ONE, 1.0, 0.0))
    o_ref[...] = jnp.where(is_a, cs_chunk, jnp.where(is_b, beta, rest))
    carry_ref[...] = jnp.broadcast_to(cs_full[sb - 1:sb, :], carry_ref.shape)


def _gates(small, alog_row, bias_row, B, S):
    sb = min(256, S)
    nsb = S // sb
    return pl.pallas_call(
        functools.partial(_gates_kernel, sb=sb),
        grid=(B, nsb),
        in_specs=[
            pl.BlockSpec((sb, LANES), lambda b, j: (b * nsb + j, 0)),
            pl.BlockSpec((1, LANES), lambda b, j: (0, 0)),
            pl.BlockSpec((1, LANES), lambda b, j: (0, 0)),
        ],
        out_specs=pl.BlockSpec((sb, LANES), lambda b, j: (b * nsb + j, 0)),
        out_shape=jax.ShapeDtypeStruct(small.shape, F32),
        scratch_shapes=[pltpu.VMEM((8, LANES), F32)],
        compiler_params=_cparams(("parallel", "arbitrary")),
        name="gates",
    )(small, alog_row, bias_row)


def _gdn_prep_kernel(x_ref, w_ref, o_ref, pad_ref, *, S, rb):
    j = pl.program_id(1)
    pad_ref[0:8, :] = jnp.zeros((8, LANES), F32)
    pad_ref[8:8 + S, :] = x_ref[...].astype(F32)
    is_qk = j < 2 * GDN_HEADS
    scale = jnp.where(j < GDN_HEADS, GDN_DK ** -0.5, 1.0).astype(F32)
    for r0 in range(0, S, rb):
        acc = jnp.zeros((rb, LANES), F32)
        for t in range(GDN_CONV):
            off = 8 - (GDN_CONV - 1) + t
            acc = acc + w_ref[t:t + 1, :] * pad_ref[r0 + off:r0 + off + rb, :]
        y = _silu(acc)
        n = y * lax.rsqrt(jnp.sum(y * y, axis=-1, keepdims=True) + EPS) * scale
        o_ref[r0:r0 + rb, :] = jnp.where(is_qk, n, y).astype(o_ref.dtype)


def _gdn_prep(proj, conv_w, B, S):
    T = B * S
    ncol = 3 * GDN_HEADS
    rb = min(256, S)
    return pl.pallas_call(
        functools.partial(_gdn_prep_kernel, S=S, rb=rb),
        grid=(B, ncol),
        in_specs=[
            pl.BlockSpec((S, LANES), lambda b, j: (b, COL_GQKV // LANES + j)),
            pl.BlockSpec((GDN_CONV, LANES), lambda b, j: (0, j)),
        ],
        out_specs=pl.BlockSpec((S, LANES), lambda b, j: (b, j)),
        out_shape=jax.ShapeDtypeStruct((T, ncol * LANES), BF16),
        scratch_shapes=[pltpu.VMEM((S + 8, LANES), F32)],
        compiler_params=_cparams(("parallel", "parallel")),
        name="gdn_prep",
    )(proj, conv_w)


_PAIR = 2 * GDN_CHUNK


def _inv_unit_lower(lows, blk16, eye, between=lambda k: None):
    mm = lambda a, b: _dot(a.astype(BF16), b.astype(BF16))
    xs = [jnp.where(blk16, -low, 0.0) for low in lows]
    offs = [jnp.where(blk16, 0.0, low) for low in lows]
    x2 = [mm(x, x) for x in xs]
    x4 = [mm(a, a) for a in x2]
    d1 = [mm(eye + x, eye + a) for x, a in zip(xs, x2)]
    between(1)
    x8 = [mm(a, a) for a in x4]
    d2 = [mm(eye + a, eye + b) for a, b in zip(x4, x8)]
    dinv = [mm(a, b) for a, b in zip(d1, d2)]
    ns = [mm(d, o) for d, o in zip(dinv, offs)]
    between(2)
    n2 = [mm(n, n) for n in ns]
    t1 = [mm(eye - n, eye + m) for n, m in zip(ns, n2)]
    return [mm(t, d) for t, d in zip(t1, dinv)]


_GDN_HB = 4
_GDN_UNROLL = 2


def _gdn_kernel(q_ref, k_ref, v_ref, ga_ref, z_ref, gn_ref, o_ref, kw_s, c_s, qp_s, oi_s, a_s, *, S):
    hg = pl.program_id(1)
    P = _PAIR
    C = GDN_CHUNK
    row = lax.broadcasted_iota(jnp.int32, (P, P), 0)
    col = lax.broadcasted_iota(jnp.int32, (P, P), 1)
    same = row // C == col // C
    causal = jnp.logical_and(same, row >= col)
    strict = jnp.logical_and(same, row > col)
    blk16 = row // 16 == col // 16
    eye = jnp.where(row == col, 1.0, 0.0).astype(F32)
    first_half = row < C
    heads = [hg * _GDN_HB + hh for hh in range(_GDN_HB)]
    row2 = lax.broadcasted_iota(jnp.int32, (2 * LANES, 2 * LANES), 0) % LANES
    col2 = lax.broadcasted_iota(jnp.int32, (2 * LANES, 2 * LANES), 1)
    colr = lax.broadcasted_iota(jnp.int32, (P, 2 * LANES), 1) % LANES
    sel_gb = [jnp.where(row2 == jnp.where(col2 < LANES, LANE_A + h, LANE_B + h), 1.0, 0.0).astype(BF16) for h in heads]
    sel_gr = [jnp.where(colr == LANE_A + h, 1.0, 0.0).astype(BF16) for h in heads]

    def prep(it, between=lambda k: None):
        it = jnp.asarray(it, jnp.int32)
        chains = [(it * _GDN_UNROLL + u, hh) for u in range(_GDN_UNROLL) for hh in range(_GDN_HB)]
        r0 = [pl.multiple_of(p * P, P) for p, _ in chains]
        ln = [slice(hh * LANES, (hh + 1) * LANES) for _, hh in chains]
        hh_ = [hh for _, hh in chains]
        n = len(chains)
        q = [q_ref[pl.ds(r0[i], P), ln[i]].astype(F32) for i in range(n)]
        k = [k_ref[pl.ds(r0[i], P), ln[i]].astype(F32) for i in range(n)]
        v = [v_ref[pl.ds(r0[i], P), ln[i]].astype(F32) for i in range(n)]
        ga = [ga_ref[pl.ds(r0[i], P), :] for i in range(n)]
        gah = [x.astype(BF16) for x in ga]
        gam = [(ga[i] - gah[i].astype(F32)).astype(BF16) for i in range(n)]
        ga2 = [jnp.concatenate([gah[i], gam[i]], axis=1) for i in range(n)]
        gb = [_dot(ga2[i], sel_gb[hh_[i]]) for i in range(n)]
        gcol = [x[:, :LANES] for x in gb]
        bcol = [x[:, LANES:] for x in gb]
        grow = [_dot(sel_gr[hh_[i]], ga2[i], _NT) for i in range(n)]
        kb = [x.astype(BF16) for x in k]
        kk = [_dot(x, x, _NT) for x in kb]
        qkr = [_dot(q[i].astype(BF16), kb[i], _NT) for i in range(n)]
        between(0)
        glast = [jnp.where(first_half, g[C - 1:C, :], g[P - 1:P, :]) for g in gcol]
        decay = [jnp.where(causal, jnp.exp(jnp.where(causal, gcol[i] - grow[i], 0.0)), 0.0) for i in range(n)]
        low = [jnp.where(strict, bcol[i] * kk[i] * decay[i], 0.0) for i in range(n)]
        tinv = _inv_unit_lower(low, blk16, eye, between)
        eg = [jnp.exp(g) for g in gcol]
        rhs = [jnp.concatenate([v[i] * bcol[i], k[i] * (bcol[i] * eg[i])], axis=1) for i in range(n)]
        solb = [_dot3(tinv[i], rhs[i]).astype(BF16) for i in range(n)]
        between(3)
        qw = [_dot((qkr[i] * decay[i]).astype(BF16), solb[i]) for i in range(n)]
        ktail = [(k[i] * jnp.exp(glast[i] - gcol[i])).astype(BF16) for i in range(n)]
        for i, (p, hh) in enumerate(chains):
            qp_s[hh, pl.ds(r0[i], P), :] = (q[i] * eg[i] - qw[i][:, LANES:]).astype(BF16)
            oi_s[hh, pl.ds(r0[i], P), :] = qw[i][:, :LANES]
            for half in range(2):
                rows = slice(half * C, (half + 1) * C)
                c = 2 * p + half
                kwc = _dot(ktail[i][rows], solb[i][rows], _TN)
                c_s[hh, c] = kwc[:, :LANES]
                kw_s[hh, c] = kwc[:, LANES:].astype(BF16)
                a_s[hh, c] = jnp.broadcast_to(jnp.exp(glast[i][half * C:half * C + 1, :]), (8, LANES))

    def step(c, states):
        r0 = pl.multiple_of(c * C, C)
        hs = range(_GDN_HB)
        sb = [states[hh].astype(BF16) for hh in hs]
        ks = [_dot(kw_s[hh, c], sb[hh]) for hh in hs]
        os_ = [_dot(qp_s[hh, pl.ds(r0, C), :], sb[hh]) for hh in hs]
        new = [a_s[hh, c][0:1, :] * states[hh] - ks[hh] + c_s[hh, c] for hh in hs]
        for hh in hs:
            lanes = slice(hh * LANES, (hh + 1) * LANES)
            o = os_[hh] + oi_s[hh, pl.ds(r0, C), :]
            on = o * lax.rsqrt(jnp.mean(o * o, axis=-1, keepdims=True) + EPS) * gn_ref[...]
            z = z_ref[pl.ds(r0, C), lanes].astype(F32)
            o_ref[pl.ds(r0, C), lanes] = (on * _silu(z)).astype(o_ref.dtype)
        return tuple(new)

    per_it = 2 * _GDN_UNROLL
    n_it = S // (P * _GDN_UNROLL)

    def fused(it, states):
        st = [states]

        def between(k):
            st[0] = step((it - 1) * per_it + k, st[0])

        prep(it, between)
        return st[0]

    prep(0)
    states = lax.fori_loop(1, n_it, fused, tuple(jnp.zeros((GDN_DK, LANES), F32) for _ in range(_GDN_HB)))
    for k in range(per_it):
        states = step((n_it - 1) * per_it + k, states)


def _gdn(qkvn, gact, proj, gnorm, B, S):
    T = B * S
    nc = S // GDN_CHUNK
    hb = _GDN_HB
    ng = GDN_HEADS // hb
    w = hb * LANES
    blk = lambda off: pl.BlockSpec((S, w), lambda b, h: (b, off + h))
    return pl.pallas_call(
        functools.partial(_gdn_kernel, S=S),
        grid=(B, ng),
        in_specs=[
            blk(0), blk(ng), blk(2 * ng),
            pl.BlockSpec((S, LANES), lambda b, h: (b, 0)),
            blk(COL_GZ // w),
            pl.BlockSpec((1, LANES), lambda b, h: (0, 0)),
        ],
        out_specs=pl.BlockSpec((S, w), lambda b, h: (b, h)),
        out_shape=jax.ShapeDtypeStruct((T, GDN_HEADS * LANES), BF16),
        scratch_shapes=[
            pltpu.VMEM((hb, nc, GDN_DK, LANES), BF16),
            pltpu.VMEM((hb, nc, GDN_DK, LANES), F32),
            pltpu.VMEM((hb, S, LANES), BF16),
            pltpu.VMEM((hb, S, LANES), F32),
            pltpu.VMEM((hb, nc, 8, LANES), F32),
        ],
        compiler_params=_cparams(("parallel", "parallel"), VMEM_LIMIT_GDN),
        name="gdn",
    )(qkvn, qkvn, qkvn, gact, proj, gnorm)


AUG0 = FOX_HD
_NPAIR = FOX_HEADS * FOX_HD // LANES


def _fox_aug_selectors():
    import numpy as np
    sel = np.zeros((2 * _NPAIR, 3 * LANES, 2 * LANES), np.float32)
    for j in range(2 * _NPAIR):
        is_q = j < _NPAIR
        for slot in range(2):
            head = 2 * (j % _NPAIR) + slot
            c_lane, one_lane, sign = (AUG0, AUG0 + 3, 1.0) if is_q else (AUG0 + 3, AUG0, -1.0)
            for piece in range(3):
                sel[j, piece * LANES + LANE_F + head, slot * LANES + c_lane + piece] = sign
                sel[j, LANE_ONE, slot * LANES + one_lane + piece] = 1.0
    return jnp.asarray(sel, BF16)


def _fox_prep_kernel(x_ref, ga_ref, g_ref, sel_ref, o_ref):
    is_q = pl.program_id(1) < _NPAIR
    x = x_ref[...].astype(F32)
    row = lax.broadcasted_iota(jnp.int32, (LANES, LANES), 0)
    col = lax.broadcasted_iota(jnp.int32, (LANES, LANES), 1)
    grp = jnp.where(row // FOX_HD == col // FOX_HD, 1.0, 0.0).astype(BF16)
    xx = x * x
    hi = xx.astype(BF16)
    lo = (xx - hi.astype(F32)).astype(BF16)
    ms = _dot(jnp.concatenate([hi, lo], axis=1), jnp.concatenate([grp, grp], axis=0)) * (1.0 / FOX_HD)
    scale = jnp.where(is_q, LOG2E * FOX_HD ** -0.5, 1.0).astype(F32)
    xn = x * lax.rsqrt(ms + EPS) * (g_ref[0] * scale)
    xr = pltpu.roll(xn, FOX_HD, axis=1)
    aug = _dot(jnp.concatenate(_split3(ga_ref[...]), axis=1), sel_ref[0])
    lane = lax.broadcasted_iota(jnp.int32, x.shape, 1)
    o_ref[:, :LANES] = jnp.where(lane < FOX_HD, xn, aug[:, :LANES]).astype(o_ref.dtype)
    o_ref[:, LANES:] = jnp.where(lane < FOX_HD, xr, aug[:, LANES:]).astype(o_ref.dtype)


def _fox_prep(proj, gact, gains, T):
    tm = min(1024, T)
    ncol = 2 * _NPAIR
    return pl.pallas_call(
        _fox_prep_kernel,
        grid=(T // tm, ncol),
        in_specs=[
            pl.BlockSpec((tm, LANES), lambda i, j: (i, COL_FQKV // LANES + j)),
            pl.BlockSpec((tm, LANES), lambda i, j: (i, 0)),
            pl.BlockSpec((1, 1, LANES), lambda i, j: (j, 0, 0)),
            pl.BlockSpec((1, 3 * LANES, 2 * LANES), lambda i, j: (j, 0, 0)),
        ],
        out_specs=pl.BlockSpec((tm, 2 * LANES), lambda i, j: (i, j)),
        out_shape=jax.ShapeDtypeStruct((T, ncol * 2 * LANES), BF16),
        compiler_params=_cparams(("parallel", "arbitrary")),
        name="fox_prep",
    )(proj, gact, gains, _fox_aug_selectors())


def _vt_kernel(v_ref, o_ref):
    o_ref[...] = jnp.transpose(v_ref[...].astype(F32)).astype(o_ref.dtype)


def _fox_vt(proj, T):
    tm = min(2048, T)
    nhp = FOX_HEADS // 2
    return pl.pallas_call(
        _vt_kernel,
        grid=(T // tm, nhp),
        in_specs=[pl.BlockSpec((tm, LANES), lambda i, h: (i, COL_FQKV // LANES + 2 * nhp + h))],
        out_specs=pl.BlockSpec((LANES, tm), lambda i, h: (h, i)),
        out_shape=jax.ShapeDtypeStruct((nhp * LANES, T), BF16),
        compiler_params=_cparams(("parallel", "parallel")),
        name="fox_vt",
    )(proj)


def _fox_kernel(q_ref, k_ref, vt_ref, o_ref, *, tq):
    i = pl.program_id(2)
    hs = range(2)
    q = [q_ref[:, hh * LANES:(hh + 1) * LANES] for hh in hs]
    th = tq // 2
    keyi = lax.broadcasted_iota(jnp.int32, (th, th), 0)
    qryi = lax.broadcasted_iota(jnp.int32, (th, th), 1)
    tri = keyi <= qryi

    def kv_step(r0, nk, q0, carry, masked):
        ms, ls, accs = carry
        vt = [vt_ref[hh * FOX_HD:(hh + 1) * FOX_HD, pl.ds(r0, nk)] for hh in hs]
        s = [_dot(k_ref[pl.ds(r0, nk), hh * LANES:(hh + 1) * LANES], q[hh][q0:, :], _NT) for hh in hs]
        if masked:
            s = [jnp.where(tri, x, -jnp.inf) if x.shape[1] == th else
                 jnp.concatenate([jnp.where(tri, x[:, :th], -jnp.inf), x[:, th:]], axis=1) for x in s]
        m_old = [ms[hh][:, q0:] for hh in hs]
        m_new = [jnp.maximum(m_old[hh], jnp.max(s[hh], axis=0, keepdims=True)) for hh in hs]
        pr = [jnp.exp2(s[hh] - m_new[hh]) for hh in hs]
        alpha = [jnp.exp2(m_old[hh] - m_new[hh]) for hh in hs]
        pv = [_dot(vt[hh], pr[hh].astype(BF16)) for hh in hs]
        l_new = [alpha[hh] * ls[hh][:, q0:] + jnp.sum(pr[hh], axis=0, keepdims=True) for hh in hs]
        a_new = [alpha[hh] * accs[hh][:, q0:] + pv[hh] for hh in hs]
        keep = lambda old, new: [jnp.concatenate([old[hh][:, :q0], new[hh]], axis=1) if q0 else new[hh] for hh in hs]
        return keep(ms, m_new), keep(ls, l_new), keep(accs, a_new)

    init = ([jnp.full((1, tq), -jnp.inf, F32) for _ in hs], [jnp.zeros((1, tq), F32) for _ in hs],
            [jnp.zeros((FOX_HD, tq), F32) for _ in hs])
    carry = lax.fori_loop(0, i, lambda jb, c: kv_step(pl.multiple_of(jb * tq, tq), tq, 0, c, False), init)
    d0 = pl.multiple_of(i * tq, tq)
    carry = kv_step(d0, th, 0, carry, True)
    _, ls, accs = kv_step(pl.multiple_of(d0 + th, th), th, th, carry, True)
    ot = jnp.concatenate([accs[hh] / ls[hh] for hh in hs], axis=0)
    o_ref[...] = jnp.transpose(ot).astype(o_ref.dtype)


def _fox(qka, vt, B, S):
    T = B * S
    tq = min(1024, S)
    nq = S // tq
    nhp = FOX_HEADS // 2
    return pl.pallas_call(
        functools.partial(_fox_kernel, tq=tq),
        grid=(B, nhp, nq),
        in_specs=[
            pl.BlockSpec((tq, 2 * LANES), lambda b, h, i: (b * nq + i, h)),
            pl.BlockSpec((S, 2 * LANES), lambda b, h, i: (b, nhp + h)),
            pl.BlockSpec((LANES, S), lambda b, h, i: (h, b)),
        ],
        out_specs=pl.BlockSpec((tq, LANES), lambda b, h, i: (b * nq + i, h)),
        out_shape=jax.ShapeDtypeStruct((T, nhp * LANES), BF16),
        compiler_params=_cparams(("parallel", "parallel", "arbitrary")),
        name="fox",
    )(qka, qka, vt)


HALF = D_MODEL // 2


def _rms(x, gain):
    return x * lax.rsqrt(jnp.mean(x * x, axis=-1, keepdims=True) + EPS) * gain


def _pack_halves(x):
    lo = lax.bitcast_convert_type(x[:, :HALF].astype(BF16).astype(F32), jnp.uint32)
    hi = lax.bitcast_convert_type(x[:, HALF:].astype(BF16).astype(F32), jnp.uint32)
    return lax.shift_right_logical(lo, jnp.uint32(16)) | (hi & jnp.uint32(0xFFFF0000))


def _unpack_halves(w):
    lo = lax.bitcast_convert_type(lax.shift_left(w, jnp.uint32(16)), F32)
    hi = lax.bitcast_convert_type(w & jnp.uint32(0xFFFF0000), F32)
    return lo, hi


def _mix_kernel(oa_ref, of_ref, ma_ref, mf_ref, x_ref, wa_ref, wf_ref, wo_ref, g_ref, h_ref, hp_ref):
    ya = _dot(oa_ref[...], wa_ref[...])
    yf = _dot(of_ref[...], wf_ref[...])
    m = _sigmoid(ma_ref[...].astype(F32)) * ya + _sigmoid(mf_ref[...].astype(F32)) * yf
    h = x_ref[...] + _dot(m.astype(BF16), wo_ref[...])
    h_ref[...] = h
    hp_ref[...] = _pack_halves(_rms(h, g_ref[...]))


def _mix(o_a, o_f, proj, x2, wa, wf, wo, gain):
    T = x2.shape[0]
    tm = min(512, T)
    row = lambda c: pl.BlockSpec((tm, D_MODEL), lambda i: (i, c))
    full = pl.BlockSpec((D_MODEL, D_MODEL), lambda i: (0, 0))
    return pl.pallas_call(
        _mix_kernel,
        grid=(T // tm,),
        in_specs=[row(0), row(0), row(COL_MERGE // D_MODEL), row(COL_MERGE // D_MODEL + 1), row(0),
                  full, full, full, pl.BlockSpec((1, D_MODEL), lambda i: (0, 0))],
        out_specs=[row(0), pl.BlockSpec((tm, HALF), lambda i: (i, 0))],
        out_shape=[jax.ShapeDtypeStruct((T, D_MODEL), F32), jax.ShapeDtypeStruct((T, HALF), jnp.uint32)],
        compiler_params=_cparams(("parallel",)),
        name="mix_out",
    )(o_a, o_f, proj, proj, x2, wa, wf, wo, gain)


def _router_kernel(h_ref, g_ref, wr_ref, bias_ref, gates_ref, sel_ref, cnt_ref):
    @pl.when(pl.program_id(0) == 0)
    def _():
        cnt_ref[...] = jnp.zeros_like(cnt_ref)

    hn = _rms(h_ref[...], g_ref[...])
    hh = hn.astype(BF16)
    hl = (hn - hh.astype(F32)).astype(BF16)
    logits = _dot(wr_ref[0], hh, _NT) + _dot(wr_ref[0], hl, _NT) + _dot(wr_ref[1], hh, _NT)
    scores = _sigmoid(logits)
    work = scores + bias_ref[...]
    eidx = lax.broadcasted_iota(jnp.int32, work.shape, 0)
    sel = jnp.zeros(work.shape, F32)
    for _ in range(TOP_K):
        mx = jnp.max(work, axis=0, keepdims=True)
        first = jnp.min(jnp.where(work == mx, eidx, N_EXPERTS), axis=0, keepdims=True)
        onehot = eidx == first
        sel = jnp.where(onehot, 1.0, sel)
        work = jnp.where(onehot, -jnp.inf, work)
    s = jnp.where(sel > 0.0, scores, 0.0)
    gates_ref[...] = s / jnp.sum(s, axis=0, keepdims=True) * ROUTED_SCALE
    sel_ref[...] = sel
    tot = jnp.zeros((N_EXPERTS, 1), F32)
    for t0 in range(0, sel.shape[1], MOE_TT):
        n = jnp.sum(sel[:, t0:t0 + MOE_TT], axis=1, keepdims=True)
        tot = tot + jnp.floor((n + (SUB - 1)) * (1.0 / SUB)) * SUB
    cnt_ref[...] += jnp.broadcast_to(tot, cnt_ref.shape)


def _router(h1, gain, wr2, bias_col):
    T = h1.shape[0]
    tm = min(2 * MOE_TT, T)
    return pl.pallas_call(
        _router_kernel,
        grid=(T // tm,),
        in_specs=[
            pl.BlockSpec((tm, D_MODEL), lambda i: (i, 0)),
            pl.BlockSpec((1, D_MODEL), lambda i: (0, 0)),
            pl.BlockSpec((2, N_EXPERTS, D_MODEL), lambda i: (0, 0, 0)),
            pl.BlockSpec((N_EXPERTS, 1), lambda i: (0, 0)),
        ],
        out_specs=[
            pl.BlockSpec((N_EXPERTS, tm), lambda i: (0, i)),
            pl.BlockSpec((N_EXPERTS, tm), lambda i: (0, i)),
            pl.BlockSpec((N_EXPERTS, LANES), lambda i: (0, 0)),
        ],
        out_shape=[
            jax.ShapeDtypeStruct((N_EXPERTS, T), F32),
            jax.ShapeDtypeStruct((N_EXPERTS, T), F32),
            jax.ShapeDtypeStruct((N_EXPERTS, LANES), F32),
        ],
        compiler_params=_cparams(("arbitrary",)),
        name="router",
    )(h1, gain, wr2, bias_col)


MOE_TT = 256
SUB = 8
MOE_R = MOE_TT * TOP_K + N_EXPERTS * SUB


def _pos_kernel(sel_ref, gates_ref, cnt_ref, lpos_ref, w_ref, tab_ref, base_s, carry_s, *, tt, tme):
    r = lax.broadcasted_iota(jnp.int32, (N_EXPERTS, N_EXPERTS), 0)
    c = lax.broadcasted_iota(jnp.int32, (N_EXPERTS, N_EXPERTS), 1)
    below = jnp.where(r > c, 1.0, 0.0).astype(BF16)

    @pl.when(pl.program_id(0) == 0)
    def _():
        padded = jnp.floor((cnt_ref[...] + (tme - 1)) * (1.0 / tme)) * tme
        base_s[...] = _dot_lx(below, padded)
        carry_s[...] = jnp.zeros_like(carry_s)

    sel = sel_ref[...]
    gates = gates_ref[...]
    rr = lax.broadcasted_iota(jnp.int32, (tt, tt), 0)
    cc = lax.broadcasted_iota(jnp.int32, (tt, tt), 1)
    cs = _dot(sel.astype(BF16), jnp.where(rr <= cc, 1.0, 0.0).astype(BF16))
    n = jnp.sum(sel, axis=1, keepdims=True)
    npad = jnp.broadcast_to(jnp.floor((n + (SUB - 1)) * (1.0 / SUB)) * SUB, (N_EXPERTS, LANES))
    loff = _dot_lx(below, npad)
    gstart = base_s[...] + carry_s[...]
    carry_s[...] += npad

    wide = lambda v: jnp.concatenate([v] * (TAB_W // LANES), axis=1)
    incl = jnp.where(r >= c, 1.0, 0.0).astype(BF16)
    nfull = jnp.floor(npad * (1.0 / PIECE))
    has8 = npad * (1.0 / SUB) - 2.0 * nfull
    cend = _dot_lx(incl, nfull)
    hend = _dot_lx(incl, has8)
    jl = lax.broadcasted_iota(jnp.int32, (N_EXPERTS, TAB_W), 1).astype(F32)
    ei = lax.broadcasted_iota(jnp.int32, (N_EXPERTS, TAB_W), 0).astype(F32)

    def owner(ends):
        e = jnp.sum(jnp.where(wide(ends) <= jl, 1.0, 0.0), axis=0, keepdims=True)
        return ei == jnp.minimum(e, N_EXPERTS - 1.0)

    take = lambda oh, v: jnp.sum(jnp.where(oh, wide(v), 0.0), axis=0, keepdims=True)
    oh16 = owner(cend)
    oh8 = owner(hend)
    off16 = PIECE * (jl[0:1, :] - take(oh16, cend - nfull))
    off8 = PIECE * take(oh8, nfull)
    last = N_EXPERTS - 1
    lane = lax.broadcasted_iota(jnp.int32, (1, TAB_W), 1)
    misc = jnp.where(lane == 0, wide(cend[last:, :]),
                     jnp.where(lane == 1, wide(hend[last:, :]), wide(loff[last:, :] + npad[last:, :])))
    tab_ref[0] = jnp.concatenate([take(oh16, gstart) + off16, take(oh16, loff) + off16,
                                  take(oh8, gstart) + off8, take(oh8, loff) + off8, misc,
                                  jnp.zeros((3, TAB_W), F32)], axis=0)
    lposd = cs - sel + jnp.concatenate([loff] * (tt // LANES), axis=1)
    eidx = lax.broadcasted_iota(jnp.int32, sel.shape, 0)
    rem = sel
    ps, ws = [], []
    for _ in range(TOP_K):
        first = jnp.min(jnp.where(rem > 0.0, eidx, N_EXPERTS), axis=0, keepdims=True)
        onehot = eidx == first
        ps.append(jnp.sum(jnp.where(onehot, lposd, 0.0), axis=0, keepdims=True))
        ws.append(jnp.sum(jnp.where(onehot, gates, 0.0), axis=0, keepdims=True))
        rem = jnp.where(onehot, 0.0, rem)
    lpos_ref[...] = jnp.concatenate(ps, axis=0).astype(jnp.int32)
    w_ref[...] = jnp.concatenate(ws, axis=0)


def _positions(sel, gates, cnt, tme):
    T = sel.shape[1]
    tt = MOE_TT
    blk = pl.BlockSpec((N_EXPERTS, tt), lambda i: (0, i))
    return pl.pallas_call(
        functools.partial(_pos_kernel, tt=tt, tme=tme),
        grid=(T // tt,),
        in_specs=[blk, blk, pl.BlockSpec((N_EXPERTS, LANES), lambda i: (0, 0))],
        out_specs=[pl.BlockSpec((TOP_K, tt), lambda i: (0, i)), pl.BlockSpec((TOP_K, tt), lambda i: (0, i)),
                   pl.BlockSpec((1, TAB_ROWS, TAB_W), lambda i: (i, 0, 0))],
        out_shape=[jax.ShapeDtypeStruct((TOP_K, T), jnp.int32), jax.ShapeDtypeStruct((TOP_K, T), F32),
                   jax.ShapeDtypeStruct((T // tt, TAB_ROWS, TAB_W), F32)],
        scratch_shapes=[pltpu.VMEM((N_EXPERTS, LANES), F32), pltpu.VMEM((N_EXPERTS, LANES), F32)],
        compiler_params=_cparams(("arbitrary",)),
        name="moe_positions",
    )(sel, gates, cnt)


PIECE = 2 * SUB
TAB_W = 2 * LANES
TAB_ROWS = 8
TAB_G16, TAB_L16, TAB_G8, TAB_L8 = 0, TAB_W, 2 * TAB_W, 3 * TAB_W
TAB_N16, TAB_N8, TAB_USED = 4 * TAB_W, 4 * TAB_W + 1, 4 * TAB_W + 2
TAB_LEN = TAB_ROWS * TAB_W


def _segment_copies(tab_ref, src_of, dst_of, sem):
    def pieces(g0, l0, rows):
        def body(j, carry):
            g = pl.multiple_of(tab_ref[0, 0, g0 + j], SUB)
            l = pl.multiple_of(tab_ref[0, 0, l0 + j], SUB)
            pltpu.make_async_copy(src_of(g, l, rows), dst_of(g, l, rows), sem).start()
            return carry
        return body

    lax.fori_loop(0, tab_ref[0, 0, TAB_N16], pieces(TAB_G16, TAB_L16, PIECE), 0)
    lax.fori_loop(0, tab_ref[0, 0, TAB_N8], pieces(TAB_G8, TAB_L8, SUB), 0)


def _wait_segment_copies(tab_ref, src_of, dst_of, sem):
    used = tab_ref[0, 0, TAB_USED]

    def unit(rows):
        def body(i, carry):
            pltpu.make_async_copy(src_of(0, 0, rows), dst_of(0, 0, rows), sem).wait()
            return carry
        return body

    lax.fori_loop(0, lax.shift_right_logical(used, 7), unit(LANES), 0)
    lax.fori_loop(0, lax.shift_right_logical(used, 3) & (LANES // SUB - 1), unit(SUB), 0)


_ROW_CHUNK = MOE_R // 4


def _pack2(lo, hi):
    lo = lax.bitcast_convert_type(lo.astype(BF16).astype(F32), jnp.uint32)
    hi = lax.bitcast_convert_type(hi.astype(BF16).astype(F32), jnp.uint32)
    return lax.shift_right_logical(lo, jnp.uint32(16)) | (hi & jnp.uint32(0xFFFF0000))


def _dispatch_kernel(meta_ref, seg_ref, prev_ref, lpos_ref, hp_ref, xs_ref, zero_s, sort_s, sem, zsem, *,
                     tt, tme, n_tiles):
    i = pl.program_id(0)
    slot = i % 2

    @pl.when(i == 0)
    def _():
        zero_s[...] = jnp.zeros_like(zero_s)
        for e in range(N_EXPERTS):
            pltpu.make_async_copy(zero_s, xs_ref.at[pl.ds(pl.multiple_of(meta_ref[e], SUB), tme)], zsem).start()
        for e in range(N_EXPERTS):
            pltpu.make_async_copy(zero_s, xs_ref.at[pl.ds(pl.multiple_of(meta_ref[e], SUB), tme)], zsem).wait()

        def zero_tile(t, carry):
            cp = pltpu.make_async_copy(zero_s, xs_ref.at[pl.ds(pl.multiple_of(t * tme, tme), tme)], zsem)
            cp.start()
            cp.wait()
            return carry

        lax.fori_loop(meta_ref[N_EXPERTS], n_tiles, zero_tile, 0)

    lo, hi = _unpack_halves(hp_ref[...])
    lo = lo.astype(BF16)
    hi = hi.astype(BF16)
    lp = lpos_ref[...].astype(jnp.int16)
    one = jnp.ones((1, 1), BF16)
    for rc in range(MOE_R // _ROW_CHUNK):
        rowi = (lax.broadcasted_iota(jnp.int32, (_ROW_CHUNK, tt), 0) + rc * _ROW_CHUNK).astype(jnp.int16)
        pm = jnp.zeros((_ROW_CHUNK, tt), BF16)
        for k in range(TOP_K):
            pm = jnp.where(rowi == lp[k:k + 1, :], one, pm)
        sort_s[slot, rc * _ROW_CHUNK:(rc + 1) * _ROW_CHUNK, :] = _pack2(_dot(pm, lo), _dot(pm, hi))

    src_of = lambda s: (lambda g, l, rows: sort_s.at[s, pl.ds(l, rows)])
    dst_of = lambda g, l, rows: xs_ref.at[pl.ds(g, rows)]

    @pl.when(i > 0)
    def _():
        _wait_segment_copies(prev_ref, src_of(1 - slot), dst_of, sem.at[1 - slot])

    _segment_copies(seg_ref, src_of(slot), dst_of, sem.at[slot])

    @pl.when(i == pl.num_programs(0) - 1)
    def _():
        _wait_segment_copies(seg_ref, src_of(slot), dst_of, sem.at[slot])


def _dispatch(meta, seg, lpos, hp, n_tiles, tme):
    T = hp.shape[0]
    tt = MOE_TT
    return pl.pallas_call(
        functools.partial(_dispatch_kernel, tt=tt, tme=tme, n_tiles=n_tiles),
        grid_spec=pltpu.PrefetchScalarGridSpec(
            num_scalar_prefetch=1,
            grid=(T // tt,),
            in_specs=[
                pl.BlockSpec((1, 1, TAB_LEN), lambda i, f: (i, 0, 0), memory_space=pltpu.SMEM),
                pl.BlockSpec((1, 1, TAB_LEN), lambda i, f: (jnp.maximum(i - 1, 0), 0, 0), memory_space=pltpu.SMEM),
                pl.BlockSpec((TOP_K, tt), lambda i, f: (0, i)),
                pl.BlockSpec((tt, HALF), lambda i, f: (i, 0)),
            ],
            out_specs=pl.BlockSpec(memory_space=pl.ANY),
            scratch_shapes=[pltpu.VMEM((tme, HALF), jnp.uint32), pltpu.VMEM((2, MOE_R, HALF), jnp.uint32),
                            pltpu.SemaphoreType.DMA((2,)), pltpu.SemaphoreType.DMA],
        ),
        out_shape=jax.ShapeDtypeStruct((n_tiles * tme, HALF), jnp.uint32),
        compiler_params=_cparams(("arbitrary",)),
        name="moe_dispatch",
    )(meta, seg, seg, lpos, hp)


def _expert_kernel(te_ref, nu_ref, xs_ref, wgu_ref, wd_ref, ys_ref):
    @pl.when(pl.program_id(0) < nu_ref[0])
    def _():
        lo, hi = _unpack_halves(xs_ref[...])
        gu = _dot(lo.astype(BF16), wgu_ref[0, :HALF, :]) + _dot(hi.astype(BF16), wgu_ref[0, HALF:, :])
        hid = _silu(gu[:, :D_EXPERT]) * gu[:, D_EXPERT:]
        ys_ref[...] = _pack_halves(_dot(hid.astype(BF16), wd_ref[0]))

    @pl.when(pl.program_id(0) >= nu_ref[0])
    def _():
        ys_ref[...] = jnp.zeros_like(ys_ref)


def _experts(tile_e, n_used, xs, wgu, wd, n_tiles, tme):
    clamp = lambda i, te, nu: (jnp.minimum(i, nu[0] - 1), 0)
    return pl.pallas_call(
        _expert_kernel,
        grid_spec=pltpu.PrefetchScalarGridSpec(
            num_scalar_prefetch=2,
            grid=(n_tiles,),
            in_specs=[
                pl.BlockSpec((tme, HALF), clamp),
                pl.BlockSpec((1, D_MODEL, 2 * D_EXPERT), lambda i, te, nu: (te[i], 0, 0)),
                pl.BlockSpec((1, D_EXPERT, D_MODEL), lambda i, te, nu: (te[i], 0, 0)),
            ],
            out_specs=pl.BlockSpec((tme, HALF), lambda i, te, nu: (i, 0)),
        ),
        out_shape=jax.ShapeDtypeStruct((n_tiles * tme, HALF), jnp.uint32),
        compiler_params=_cparams(("arbitrary",)),
        name="moe_experts",
    )(tile_e, n_used, xs, wgu, wd)


def _tail_kernel(seg_ref, ys_ref, lpos_ref, w_ref, h_ref, p_ref, gf_ref, wsgu_ref, wsd_ref, wple_ref, gple_ref,
                 gpg_ref, wpg_ref, o_ref, buf, sem, *, tt):
    src_of = lambda g, l, rows: ys_ref.at[pl.ds(g, rows)]
    dst_of = lambda g, l, rows: buf.at[pl.ds(l, rows)]
    _segment_copies(seg_ref, src_of, dst_of, sem)

    used = seg_ref[0, 0, TAB_USED]

    def clear(j, carry):
        buf[pl.ds(pl.multiple_of(j * SUB, SUB), SUB), :] = jnp.zeros((SUB, HALF), jnp.uint32)
        return carry

    lax.fori_loop(lax.shift_right_logical(used, 3), MOE_R // SUB, clear, 0)

    h = h_ref[...]
    hn = _rms(h, gf_ref[...]).astype(BF16)
    sgu = _dot(hn, wsgu_ref[...])
    shared = _dot((_silu(sgu[:, :D_SHARED]) * sgu[:, D_SHARED:]).astype(BF16), wsd_ref[...])
    e = _rms(_dot(p_ref[...].astype(BF16), wple_ref[...]), gple_ref[...])
    _wait_segment_copies(seg_ref, src_of, dst_of, sem)

    lp = lpos_ref[...].astype(jnp.int16)
    wt = w_ref[...].astype(BF16)
    acc_lo = jnp.zeros((tt, HALF), F32)
    acc_hi = jnp.zeros((tt, HALF), F32)
    for rc in range(MOE_R // _ROW_CHUNK):
        coli = (lax.broadcasted_iota(jnp.int32, (tt, _ROW_CHUNK), 1) + rc * _ROW_CHUNK).astype(jnp.int16)
        g = jnp.zeros((tt, _ROW_CHUNK), BF16)
        for k in range(TOP_K):
            g = jnp.where(coli == lp[:, k:k + 1], wt[:, k:k + 1], g)
        lo, hi = _unpack_halves(buf[rc * _ROW_CHUNK:(rc + 1) * _ROW_CHUNK, :])
        acc_lo = acc_lo + _dot(g, lo.astype(BF16))
        acc_hi = acc_hi + _dot(g, hi.astype(BF16))
    h2 = h + jnp.concatenate([acc_lo, acc_hi], axis=1) + shared
    gte = _sigmoid(_dot(_rms(h2, gpg_ref[...]).astype(BF16), wpg_ref[...]))
    o_ref[...] = h2 + gte * e


def _tail(seg, ys, lpos_tok, w_tok, h1, p2, gf, wsgu, wsd, wple, gple, gpg, wpg):
    T = h1.shape[0]
    tt = MOE_TT
    row = lambda n: pl.BlockSpec((tt, n), lambda i: (i, 0))
    full = lambda a: pl.BlockSpec(a.shape, lambda i: (0,) * a.ndim)
    return pl.pallas_call(
        functools.partial(_tail_kernel, tt=tt),
        grid=(T // tt,),
        in_specs=[
            pl.BlockSpec((1, 1, TAB_LEN), lambda i: (i, 0, 0), memory_space=pltpu.SMEM),
            pl.BlockSpec(memory_space=pl.ANY),
            row(TOP_K), row(TOP_K), row(D_MODEL), row(PLE_DIM),
            full(gf), full(wsgu), full(wsd), full(wple), full(gple), full(gpg), full(wpg),
        ],
        out_specs=row(D_MODEL),
        out_shape=jax.ShapeDtypeStruct((T, D_MODEL), F32),
        scratch_shapes=[pltpu.VMEM((MOE_R, HALF), jnp.uint32), pltpu.SemaphoreType.DMA],
        compiler_params=_cparams(("arbitrary",)),
        name="moe_combine_ple",
    )(seg, ys, lpos_tok, w_tok, h1, p2, gf, wsgu, wsd, wple, gple, gpg, wpg)


def _stages(x, p, norm_mix, w_in, conv_w, a_log, dt_bias, gdn_norm, fox_f_bias, q_norm, k_norm,
            w_branch_gdn, w_branch_fox, w_out, norm_ffn, w_router, router_bias,
            we_gate, we_up, we_down, ws_gate, ws_up, ws_down, w_ple, ple_norm, ple_gate_norm, w_ple_gate):
    B, S, _ = x.shape
    T = B * S
    st = {}
    wi = w_in[0]
    o0 = 3 * GDN_HEADS * GDN_DK
    o1 = o0 + GDN_HEADS * GDN_DK
    o2 = o1 + GDN_HEADS
    o3 = o2 + GDN_HEADS
    o4 = o3 + 3 * FOX_HEADS * FOX_HD
    o5 = o4 + FOX_HEADS
    w_big = jnp.concatenate([wi[:, :o1], wi[:, o3:o4], wi[:, o5:]], axis=1).astype(BF16)
    w_small = jnp.concatenate([wi[:, o1:o3], wi[:, o4:o5],
                               jnp.zeros((D_MODEL, LANES - 2 * GDN_HEADS - FOX_HEADS), F32)], axis=1).astype(BF16)
    proj, small = _in_proj(x.reshape(T, D_MODEL), norm_mix[0].reshape(1, D_MODEL), w_big, w_small)
    st["proj_big"] = proj
    st["small"] = small

    pad = lambda v, off: jnp.zeros((1, LANES), F32).at[0, off:off + v.shape[0]].set(v)
    alog_row = pad(a_log[0], LANE_A)
    bias_row = pad(dt_bias[0], LANE_A) + pad(fox_f_bias[0], LANE_F)
    gact = _gates(small, alog_row, bias_row, B, S)
    st["gact"] = gact

    qkvn = _gdn_prep(proj, conv_w[0], B, S)
    st["qkvn"] = qkvn
    o_a = _gdn(qkvn, gact, proj, gdn_norm[0].reshape(1, LANES), B, S)
    st["o_a"] = o_a

    nrep = FOX_HEADS * FOX_HD // LANES
    gains = jnp.concatenate([jnp.tile(q_norm[0], (nrep, LANES // FOX_HD)),
                             jnp.tile(k_norm[0], (nrep, LANES // FOX_HD))], axis=0).reshape(2 * nrep, 1, LANES)
    qka = _fox_prep(proj, gact, gains, T)
    o_f = _fox(qka, _fox_vt(proj, T), B, S)
    st["o_f"] = o_f

    x2 = x.reshape(T, D_MODEL)
    gf = norm_ffn[0].reshape(1, D_MODEL)
    h1, hp = _mix(o_a, o_f, proj, x2, w_branch_gdn[0].astype(BF16), w_branch_fox[0].astype(BF16),
                  w_out[0].astype(BF16), gf)
    st["h1"] = h1

    wrt = w_router[0].T
    wr_hi = wrt.astype(BF16)
    wr2 = jnp.stack([wr_hi, (wrt - wr_hi.astype(F32)).astype(BF16)])
    gates, sel, cnt = _router(h1, gf, wr2, router_bias[0].reshape(N_EXPERTS, 1))
    st["gates_t"] = gates

    tme = 512 if T * TOP_K // N_EXPERTS >= 2048 else 64
    lpos, wts, segf = _positions(sel, gates, cnt, tme)
    nt_tok = T // MOE_TT
    seg = segf.astype(jnp.int32).reshape(nt_tok, 1, TAB_LEN)
    cnt1 = cnt[:, 0].astype(jnp.int32)
    padded = (cnt1 + (tme - 1)) // tme * tme
    ends = jnp.cumsum(padded)
    n_tiles = (T * TOP_K + (SUB - 1) * N_EXPERTS * nt_tok) // tme + N_EXPERTS
    tile_start = jnp.arange(n_tiles, dtype=jnp.int32) * tme
    tile_e = jnp.minimum(jnp.sum((ends[None, :] <= tile_start[:, None]).astype(jnp.int32), axis=1), N_EXPERTS - 1)
    n_used = (ends[-1:] // tme).astype(jnp.int32)
    fill = (ends - padded + cnt1).astype(jnp.int32)
    xs = _dispatch(jnp.concatenate([fill, n_used]), seg, lpos, hp, n_tiles, tme)
    wgu = jnp.concatenate([we_gate[0], we_up[0]], axis=2).astype(BF16)
    ys = _experts(tile_e, n_used, xs, wgu, we_down[0].astype(BF16), n_tiles, tme)

    row = lambda v: v.reshape(1, D_MODEL)
    wsgu = jnp.concatenate([ws_gate[0], ws_up[0]], axis=1).astype(BF16)
    out = _tail(seg, ys, lpos.T, wts.T, h1, p[0].reshape(T, PLE_DIM), gf, wsgu, ws_down[0].astype(BF16),
                w_ple[0].astype(BF16), row(ple_norm[0]), row(ple_gate_norm[0]), w_ple_gate[0].astype(BF16))
    st["out"] = out.reshape(B, S, D_MODEL)
    return st


def kernel(x, p, norm_mix, w_in, conv_w, a_log, dt_bias, gdn_norm, fox_f_bias, q_norm, k_norm, w_branch_gdn, w_branch_fox, w_out, norm_ffn, w_router, router_bias, we_gate, we_up, we_down, ws_gate, ws_up, ws_down, w_ple, ple_norm, ple_gate_norm, w_ple_gate):
    return _stages(x, p, norm_mix, w_in, conv_w, a_log, dt_bias, gdn_norm, fox_f_bias, q_norm, k_norm,
                   w_branch_gdn, w_branch_fox, w_out, norm_ffn, w_router, router_bias,
                   we_gate, we_up, we_down, ws_gate, ws_up, ws_down, w_ple, ple_norm, ple_gate_norm,
                   w_ple_gate)["out"]
```

```python
import functools

import jax
import jax.numpy as jnp
from jax import lax
from jax.experimental import pallas as pl
from jax.experimental.pallas import tpu as pltpu

F32 = jnp.float32
BF16 = jnp.bfloat16

D_MODEL = 1024
PLE_DIM = 256
EPS = 1e-6
GDN_HEADS = 8
GDN_DK = 128
GDN_CONV = 4
GDN_CHUNK = 64
FOX_HEADS = 16
FOX_HD = 64
N_EXPERTS = 64
TOP_K = 8
D_EXPERT = 256
D_SHARED = 256
ROUTED_SCALE = 2.5

LANES = 128
COL_GQKV = 0
COL_GZ = 3072
COL_FQKV = 4096
COL_MERGE = 7168
N_BIG = 9216
LANE_A = 0
LANE_B = 8
LANE_F = 16
LANE_ONE = 127
LOG2E = 1.4426950408889634

VMEM_LIMIT = 48 * 1024 * 1024
VMEM_LIMIT_GDN = 56 * 1024 * 1024


def _cparams(sem, vmem=VMEM_LIMIT):
    return pltpu.CompilerParams(dimension_semantics=sem, vmem_limit_bytes=vmem)


def _split3(x):
    h = x.astype(BF16)
    r = x - h.astype(F32)
    m = r.astype(BF16)
    l = (r - m.astype(F32)).astype(BF16)
    return h, m, l


def _dot(a, b, dims=(((1,), (0,)), ((), ()))):
    return lax.dot_general(a, b, dims, preferred_element_type=F32)


_NT = (((1,), (1,)), ((), ()))
_TN = (((0,), (0,)), ((), ()))


def _dot_xl(x, sel, dims=(((1,), (0,)), ((), ()))):
    h, m, l = _split3(x)
    return _dot(h, sel, dims) + _dot(m, sel, dims) + _dot(l, sel, dims)


def _dot_lx(sel, x, dims=(((1,), (0,)), ((), ()))):
    h, m, l = _split3(x)
    return _dot(sel, h, dims) + _dot(sel, m, dims) + _dot(sel, l, dims)


def _dot3(a, b):
    ah = a.astype(BF16)
    al = (a - ah.astype(F32)).astype(BF16)
    bh = b.astype(BF16)
    bl = (b - bh.astype(F32)).astype(BF16)
    return _dot(jnp.concatenate([ah, ah, al], axis=1), jnp.concatenate([bh, bl, bh], axis=0))


def _sigmoid(x):
    return 1.0 / (1.0 + jnp.exp(-x))


def _silu(x):
    return x * _sigmoid(x)


def _softplus(x):
    return jnp.maximum(x, 0.0) + jnp.log1p(jnp.exp(-jnp.abs(x)))


def _in_proj_kernel(x_ref, g_ref, wb_ref, ws_ref, ob_ref, os_ref, xn_ref):
    @pl.when(pl.program_id(1) == 0)
    def _():
        x = x_ref[...]
        ms = jnp.mean(x * x, axis=-1, keepdims=True)
        xn = (x * lax.rsqrt(ms + EPS) * g_ref[...]).astype(BF16)
        xn_ref[...] = xn
        os_ref[...] = _dot(xn, ws_ref[...])

    ob_ref[...] = _dot(xn_ref[...], wb_ref[...]).astype(ob_ref.dtype)


def _in_proj(x2, gain, w_big, w_small):
    T = x2.shape[0]
    tm = min(1024, T)
    tn = 1024
    return pl.pallas_call(
        _in_proj_kernel,
        grid=(T // tm, N_BIG // tn),
        in_specs=[
            pl.BlockSpec((tm, D_MODEL), lambda i, j: (i, 0)),
            pl.BlockSpec((1, D_MODEL), lambda i, j: (0, 0)),
            pl.BlockSpec((D_MODEL, tn), lambda i, j: (0, j)),
            pl.BlockSpec((D_MODEL, LANES), lambda i, j: (0, 0)),
        ],
        out_specs=[
            pl.BlockSpec((tm, tn), lambda i, j: (i, j)),
            pl.BlockSpec((tm, LANES), lambda i, j: (i, 0)),
        ],
        out_shape=[
            jax.ShapeDtypeStruct((T, N_BIG), BF16),
            jax.ShapeDtypeStruct((T, LANES), F32),
        ],
        scratch_shapes=[pltpu.VMEM((tm, D_MODEL), BF16)],
        compiler_params=_cparams(("parallel", "arbitrary")),
        name="in_proj",
    )(x2, gain, w_big, w_small)


def _gates_kernel(s_ref, alog_ref, bias_ref, o_ref, carry_ref, *, sb):
    @pl.when(pl.program_id(1) == 0)
    def _():
        carry_ref[...] = jnp.zeros_like(carry_ref)

    x = s_ref[...] + bias_ref[...]
    lane = lax.broadcasted_iota(jnp.int32, x.shape, 1)
    e = jnp.log1p(jnp.exp(-jnp.abs(x)))
    log_alpha = -jnp.exp(alog_ref[...]) * (jnp.maximum(x, 0.0) + e)
    beta = _sigmoid(x)
    log_f = -(jnp.maximum(-x, 0.0) + e)
    is_a = lane < LANE_B
    is_b = jnp.logical_and(lane >= LANE_B, lane < LANE_F)
    is_f = jnp.logical_and(lane >= LANE_F, lane < LANE_F + FOX_HEADS)
    val = jnp.where(is_a, log_alpha, jnp.where(is_f, log_f, 0.0))
    row = lax.broadcasted_iota(jnp.int32, (sb, sb), 0)
    col = lax.broadcasted_iota(jnp.int32, (sb, sb), 1)
    tri = row >= col
    tri_full = jnp.where(tri, 1.0, 0.0).astype(BF16)
    tri_chunk = jnp.where(jnp.logical_and(tri, row // GDN_CHUNK == col // GDN_CHUNK), 1.0, 0.0).astype(BF16)
    h, m, l = _split3(val)
    cs_full = _dot(tri_full, h) + _dot(tri_full, m) + _dot(tri_full, l) + carry_ref[0:1, :]
    cs_chunk = _dot(tri_chunk, h) + _dot(tri_chunk, m) + _dot(tri_chunk, l)
    rest = jnp.where(is_f, LOG2E * cs_full, jnp.where(lane == LANE_ONE, 1.0, 0.0))
    o_ref[...] = jnp.where(is_a, cs_chunk, jnp.where(is_b, beta, rest))
    carry_ref[...] = jnp.broadcast_to(cs_full[sb - 1:sb, :], carry_ref.shape)


def _gates(small, alog_row, bias_row, B, S):
    sb = min(256, S)
    nsb = S // sb
    return pl.pallas_call(
        functools.partial(_gates_kernel, sb=sb),
        grid=(B, nsb),
        in_specs=[
            pl.BlockSpec((sb, LANES), lambda b, j: (b * nsb + j, 0)),
            pl.BlockSpec((1, LANES), lambda b, j: (0, 0)),
            pl.BlockSpec((1, LANES), lambda b, j: (0, 0)),
        ],
        out_specs=pl.BlockSpec((sb, LANES), lambda b, j: (b * nsb + j, 0)),
        out_shape=jax.ShapeDtypeStruct(small.shape, F32),
        scratch_shapes=[pltpu.VMEM((8, LANES), F32)],
        compiler_params=_cparams(("parallel", "arbitrary")),
        name="gates",
    )(small, alog_row, bias_row)


_GP_TILES = 4


def _gdn_prep_kernel(x_ref, w_ref, o_ref, pad_ref, *, S, rb):
    j = pl.program_id(1)
    per_kind = GDN_HEADS // _GP_TILES
    is_qk = j < 2 * per_kind
    scale = jnp.where(j < per_kind, GDN_DK ** -0.5, 1.0).astype(F32)
    pad_ref[0:8, :] = jnp.zeros((8, LANES), F32)
    for c in range(_GP_TILES):
        cols = slice(c * LANES, (c + 1) * LANES)
        pad_ref[8:8 + S, :] = x_ref[:, cols].astype(F32)
        for r0 in range(0, S, rb):
            acc = jnp.zeros((rb, LANES), F32)
            for t in range(GDN_CONV):
                off = 8 - (GDN_CONV - 1) + t
                acc = acc + w_ref[t:t + 1, cols] * pad_ref[r0 + off:r0 + off + rb, :]
            y = _silu(acc)
            n = y * lax.rsqrt(jnp.sum(y * y, axis=-1, keepdims=True) + EPS) * scale
            o_ref[r0:r0 + rb, cols] = jnp.where(is_qk, n, y).astype(o_ref.dtype)


def _gdn_prep(proj, conv_w, B, S):
    T = B * S
    w = _GP_TILES * LANES
    ncol = 3 * GDN_HEADS // _GP_TILES
    rb = min(256, S)
    return pl.pallas_call(
        functools.partial(_gdn_prep_kernel, S=S, rb=rb),
        grid=(B, ncol),
        in_specs=[
            pl.BlockSpec((S, w), lambda b, j: (b, COL_GQKV // w + j)),
            pl.BlockSpec((GDN_CONV, w), lambda b, j: (0, j)),
        ],
        out_specs=pl.BlockSpec((S, w), lambda b, j: (b, j)),
        out_shape=jax.ShapeDtypeStruct((T, ncol * w), BF16),
        scratch_shapes=[pltpu.VMEM((S + 8, LANES), F32)],
        compiler_params=_cparams(("parallel", "parallel")),
        name="gdn_prep",
    )(proj, conv_w)


_PAIR = 2 * GDN_CHUNK


def _inv_unit_lower(lows, blk16, eye, between=lambda k: None):
    mm = lambda a, b: _dot(a.astype(BF16), b.astype(BF16))
    xs = [jnp.where(blk16, -low, 0.0) for low in lows]
    offs = [jnp.where(blk16, 0.0, low) for low in lows]
    x2 = [mm(x, x) for x in xs]
    x4 = [mm(a, a) for a in x2]
    d1 = [mm(eye + x, eye + a) for x, a in zip(xs, x2)]
    between(1)
    x8 = [mm(a, a) for a in x4]
    d2 = [mm(eye + a, eye + b) for a, b in zip(x4, x8)]
    dinv = [mm(a, b) for a, b in zip(d1, d2)]
    ns = [mm(d, o) for d, o in zip(dinv, offs)]
    between(2)
    n2 = [mm(n, n) for n in ns]
    t1 = [mm(eye - n, eye + m) for n, m in zip(ns, n2)]
    return [mm(t, d) for t, d in zip(t1, dinv)]


_GDN_HB = 4
_GDN_UNROLL = 2


def _gdn_kernel(q_ref, k_ref, v_ref, ga_ref, z_ref, gn_ref, o_ref, kw_s, c_s, qp_s, oi_s, a_s, *, S):
    hg = pl.program_id(1)
    P = _PAIR
    C = GDN_CHUNK
    row = lax.broadcasted_iota(jnp.int32, (P, P), 0)
    col = lax.broadcasted_iota(jnp.int32, (P, P), 1)
    same = row // C == col // C
    causal = jnp.logical_and(same, row >= col)
    strict = jnp.logical_and(same, row > col)
    blk16 = row // 16 == col // 16
    eye = jnp.where(row == col, 1.0, 0.0).astype(F32)
    first_half = row < C
    heads = [hg * _GDN_HB + hh for hh in range(_GDN_HB)]
    row2 = lax.broadcasted_iota(jnp.int32, (2 * LANES, 2 * LANES), 0) % LANES
    col2 = lax.broadcasted_iota(jnp.int32, (2 * LANES, 2 * LANES), 1)
    colr = lax.broadcasted_iota(jnp.int32, (P, 2 * LANES), 1) % LANES
    sel_gb = [jnp.where(row2 == jnp.where(col2 < LANES, LANE_A + h, LANE_B + h), 1.0, 0.0).astype(BF16) for h in heads]
    sel_gr = [jnp.where(colr == LANE_A + h, 1.0, 0.0).astype(BF16) for h in heads]

    def prep(it, between=lambda k: None):
        it = jnp.asarray(it, jnp.int32)
        chains = [(it * _GDN_UNROLL + u, hh) for u in range(_GDN_UNROLL) for hh in range(_GDN_HB)]
        r0 = [pl.multiple_of(p * P, P) for p, _ in chains]
        ln = [slice(hh * LANES, (hh + 1) * LANES) for _, hh in chains]
        hh_ = [hh for _, hh in chains]
        n = len(chains)
        q = [q_ref[pl.ds(r0[i], P), ln[i]].astype(F32) for i in range(n)]
        k = [k_ref[pl.ds(r0[i], P), ln[i]].astype(F32) for i in range(n)]
        v = [v_ref[pl.ds(r0[i], P), ln[i]].astype(F32) for i in range(n)]
        ga = [ga_ref[pl.ds(r0[i], P), :] for i in range(n)]
        gah = [x.astype(BF16) for x in ga]
        gam = [(ga[i] - gah[i].astype(F32)).astype(BF16) for i in range(n)]
        ga2 = [jnp.concatenate([gah[i], gam[i]], axis=1) for i in range(n)]
        gb = [_dot(ga2[i], sel_gb[hh_[i]]) for i in range(n)]
        gcol = [x[:, :LANES] for x in gb]
        bcol = [x[:, LANES:] for x in gb]
        grow = [_dot(sel_gr[hh_[i]], ga2[i], _NT) for i in range(n)]
        kb = [x.astype(BF16) for x in k]
        kk = [_dot(x, x, _NT) for x in kb]
        qkr = [_dot(q[i].astype(BF16), kb[i], _NT) for i in range(n)]
        between(0)
        glast = [jnp.where(first_half, g[C - 1:C, :], g[P - 1:P, :]) for g in gcol]
        decay = [jnp.where(causal, jnp.exp(jnp.where(causal, gcol[i] - grow[i], 0.0)), 0.0) for i in range(n)]
        low = [jnp.where(strict, bcol[i] * kk[i] * decay[i], 0.0) for i in range(n)]
        tinv = _inv_unit_lower(low, blk16, eye, between)
        eg = [jnp.exp(g) for g in gcol]
        rhs = [jnp.concatenate([v[i] * bcol[i], k[i] * (bcol[i] * eg[i])], axis=1) for i in range(n)]
        solb = [_dot3(tinv[i], rhs[i]).astype(BF16) for i in range(n)]
        between(3)
        qw = [_dot((qkr[i] * decay[i]).astype(BF16), solb[i]) for i in range(n)]
        ktail = [(k[i] * jnp.exp(glast[i] - gcol[i])).astype(BF16) for i in range(n)]
        for i, (p, hh) in enumerate(chains):
            qp_s[hh, pl.ds(r0[i], P), :] = (q[i] * eg[i] - qw[i][:, LANES:]).astype(BF16)
            oi_s[hh, pl.ds(r0[i], P), :] = qw[i][:, :LANES]
            for half in range(2):
                rows = slice(half * C, (half + 1) * C)
                c = 2 * p + half
                kwc = _dot(ktail[i][rows], solb[i][rows], _TN)
                c_s[hh, c] = kwc[:, :LANES]
                kw_s[hh, c] = kwc[:, LANES:].astype(BF16)
                a_s[hh, c] = jnp.broadcast_to(jnp.exp(glast[i][half * C:half * C + 1, :]), (8, LANES))

    def step(c, states):
        r0 = pl.multiple_of(c * C, C)
        hs = range(_GDN_HB)
        sb = [states[hh].astype(BF16) for hh in hs]
        ks = [_dot(kw_s[hh, c], sb[hh]) for hh in hs]
        os_ = [_dot(qp_s[hh, pl.ds(r0, C), :], sb[hh]) for hh in hs]
        new = [a_s[hh, c][0:1, :] * states[hh] - ks[hh] + c_s[hh, c] for hh in hs]
        for hh in hs:
            lanes = slice(hh * LANES, (hh + 1) * LANES)
            o = os_[hh] + oi_s[hh, pl.ds(r0, C), :]
            on = o * lax.rsqrt(jnp.mean(o * o, axis=-1, keepdims=True) + EPS) * gn_ref[...]
            z = z_ref[pl.ds(r0, C), lanes].astype(F32)
            o_ref[pl.ds(r0, C), lanes] = (on * _silu(z)).astype(o_ref.dtype)
        return tuple(new)

    per_it = 2 * _GDN_UNROLL
    n_it = S // (P * _GDN_UNROLL)

    def fused(it, states):
        st = [states]

        def between(k):
            st[0] = step((it - 1) * per_it + k, st[0])

        prep(it, between)
        return st[0]

    prep(0)
    states = lax.fori_loop(1, n_it, fused, tuple(jnp.zeros((GDN_DK, LANES), F32) for _ in range(_GDN_HB)))
    for k in range(per_it):
        states = step((n_it - 1) * per_it + k, states)


def _gdn(qkvn, gact, proj, gnorm, B, S):
    T = B * S
    nc = S // GDN_CHUNK
    hb = _GDN_HB
    ng = GDN_HEADS // hb
    w = hb * LANES
    blk = lambda off: pl.BlockSpec((S, w), lambda b, h: (b, off + h))
    return pl.pallas_call(
        functools.partial(_gdn_kernel, S=S),
        grid=(B, ng),
        in_specs=[
            blk(0), blk(ng), blk(2 * ng),
            pl.BlockSpec((S, LANES), lambda b, h: (b, 0)),
            blk(COL_GZ // w),
            pl.BlockSpec((1, LANES), lambda b, h: (0, 0)),
        ],
        out_specs=pl.BlockSpec((S, w), lambda b, h: (b, h)),
        out_shape=jax.ShapeDtypeStruct((T, GDN_HEADS * LANES), BF16),
        scratch_shapes=[
            pltpu.VMEM((hb, nc, GDN_DK, LANES), BF16),
            pltpu.VMEM((hb, nc, GDN_DK, LANES), F32),
            pltpu.VMEM((hb, S, LANES), BF16),
            pltpu.VMEM((hb, S, LANES), F32),
            pltpu.VMEM((hb, nc, 8, LANES), F32),
        ],
        compiler_params=_cparams(("parallel", "parallel"), VMEM_LIMIT_GDN),
        name="gdn",
    )(qkvn, qkvn, qkvn, gact, proj, gnorm)


AUG0 = FOX_HD
_NPAIR = FOX_HEADS * FOX_HD // LANES


def _fox_aug_selectors():
    import numpy as np
    sel = np.zeros((2 * _NPAIR, 3 * LANES, 2 * LANES), np.float32)
    for j in range(2 * _NPAIR):
        is_q = j < _NPAIR
        for slot in range(2):
            head = 2 * (j % _NPAIR) + slot
            c_lane, one_lane, sign = (AUG0, AUG0 + 3, 1.0) if is_q else (AUG0 + 3, AUG0, -1.0)
            for piece in range(3):
                sel[j, piece * LANES + LANE_F + head, slot * LANES + c_lane + piece] = sign
                sel[j, LANE_ONE, slot * LANES + one_lane + piece] = 1.0
    return jnp.asarray(sel, BF16)


def _fox_prep_kernel(x_ref, ga_ref, g_ref, sel_ref, o_ref):
    is_q = pl.program_id(1) < _NPAIR
    x = x_ref[...].astype(F32)
    row = lax.broadcasted_iota(jnp.int32, (LANES, LANES), 0)
    col = lax.broadcasted_iota(jnp.int32, (LANES, LANES), 1)
    grp = jnp.where(row // FOX_HD == col // FOX_HD, 1.0, 0.0).astype(BF16)
    xx = x * x
    hi = xx.astype(BF16)
    lo = (xx - hi.astype(F32)).astype(BF16)
    ms = _dot(jnp.concatenate([hi, lo], axis=1), jnp.concatenate([grp, grp], axis=0)) * (1.0 / FOX_HD)
    scale = jnp.where(is_q, LOG2E * FOX_HD ** -0.5, 1.0).astype(F32)
    xn = x * lax.rsqrt(ms + EPS) * (g_ref[0] * scale)
    xr = pltpu.roll(xn, FOX_HD, axis=1)
    aug = _dot(jnp.concatenate(_split3(ga_ref[...]), axis=1), sel_ref[0])
    lane = lax.broadcasted_iota(jnp.int32, x.shape, 1)
    o_ref[:, :LANES] = jnp.where(lane < FOX_HD, xn, aug[:, :LANES]).astype(o_ref.dtype)
    o_ref[:, LANES:] = jnp.where(lane < FOX_HD, xr, aug[:, LANES:]).astype(o_ref.dtype)


def _fox_prep(proj, gact, gains, T):
    tm = min(2048, T)
    ncol = 2 * _NPAIR
    return pl.pallas_call(
        _fox_prep_kernel,
        grid=(T // tm, ncol),
        in_specs=[
            pl.BlockSpec((tm, LANES), lambda i, j: (i, COL_FQKV // LANES + j)),
            pl.BlockSpec((tm, LANES), lambda i, j: (i, 0)),
            pl.BlockSpec((1, 1, LANES), lambda i, j: (j, 0, 0)),
            pl.BlockSpec((1, 3 * LANES, 2 * LANES), lambda i, j: (j, 0, 0)),
        ],
        out_specs=pl.BlockSpec((tm, 2 * LANES), lambda i, j: (i, j)),
        out_shape=jax.ShapeDtypeStruct((T, ncol * 2 * LANES), BF16),
        compiler_params=_cparams(("parallel", "arbitrary")),
        name="fox_prep",
    )(proj, gact, gains, _fox_aug_selectors())


def _vt_kernel(v_ref, o_ref):
    o_ref[...] = jnp.transpose(v_ref[...].astype(F32)).astype(o_ref.dtype)


def _fox_vt(proj, T):
    tm = min(2048, T)
    nhp = FOX_HEADS // 2
    return pl.pallas_call(
        _vt_kernel,
        grid=(T // tm, nhp),
        in_specs=[pl.BlockSpec((tm, LANES), lambda i, h: (i, COL_FQKV // LANES + 2 * nhp + h))],
        out_specs=pl.BlockSpec((LANES, tm), lambda i, h: (h, i)),
        out_shape=jax.ShapeDtypeStruct((nhp * LANES, T), BF16),
        compiler_params=_cparams(("parallel", "parallel")),
        name="fox_vt",
    )(proj)


def _fox_kernel(q_ref, k_ref, vt_ref, o_ref, *, tq):
    i = pl.program_id(2)
    hs = range(2)
    q = [q_ref[:, hh * LANES:(hh + 1) * LANES] for hh in hs]
    th = tq // 2
    keyi = lax.broadcasted_iota(jnp.int32, (th, th), 0)
    qryi = lax.broadcasted_iota(jnp.int32, (th, th), 1)
    tri = keyi <= qryi

    def kv_step(r0, nk, q0, carry, masked):
        ms, ls, accs = carry
        vt = [vt_ref[hh * FOX_HD:(hh + 1) * FOX_HD, pl.ds(r0, nk)] for hh in hs]
        s = [_dot(k_ref[pl.ds(r0, nk), hh * LANES:(hh + 1) * LANES], q[hh][q0:, :], _NT) for hh in hs]
        if masked:
            s = [jnp.where(tri, x, -jnp.inf) if x.shape[1] == th else
                 jnp.concatenate([jnp.where(tri, x[:, :th], -jnp.inf), x[:, th:]], axis=1) for x in s]
        m_old = [ms[hh][:, q0:] for hh in hs]
        m_new = [jnp.maximum(m_old[hh], jnp.max(s[hh], axis=0, keepdims=True)) for hh in hs]
        pr = [jnp.exp2(s[hh] - m_new[hh]) for hh in hs]
        alpha = [jnp.exp2(m_old[hh] - m_new[hh]) for hh in hs]
        pv = [_dot(vt[hh], pr[hh].astype(BF16)) for hh in hs]
        l_new = [alpha[hh] * ls[hh][:, q0:] + jnp.sum(pr[hh], axis=0, keepdims=True) for hh in hs]
        a_new = [alpha[hh] * accs[hh][:, q0:] + pv[hh] for hh in hs]
        keep = lambda old, new: [jnp.concatenate([old[hh][:, :q0], new[hh]], axis=1) if q0 else new[hh] for hh in hs]
        return keep(ms, m_new), keep(ls, l_new), keep(accs, a_new)

    init = ([jnp.full((1, tq), -jnp.inf, F32) for _ in hs], [jnp.zeros((1, tq), F32) for _ in hs],
            [jnp.zeros((FOX_HD, tq), F32) for _ in hs])
    carry = lax.fori_loop(0, i, lambda jb, c: kv_step(pl.multiple_of(jb * tq, tq), tq, 0, c, False), init)
    d0 = pl.multiple_of(i * tq, tq)
    carry = kv_step(d0, th, 0, carry, True)
    _, ls, accs = kv_step(pl.multiple_of(d0 + th, th), th, th, carry, True)
    ot = jnp.concatenate([accs[hh] / ls[hh] for hh in hs], axis=0)
    o_ref[...] = jnp.transpose(ot).astype(o_ref.dtype)


def _fox(qka, vt, B, S):
    T = B * S
    tq = min(1024, S)
    nq = S // tq
    nhp = FOX_HEADS // 2
    return pl.pallas_call(
        functools.partial(_fox_kernel, tq=tq),
        grid=(B, nhp, nq),
        in_specs=[
            pl.BlockSpec((tq, 2 * LANES), lambda b, h, i: (b * nq + i, h)),
            pl.BlockSpec((S, 2 * LANES), lambda b, h, i: (b, nhp + h)),
            pl.BlockSpec((LANES, S), lambda b, h, i: (h, b)),
        ],
        out_specs=pl.BlockSpec((tq, LANES), lambda b, h, i: (b * nq + i, h)),
        out_shape=jax.ShapeDtypeStruct((T, nhp * LANES), BF16),
        compiler_params=_cparams(("parallel", "parallel", "arbitrary")),
        name="fox",
    )(qka, qka, vt)


HALF = D_MODEL // 2


def _rms(x, gain):
    return x * lax.rsqrt(jnp.mean(x * x, axis=-1, keepdims=True) + EPS) * gain


def _pack_halves(x):
    lo = lax.bitcast_convert_type(x[:, :HALF].astype(BF16).astype(F32), jnp.uint32)
    hi = lax.bitcast_convert_type(x[:, HALF:].astype(BF16).astype(F32), jnp.uint32)
    return lax.shift_right_logical(lo, jnp.uint32(16)) | (hi & jnp.uint32(0xFFFF0000))


def _unpack_halves(w):
    lo = lax.bitcast_convert_type(lax.shift_left(w, jnp.uint32(16)), F32)
    hi = lax.bitcast_convert_type(w & jnp.uint32(0xFFFF0000), F32)
    return lo, hi


def _mix_kernel(oa_ref, of_ref, ma_ref, mf_ref, x_ref, wa_ref, wf_ref, wo_ref, g_ref, h_ref, hp_ref):
    ya = _dot(oa_ref[...], wa_ref[...])
    yf = _dot(of_ref[...], wf_ref[...])
    m = _sigmoid(ma_ref[...].astype(F32)) * ya + _sigmoid(mf_ref[...].astype(F32)) * yf
    h = x_ref[...] + _dot(m.astype(BF16), wo_ref[...])
    h_ref[...] = h
    hp_ref[...] = _pack_halves(_rms(h, g_ref[...]))


def _mix(o_a, o_f, proj, x2, wa, wf, wo, gain):
    T = x2.shape[0]
    tm = min(512, T)
    row = lambda c: pl.BlockSpec((tm, D_MODEL), lambda i: (i, c))
    full = pl.BlockSpec((D_MODEL, D_MODEL), lambda i: (0, 0))
    return pl.pallas_call(
        _mix_kernel,
        grid=(T // tm,),
        in_specs=[row(0), row(0), row(COL_MERGE // D_MODEL), row(COL_MERGE // D_MODEL + 1), row(0),
                  full, full, full, pl.BlockSpec((1, D_MODEL), lambda i: (0, 0))],
        out_specs=[row(0), pl.BlockSpec((tm, HALF), lambda i: (i, 0))],
        out_shape=[jax.ShapeDtypeStruct((T, D_MODEL), F32), jax.ShapeDtypeStruct((T, HALF), jnp.uint32)],
        compiler_params=_cparams(("parallel",)),
        name="mix_out",
    )(o_a, o_f, proj, proj, x2, wa, wf, wo, gain)


def _router_kernel(h_ref, g_ref, wr_ref, bias_ref, gates_ref, sel_ref, cnt_ref):
    @pl.when(pl.program_id(0) == 0)
    def _():
        cnt_ref[...] = jnp.zeros_like(cnt_ref)

    hn = _rms(h_ref[...], g_ref[...])
    hh = hn.astype(BF16)
    hl = (hn - hh.astype(F32)).astype(BF16)
    logits = _dot(wr_ref[0], hh, _NT) + _dot(wr_ref[0], hl, _NT) + _dot(wr_ref[1], hh, _NT)
    scores = _sigmoid(logits)
    work = scores + bias_ref[...]
    eidx = lax.broadcasted_iota(jnp.int32, work.shape, 0)
    sel = jnp.zeros(work.shape, F32)
    for _ in range(TOP_K):
        mx = jnp.max(work, axis=0, keepdims=True)
        first = jnp.min(jnp.where(work == mx, eidx, N_EXPERTS), axis=0, keepdims=True)
        onehot = eidx == first
        sel = jnp.where(onehot, 1.0, sel)
        work = jnp.where(onehot, -jnp.inf, work)
    s = jnp.where(sel > 0.0, scores, 0.0)
    gates_ref[...] = s / jnp.sum(s, axis=0, keepdims=True) * ROUTED_SCALE
    sel_ref[...] = sel
    tot = jnp.zeros((N_EXPERTS, 1), F32)
    for t0 in range(0, sel.shape[1], MOE_TT):
        n = jnp.sum(sel[:, t0:t0 + MOE_TT], axis=1, keepdims=True)
        tot = tot + jnp.floor((n + (SUB - 1)) * (1.0 / SUB)) * SUB
    cnt_ref[...] += jnp.broadcast_to(tot, cnt_ref.shape)


def _router(h1, gain, wr2, bias_col):
    T = h1.shape[0]
    tm = min(2 * MOE_TT, T)
    return pl.pallas_call(
        _router_kernel,
        grid=(T // tm,),
        in_specs=[
            pl.BlockSpec((tm, D_MODEL), lambda i: (i, 0)),
            pl.BlockSpec((1, D_MODEL), lambda i: (0, 0)),
            pl.BlockSpec((2, N_EXPERTS, D_MODEL), lambda i: (0, 0, 0)),
            pl.BlockSpec((N_EXPERTS, 1), lambda i: (0, 0)),
        ],
        out_specs=[
            pl.BlockSpec((N_EXPERTS, tm), lambda i: (0, i)),
            pl.BlockSpec((N_EXPERTS, tm), lambda i: (0, i)),
            pl.BlockSpec((N_EXPERTS, LANES), lambda i: (0, 0)),
        ],
        out_shape=[
            jax.ShapeDtypeStruct((N_EXPERTS, T), F32),
            jax.ShapeDtypeStruct((N_EXPERTS, T), F32),
            jax.ShapeDtypeStruct((N_EXPERTS, LANES), F32),
        ],
        compiler_params=_cparams(("arbitrary",)),
        name="router",
    )(h1, gain, wr2, bias_col)


MOE_TT = 256
SUB = 8
MOE_R = MOE_TT * TOP_K + N_EXPERTS * SUB


def _pos_kernel(sel_ref, gates_ref, cnt_ref, lpos_ref, w_ref, tab_ref, base_s, carry_s, *, tt, tme):
    r = lax.broadcasted_iota(jnp.int32, (N_EXPERTS, N_EXPERTS), 0)
    c = lax.broadcasted_iota(jnp.int32, (N_EXPERTS, N_EXPERTS), 1)
    below = jnp.where(r > c, 1.0, 0.0).astype(BF16)

    @pl.when(pl.program_id(0) == 0)
    def _():
        padded = jnp.floor((cnt_ref[...] + (tme - 1)) * (1.0 / tme)) * tme
        base_s[...] = _dot_lx(below, padded)
        carry_s[...] = jnp.zeros_like(carry_s)

    sel = sel_ref[...]
    gates = gates_ref[...]
    rr = lax.broadcasted_iota(jnp.int32, (tt, tt), 0)
    cc = lax.broadcasted_iota(jnp.int32, (tt, tt), 1)
    cs = _dot(sel.astype(BF16), jnp.where(rr <= cc, 1.0, 0.0).astype(BF16))
    n = jnp.sum(sel, axis=1, keepdims=True)
    npad = jnp.broadcast_to(jnp.floor((n + (SUB - 1)) * (1.0 / SUB)) * SUB, (N_EXPERTS, LANES))
    loff = _dot_lx(below, npad)
    gstart = base_s[...] + carry_s[...]
    carry_s[...] += npad

    wide = lambda v: jnp.concatenate([v] * (TAB_W // LANES), axis=1)
    incl = jnp.where(r >= c, 1.0, 0.0).astype(BF16)
    nfull = jnp.floor(npad * (1.0 / PIECE))
    has8 = npad * (1.0 / SUB) - 2.0 * nfull
    cend = _dot_lx(incl, nfull)
    hend = _dot_lx(incl, has8)
    jl = lax.broadcasted_iota(jnp.int32, (N_EXPERTS, TAB_W), 1).astype(F32)
    ei = lax.broadcasted_iota(jnp.int32, (N_EXPERTS, TAB_W), 0).astype(F32)

    def owner(ends):
        e = jnp.sum(jnp.where(wide(ends) <= jl, 1.0, 0.0), axis=0, keepdims=True)
        return ei == jnp.minimum(e, N_EXPERTS - 1.0)

    take = lambda oh, v: jnp.sum(jnp.where(oh, wide(v), 0.0), axis=0, keepdims=True)
    oh16 = owner(cend)
    oh8 = owner(hend)
    off16 = PIECE * (jl[0:1, :] - take(oh16, cend - nfull))
    off8 = PIECE * take(oh8, nfull)
    last = N_EXPERTS - 1
    lane = lax.broadcasted_iota(jnp.int32, (1, TAB_W), 1)
    misc = jnp.where(lane == 0, wide(cend[last:, :]),
                     jnp.where(lane == 1, wide(hend[last:, :]), wide(loff[last:, :] + npad[last:, :])))
    tab_ref[0] = jnp.concatenate([take(oh16, gstart) + off16, take(oh16, loff) + off16,
                                  take(oh8, gstart) + off8, take(oh8, loff) + off8, misc,
                                  jnp.zeros((3, TAB_W), F32)], axis=0)
    lposd = cs - sel + jnp.concatenate([loff] * (tt // LANES), axis=1)
    eidx = lax.broadcasted_iota(jnp.int32, sel.shape, 0)
    rem = sel
    ps, ws = [], []
    for _ in range(TOP_K):
        first = jnp.min(jnp.where(rem > 0.0, eidx, N_EXPERTS), axis=0, keepdims=True)
        onehot = eidx == first
        ps.append(jnp.sum(jnp.where(onehot, lposd, 0.0), axis=0, keepdims=True))
        ws.append(jnp.sum(jnp.where(onehot, gates, 0.0), axis=0, keepdims=True))
        rem = jnp.where(onehot, 0.0, rem)
    lpos_ref[...] = jnp.concatenate(ps, axis=0).astype(jnp.int32)
    w_ref[...] = jnp.concatenate(ws, axis=0)


def _positions(sel, gates, cnt, tme):
    T = sel.shape[1]
    tt = MOE_TT
    blk = pl.BlockSpec((N_EXPERTS, tt), lambda i: (0, i))
    return pl.pallas_call(
        functools.partial(_pos_kernel, tt=tt, tme=tme),
        grid=(T // tt,),
        in_specs=[blk, blk, pl.BlockSpec((N_EXPERTS, LANES), lambda i: (0, 0))],
        out_specs=[pl.BlockSpec((TOP_K, tt), lambda i: (0, i)), pl.BlockSpec((TOP_K, tt), lambda i: (0, i)),
                   pl.BlockSpec((1, TAB_ROWS, TAB_W), lambda i: (i, 0, 0))],
        out_shape=[jax.ShapeDtypeStruct((TOP_K, T), jnp.int32), jax.ShapeDtypeStruct((TOP_K, T), F32),
                   jax.ShapeDtypeStruct((T // tt, TAB_ROWS, TAB_W), F32)],
        scratch_shapes=[pltpu.VMEM((N_EXPERTS, LANES), F32), pltpu.VMEM((N_EXPERTS, LANES), F32)],
        compiler_params=_cparams(("arbitrary",)),
        name="moe_positions",
    )(sel, gates, cnt)


PIECE = 2 * SUB
TAB_W = 2 * LANES
TAB_ROWS = 8
TAB_G16, TAB_L16, TAB_G8, TAB_L8 = 0, TAB_W, 2 * TAB_W, 3 * TAB_W
TAB_N16, TAB_N8, TAB_USED = 4 * TAB_W, 4 * TAB_W + 1, 4 * TAB_W + 2
TAB_LEN = TAB_ROWS * TAB_W


def _segment_copies(tab_ref, src_of, dst_of, sem):
    def pieces(g0, l0, rows):
        def body(j, carry):
            g = pl.multiple_of(tab_ref[0, 0, g0 + j], SUB)
            l = pl.multiple_of(tab_ref[0, 0, l0 + j], SUB)
            pltpu.make_async_copy(src_of(g, l, rows), dst_of(g, l, rows), sem).start()
            return carry
        return body

    lax.fori_loop(0, tab_ref[0, 0, TAB_N16], pieces(TAB_G16, TAB_L16, PIECE), 0)
    lax.fori_loop(0, tab_ref[0, 0, TAB_N8], pieces(TAB_G8, TAB_L8, SUB), 0)


def _wait_segment_copies(tab_ref, src_of, dst_of, sem):
    used = tab_ref[0, 0, TAB_USED]

    def unit(rows):
        def body(i, carry):
            pltpu.make_async_copy(src_of(0, 0, rows), dst_of(0, 0, rows), sem).wait()
            return carry
        return body

    lax.fori_loop(0, lax.shift_right_logical(used, 7), unit(LANES), 0)
    lax.fori_loop(0, lax.shift_right_logical(used, 3) & (LANES // SUB - 1), unit(SUB), 0)


_ROW_CHUNK = MOE_R // 4


def _pack2(lo, hi):
    lo = lax.bitcast_convert_type(lo.astype(BF16).astype(F32), jnp.uint32)
    hi = lax.bitcast_convert_type(hi.astype(BF16).astype(F32), jnp.uint32)
    return lax.shift_right_logical(lo, jnp.uint32(16)) | (hi & jnp.uint32(0xFFFF0000))


def _dispatch_kernel(meta_ref, seg_ref, prev_ref, lpos_ref, hp_ref, xs_ref, zero_s, sort_s, sem, zsem, *,
                     tt, tme, n_tiles):
    i = pl.program_id(0)
    slot = i % 2

    @pl.when(i == 0)
    def _():
        zero_s[...] = jnp.zeros_like(zero_s)
        for e in range(N_EXPERTS):
            pltpu.make_async_copy(zero_s, xs_ref.at[pl.ds(pl.multiple_of(meta_ref[e], SUB), tme)], zsem).start()
        for e in range(N_EXPERTS):
            pltpu.make_async_copy(zero_s, xs_ref.at[pl.ds(pl.multiple_of(meta_ref[e], SUB), tme)], zsem).wait()

        def zero_tile(t, carry):
            cp = pltpu.make_async_copy(zero_s, xs_ref.at[pl.ds(pl.multiple_of(t * tme, tme), tme)], zsem)
            cp.start()
            cp.wait()
            return carry

        lax.fori_loop(meta_ref[N_EXPERTS], n_tiles, zero_tile, 0)

    lo, hi = _unpack_halves(hp_ref[...])
    lo = lo.astype(BF16)
    hi = hi.astype(BF16)
    lp = lpos_ref[...].astype(jnp.int16)
    one = jnp.ones((1, 1), BF16)
    for rc in range(MOE_R // _ROW_CHUNK):
        rowi = (lax.broadcasted_iota(jnp.int32, (_ROW_CHUNK, tt), 0) + rc * _ROW_CHUNK).astype(jnp.int16)
        pm = jnp.zeros((_ROW_CHUNK, tt), BF16)
        for k in range(TOP_K):
            pm = jnp.where(rowi == lp[k:k + 1, :], one, pm)
        sort_s[slot, rc * _ROW_CHUNK:(rc + 1) * _ROW_CHUNK, :] = _pack2(_dot(pm, lo), _dot(pm, hi))

    src_of = lambda s: (lambda g, l, rows: sort_s.at[s, pl.ds(l, rows)])
    dst_of = lambda g, l, rows: xs_ref.at[pl.ds(g, rows)]

    @pl.when(i > 0)
    def _():
        _wait_segment_copies(prev_ref, src_of(1 - slot), dst_of, sem.at[1 - slot])

    _segment_copies(seg_ref, src_of(slot), dst_of, sem.at[slot])

    @pl.when(i == pl.num_programs(0) - 1)
    def _():
        _wait_segment_copies(seg_ref, src_of(slot), dst_of, sem.at[slot])


def _dispatch(meta, seg, lpos, hp, n_tiles, tme):
    T = hp.shape[0]
    tt = MOE_TT
    return pl.pallas_call(
        functools.partial(_dispatch_kernel, tt=tt, tme=tme, n_tiles=n_tiles),
        grid_spec=pltpu.PrefetchScalarGridSpec(
            num_scalar_prefetch=1,
            grid=(T // tt,),
            in_specs=[
                pl.BlockSpec((1, 1, TAB_LEN), lambda i, f: (i, 0, 0), memory_space=pltpu.SMEM),
                pl.BlockSpec((1, 1, TAB_LEN), lambda i, f: (jnp.maximum(i - 1, 0), 0, 0), memory_space=pltpu.SMEM),
                pl.BlockSpec((TOP_K, tt), lambda i, f: (0, i)),
                pl.BlockSpec((tt, HALF), lambda i, f: (i, 0)),
            ],
            out_specs=pl.BlockSpec(memory_space=pl.ANY),
            scratch_shapes=[pltpu.VMEM((tme, HALF), jnp.uint32), pltpu.VMEM((2, MOE_R, HALF), jnp.uint32),
                            pltpu.SemaphoreType.DMA((2,)), pltpu.SemaphoreType.DMA],
        ),
        out_shape=jax.ShapeDtypeStruct((n_tiles * tme, HALF), jnp.uint32),
        compiler_params=_cparams(("arbitrary",)),
        name="moe_dispatch",
    )(meta, seg, seg, lpos, hp)


def _expert_kernel(te_ref, nu_ref, xs_ref, wgu_ref, wd_ref, ys_ref):
    @pl.when(pl.program_id(0) < nu_ref[0])
    def _():
        lo, hi = _unpack_halves(xs_ref[...])
        gu = _dot(lo.astype(BF16), wgu_ref[0, :HALF, :]) + _dot(hi.astype(BF16), wgu_ref[0, HALF:, :])
        hid = _silu(gu[:, :D_EXPERT]) * gu[:, D_EXPERT:]
        ys_ref[...] = _pack_halves(_dot(hid.astype(BF16), wd_ref[0]))

    @pl.when(pl.program_id(0) >= nu_ref[0])
    def _():
        ys_ref[...] = jnp.zeros_like(ys_ref)


def _experts(tile_e, n_used, xs, wgu, wd, n_tiles, tme):
    clamp = lambda i, te, nu: (jnp.minimum(i, nu[0] - 1), 0)
    return pl.pallas_call(
        _expert_kernel,
        grid_spec=pltpu.PrefetchScalarGridSpec(
            num_scalar_prefetch=2,
            grid=(n_tiles,),
            in_specs=[
                pl.BlockSpec((tme, HALF), clamp),
                pl.BlockSpec((1, D_MODEL, 2 * D_EXPERT), lambda i, te, nu: (te[i], 0, 0)),
                pl.BlockSpec((1, D_EXPERT, D_MODEL), lambda i, te, nu: (te[i], 0, 0)),
            ],
            out_specs=pl.BlockSpec((tme, HALF), lambda i, te, nu: (i, 0)),
        ),
        out_shape=jax.ShapeDtypeStruct((n_tiles * tme, HALF), jnp.uint32),
        compiler_params=_cparams(("arbitrary",)),
        name="moe_experts",
    )(tile_e, n_used, xs, wgu, wd)


def _tail_kernel(seg_ref, ys_ref, lpos_ref, w_ref, h_ref, p_ref, gf_ref, wsgu_ref, wsd_ref, wple_ref, gple_ref,
                 gpg_ref, wpg_ref, o_ref, buf, sem, *, tt):
    src_of = lambda g, l, rows: ys_ref.at[pl.ds(g, rows)]
    dst_of = lambda g, l, rows: buf.at[pl.ds(l, rows)]
    _segment_copies(seg_ref, src_of, dst_of, sem)

    used = seg_ref[0, 0, TAB_USED]

    def clear(j, carry):
        buf[pl.ds(pl.multiple_of(j * SUB, SUB), SUB), :] = jnp.zeros((SUB, HALF), jnp.uint32)
        return carry

    lax.fori_loop(lax.shift_right_logical(used, 3), MOE_R // SUB, clear, 0)

    h = h_ref[...]
    hn = _rms(h, gf_ref[...]).astype(BF16)
    sgu = _dot(hn, wsgu_ref[...])
    shared = _dot((_silu(sgu[:, :D_SHARED]) * sgu[:, D_SHARED:]).astype(BF16), wsd_ref[...])
    e = _rms(_dot(p_ref[...].astype(BF16), wple_ref[...]), gple_ref[...])
    _wait_segment_copies(seg_ref, src_of, dst_of, sem)

    lp = lpos_ref[...].astype(jnp.int16)
    wt = w_ref[...].astype(BF16)
    acc_lo = jnp.zeros((tt, HALF), F32)
    acc_hi = jnp.zeros((tt, HALF), F32)
    for rc in range(MOE_R // _ROW_CHUNK):
        coli = (lax.broadcasted_iota(jnp.int32, (tt, _ROW_CHUNK), 1) + rc * _ROW_CHUNK).astype(jnp.int16)
        g = jnp.zeros((tt, _ROW_CHUNK), BF16)
        for k in range(TOP_K):
            g = jnp.where(coli == lp[:, k:k + 1], wt[:, k:k + 1], g)
        lo, hi = _unpack_halves(buf[rc * _ROW_CHUNK:(rc + 1) * _ROW_CHUNK, :])
        acc_lo = acc_lo + _dot(g, lo.astype(BF16))
        acc_hi = acc_hi + _dot(g, hi.astype(BF16))
    h2 = h + jnp.concatenate([acc_lo, acc_hi], axis=1) + shared
    gte = _sigmoid(_dot(_rms(h2, gpg_ref[...]).astype(BF16), wpg_ref[...]))
    o_ref[...] = h2 + gte * e


def _tail(seg, ys, lpos_tok, w_tok, h1, p2, gf, wsgu, wsd, wple, gple, gpg, wpg):
    T = h1.shape[0]
    tt = MOE_TT
    row = lambda n: pl.BlockSpec((tt, n), lambda i: (i, 0))
    full = lambda a: pl.BlockSpec(a.shape, lambda i: (0,) * a.ndim)
    return pl.pallas_call(
        functools.partial(_tail_kernel, tt=tt),
        grid=(T // tt,),
        in_specs=[
            pl.BlockSpec((1, 1, TAB_LEN), lambda i: (i, 0, 0), memory_space=pltpu.SMEM),
            pl.BlockSpec(memory_space=pl.ANY),
            row(TOP_K), row(TOP_K), row(D_MODEL), row(PLE_DIM),
            full(gf), full(wsgu), full(wsd), full(wple), full(gple), full(gpg), full(wpg),
        ],
        out_specs=row(D_MODEL),
        out_shape=jax.ShapeDtypeStruct((T, D_MODEL), F32),
        scratch_shapes=[pltpu.VMEM((MOE_R, HALF), jnp.uint32), pltpu.SemaphoreType.DMA],
        compiler_params=_cparams(("arbitrary",)),
        name="moe_combine_ple",
    )(seg, ys, lpos_tok, w_tok, h1, p2, gf, wsgu, wsd, wple, gple, gpg, wpg)


def _stages(x, p, norm_mix, w_in, conv_w, a_log, dt_bias, gdn_norm, fox_f_bias, q_norm, k_norm,
            w_branch_gdn, w_branch_fox, w_out, norm_ffn, w_router, router_bias,
            we_gate, we_up, we_down, ws_gate, ws_up, ws_down, w_ple, ple_norm, ple_gate_norm, w_ple_gate):
    B, S, _ = x.shape
    T = B * S
    st = {}
    wi = w_in[0]
    o0 = 3 * GDN_HEADS * GDN_DK
    o1 = o0 + GDN_HEADS * GDN_DK
    o2 = o1 + GDN_HEADS
    o3 = o2 + GDN_HEADS
    o4 = o3 + 3 * FOX_HEADS * FOX_HD
    o5 = o4 + FOX_HEADS
    w_big = jnp.concatenate([wi[:, :o1], wi[:, o3:o4], wi[:, o5:]], axis=1).astype(BF16)
    w_small = jnp.concatenate([wi[:, o1:o3], wi[:, o4:o5],
                               jnp.zeros((D_MODEL, LANES - 2 * GDN_HEADS - FOX_HEADS), F32)], axis=1).astype(BF16)
    proj, small = _in_proj(x.reshape(T, D_MODEL), norm_mix[0].reshape(1, D_MODEL), w_big, w_small)
    st["proj_big"] = proj
    st["small"] = small

    pad = lambda v, off: jnp.zeros((1, LANES), F32).at[0, off:off + v.shape[0]].set(v)
    alog_row = pad(a_log[0], LANE_A)
    bias_row = pad(dt_bias[0], LANE_A) + pad(fox_f_bias[0], LANE_F)
    gact = _gates(small, alog_row, bias_row, B, S)
    st["gact"] = gact

    qkvn = _gdn_prep(proj, conv_w[0], B, S)
    st["qkvn"] = qkvn
    o_a = _gdn(qkvn, gact, proj, gdn_norm[0].reshape(1, LANES), B, S)
    st["o_a"] = o_a

    nrep = FOX_HEADS * FOX_HD // LANES
    gains = jnp.concatenate([jnp.tile(q_norm[0], (nrep, LANES // FOX_HD)),
                             jnp.tile(k_norm[0], (nrep, LANES // FOX_HD))], axis=0).reshape(2 * nrep, 1, LANES)
    qka = _fox_prep(proj, gact, gains, T)
    o_f = _fox(qka, _fox_vt(proj, T), B, S)
    st["o_f"] = o_f

    x2 = x.reshape(T, D_MODEL)
    gf = norm_ffn[0].reshape(1, D_MODEL)
    h1, hp = _mix(o_a, o_f, proj, x2, w_branch_gdn[0].astype(BF16), w_branch_fox[0].astype(BF16),
                  w_out[0].astype(BF16), gf)
    st["h1"] = h1

    wrt = w_router[0].T
    wr_hi = wrt.astype(BF16)
    wr2 = jnp.stack([wr_hi, (wrt - wr_hi.astype(F32)).astype(BF16)])
    gates, sel, cnt = _router(h1, gf, wr2, router_bias[0].reshape(N_EXPERTS, 1))
    st["gates_t"] = gates

    tme = 512 if T * TOP_K // N_EXPERTS >= 2048 else 64
    lpos, wts, segf = _positions(sel, gates, cnt, tme)
    nt_tok = T // MOE_TT
    seg = segf.astype(jnp.int32).reshape(nt_tok, 1, TAB_LEN)
    cnt1 = cnt[:, 0].astype(jnp.int32)
    padded = (cnt1 + (tme - 1)) // tme * tme
    ends = jnp.cumsum(padded)
    n_tiles = (T * TOP_K + (SUB - 1) * N_EXPERTS * nt_tok) // tme + N_EXPERTS
    tile_start = jnp.arange(n_tiles, dtype=jnp.int32) * tme
    tile_e = jnp.minimum(jnp.sum((ends[None, :] <= tile_start[:, None]).astype(jnp.int32), axis=1), N_EXPERTS - 1)
    n_used = (ends[-1:] // tme).astype(jnp.int32)
    fill = (ends - padded + cnt1).astype(jnp.int32)
    xs = _dispatch(jnp.concatenate([fill, n_used]), seg, lpos, hp, n_tiles, tme)
    wgu = jnp.concatenate([we_gate[0], we_up[0]], axis=2).astype(BF16)
    ys = _experts(tile_e, n_used, xs, wgu, we_down[0].astype(BF16), n_tiles, tme)

    row = lambda v: v.reshape(1, D_MODEL)
    wsgu = jnp.concatenate([ws_gate[0], ws_up[0]], axis=1).astype(BF16)
    out = _tail(seg, ys, lpos.T, wts.T, h1, p[0].reshape(T, PLE_DIM), gf, wsgu, ws_down[0].astype(BF16),
                w_ple[0].astype(BF16), row(ple_norm[0]), row(ple_gate_norm[0]), w_ple_gate[0].astype(BF16))
    st["out"] = out.reshape(B, S, D_MODEL)
    return st


def kernel(x, p, norm_mix, w_in, conv_w, a_log, dt_bias, gdn_norm, fox_f_bias, q_norm, k_norm, w_branch_gdn, w_branch_fox, w_out, norm_ffn, w_router, router_bias, we_gate, we_up, we_down, ws_gate, ws_up, ws_down, w_ple, ple_norm, ple_gate_norm, w_ple_gate):
    return _stages(x, p, norm_mix, w_in, conv_w, a_log, dt_bias, gdn_norm, fox_f_bias, q_norm, k_norm,
                   w_branch_gdn, w_branch_fox, w_out, norm_ffn, w_router, router_bias,
                   we_gate, we_up, we_down, ws_gate, ws_up, ws_down, w_ple, ple_norm, ple_gate_norm,
                   w_ple_gate)["out"]
```

```python
import functools

import jax
import jax.numpy as jnp
from jax import lax
from jax.experimental import pallas as pl
from jax.experimental.pallas import tpu as pltpu

F32 = jnp.float32
BF16 = jnp.bfloat16

D_MODEL = 1024
PLE_DIM = 256
EPS = 1e-6
GDN_HEADS = 8
GDN_DK = 128
GDN_CONV = 4
GDN_CHUNK = 64
FOX_HEADS = 16
FOX_HD = 64
N_EXPERTS = 64
TOP_K = 8
D_EXPERT = 256
D_SHARED = 256
ROUTED_SCALE = 2.5

LANES = 128
COL_GQKV = 0
COL_GZ = 3072
COL_FQKV = 4096
COL_MERGE = 7168
N_BIG = 9216
LANE_A = 0
LANE_B = 8
LANE_F = 16
LANE_ONE = 127
LOG2E = 1.4426950408889634

VMEM_LIMIT = 48 * 1024 * 1024
VMEM_LIMIT_GDN = 56 * 1024 * 1024


def _cparams(sem, vmem=VMEM_LIMIT):
    return pltpu.CompilerParams(dimension_semantics=sem, vmem_limit_bytes=vmem)


def _split3(x):
    h = x.astype(BF16)
    r = x - h.astype(F32)
    m = r.astype(BF16)
    l = (r - m.astype(F32)).astype(BF16)
    return h, m, l


def _dot(a, b, dims=(((1,), (0,)), ((), ()))):
    return lax.dot_general(a, b, dims, preferred_element_type=F32)


_NT = (((1,), (1,)), ((), ()))
_TN = (((0,), (0,)), ((), ()))


def _dot_xl(x, sel, dims=(((1,), (0,)), ((), ()))):
    h, m, l = _split3(x)
    return _dot(h, sel, dims) + _dot(m, sel, dims) + _dot(l, sel, dims)


def _dot_lx(sel, x, dims=(((1,), (0,)), ((), ()))):
    h, m, l = _split3(x)
    return _dot(sel, h, dims) + _dot(sel, m, dims) + _dot(sel, l, dims)


def _dot3(a, b):
    ah = a.astype(BF16)
    al = (a - ah.astype(F32)).astype(BF16)
    bh = b.astype(BF16)
    bl = (b - bh.astype(F32)).astype(BF16)
    return _dot(jnp.concatenate([ah, ah, al], axis=1), jnp.concatenate([bh, bl, bh], axis=0))


def _sigmoid(x):
    return 1.0 / (1.0 + jnp.exp(-x))


def _silu(x):
    return x * _sigmoid(x)


def _softplus(x):
    return jnp.maximum(x, 0.0) + jnp.log1p(jnp.exp(-jnp.abs(x)))


def _in_proj_kernel(x_ref, g_ref, wb_ref, ws_ref, ob_ref, os_ref, xn_ref):
    @pl.when(pl.program_id(1) == 0)
    def _():
        x = x_ref[...]
        ms = jnp.mean(x * x, axis=-1, keepdims=True)
        xn = (x * lax.rsqrt(ms + EPS) * g_ref[...]).astype(BF16)
        xn_ref[...] = xn
        os_ref[...] = _dot(xn, ws_ref[...])

    ob_ref[...] = _dot(xn_ref[...], wb_ref[...]).astype(ob_ref.dtype)


def _in_proj(x2, gain, w_big, w_small):
    T = x2.shape[0]
    tm = min(1024, T)
    tn = 1024
    return pl.pallas_call(
        _in_proj_kernel,
        grid=(T // tm, N_BIG // tn),
        in_specs=[
            pl.BlockSpec((tm, D_MODEL), lambda i, j: (i, 0)),
            pl.BlockSpec((1, D_MODEL), lambda i, j: (0, 0)),
            pl.BlockSpec((D_MODEL, tn), lambda i, j: (0, j)),
            pl.BlockSpec((D_MODEL, LANES), lambda i, j: (0, 0)),
        ],
        out_specs=[
            pl.BlockSpec((tm, tn), lambda i, j: (i, j)),
            pl.BlockSpec((tm, LANES), lambda i, j: (i, 0)),
        ],
        out_shape=[
            jax.ShapeDtypeStruct((T, N_BIG), BF16),
            jax.ShapeDtypeStruct((T, LANES), F32),
        ],
        scratch_shapes=[pltpu.VMEM((tm, D_MODEL), BF16)],
        compiler_params=_cparams(("parallel", "arbitrary")),
        name="in_proj",
    )(x2, gain, w_big, w_small)


def _gates_kernel(s_ref, alog_ref, bias_ref, o_ref, carry_ref, *, sb):
    @pl.when(pl.program_id(1) == 0)
    def _():
        carry_ref[...] = jnp.zeros_like(carry_ref)

    x = s_ref[...] + bias_ref[...]
    lane = lax.broadcasted_iota(jnp.int32, x.shape, 1)
    e = jnp.log1p(jnp.exp(-jnp.abs(x)))
    log_alpha = -jnp.exp(alog_ref[...]) * (jnp.maximum(x, 0.0) + e)
    beta = _sigmoid(x)
    log_f = -(jnp.maximum(-x, 0.0) + e)
    is_a = lane < LANE_B
    is_b = jnp.logical_and(lane >= LANE_B, lane < LANE_F)
    is_f = jnp.logical_and(lane >= LANE_F, lane < LANE_F + FOX_HEADS)
    val = jnp.where(is_a, log_alpha, jnp.where(is_f, log_f, 0.0))
    row = lax.broadcasted_iota(jnp.int32, (sb, sb), 0)
    col = lax.broadcasted_iota(jnp.int32, (sb, sb), 1)
    tri = row >= col
    tri_full = jnp.where(tri, 1.0, 0.0).astype(BF16)
    tri_chunk = jnp.where(jnp.logical_and(tri, row // GDN_CHUNK == col // GDN_CHUNK), 1.0, 0.0).astype(BF16)
    h, m, l = _split3(val)
    cs_full = _dot(tri_full, h) + _dot(tri_full, m) + _dot(tri_full, l) + carry_ref[0:1, :]
    cs_chunk = _dot(tri_chunk, h) + _dot(tri_chunk, m) + _dot(tri_chunk, l)
    rest = jnp.where(is_f, LOG2E * cs_full, jnp.where(lane == LANE_ONE, 1.0, 0.0))
    o_ref[...] = jnp.where(is_a, cs_chunk, jnp.where(is_b, beta, rest))
    carry_ref[...] = jnp.broadcast_to(cs_full[sb - 1:sb, :], carry_ref.shape)


def _gates(small, alog_row, bias_row, B, S):
    sb = min(256, S)
    nsb = S // sb
    return pl.pallas_call(
        functools.partial(_gates_kernel, sb=sb),
        grid=(B, nsb),
        in_specs=[
            pl.BlockSpec((sb, LANES), lambda b, j: (b * nsb + j, 0)),
            pl.BlockSpec((1, LANES), lambda b, j: (0, 0)),
            pl.BlockSpec((1, LANES), lambda b, j: (0, 0)),
        ],
        out_specs=pl.BlockSpec((sb, LANES), lambda b, j: (b * nsb + j, 0)),
        out_shape=jax.ShapeDtypeStruct(small.shape, F32),
        scratch_shapes=[pltpu.VMEM((8, LANES), F32)],
        compiler_params=_cparams(("parallel", "arbitrary")),
        name="gates",
    )(small, alog_row, bias_row)


_GP_TILES = 4


def _gdn_prep_kernel(x_ref, w_ref, o_ref, pad_ref, *, S, rb):
    j = pl.program_id(1)
    per_kind = GDN_HEADS // _GP_TILES
    is_qk = j < 2 * per_kind
    scale = jnp.where(j < per_kind, GDN_DK ** -0.5, 1.0).astype(F32)
    pad_ref[0:8, :] = jnp.zeros((8, LANES), F32)
    for c in range(_GP_TILES):
        cols = slice(c * LANES, (c + 1) * LANES)
        pad_ref[8:8 + S, :] = x_ref[:, cols].astype(F32)
        for r0 in range(0, S, rb):
            acc = jnp.zeros((rb, LANES), F32)
            for t in range(GDN_CONV):
                off = 8 - (GDN_CONV - 1) + t
                acc = acc + w_ref[t:t + 1, cols] * pad_ref[r0 + off:r0 + off + rb, :]
            y = _silu(acc)
            n = y * lax.rsqrt(jnp.sum(y * y, axis=-1, keepdims=True) + EPS) * scale
            o_ref[r0:r0 + rb, cols] = jnp.where(is_qk, n, y).astype(o_ref.dtype)


def _gdn_prep(proj, conv_w, B, S):
    T = B * S
    w = _GP_TILES * LANES
    ncol = 3 * GDN_HEADS // _GP_TILES
    rb = min(256, S)
    return pl.pallas_call(
        functools.partial(_gdn_prep_kernel, S=S, rb=rb),
        grid=(B, ncol),
        in_specs=[
            pl.BlockSpec((S, w), lambda b, j: (b, COL_GQKV // w + j)),
            pl.BlockSpec((GDN_CONV, w), lambda b, j: (0, j)),
        ],
        out_specs=pl.BlockSpec((S, w), lambda b, j: (b, j)),
        out_shape=jax.ShapeDtypeStruct((T, ncol * w), BF16),
        scratch_shapes=[pltpu.VMEM((S + 8, LANES), F32)],
        compiler_params=_cparams(("parallel", "parallel")),
        name="gdn_prep",
    )(proj, conv_w)


_PAIR = 2 * GDN_CHUNK


def _inv_unit_lower(lows, blk16, eye, between=lambda k: None):
    mm = lambda a, b: _dot(a.astype(BF16), b.astype(BF16))
    xs = [jnp.where(blk16, -low, 0.0) for low in lows]
    offs = [jnp.where(blk16, 0.0, low) for low in lows]
    x2 = [mm(x, x) for x in xs]
    x4 = [mm(a, a) for a in x2]
    d1 = [mm(eye + x, eye + a) for x, a in zip(xs, x2)]
    between(1)
    x8 = [mm(a, a) for a in x4]
    d2 = [mm(eye + a, eye + b) for a, b in zip(x4, x8)]
    dinv = [mm(a, b) for a, b in zip(d1, d2)]
    ns = [mm(d, o) for d, o in zip(dinv, offs)]
    between(2)
    n2 = [mm(n, n) for n in ns]
    t1 = [mm(eye - n, eye + m) for n, m in zip(ns, n2)]
    return [mm(t, d) for t, d in zip(t1, dinv)]


_GDN_HB = 4
_GDN_UNROLL = 2


def _gdn_kernel(q_ref, k_ref, v_ref, ga_ref, z_ref, gn_ref, o_ref, kw_s, c_s, qp_s, oi_s, a_s, *, S):
    hg = pl.program_id(1)
    P = _PAIR
    C = GDN_CHUNK
    row = lax.broadcasted_iota(jnp.int32, (P, P), 0)
    col = lax.broadcasted_iota(jnp.int32, (P, P), 1)
    same = row // C == col // C
    causal = jnp.logical_and(same, row >= col)
    strict = jnp.logical_and(same, row > col)
    blk16 = row // 16 == col // 16
    eye = jnp.where(row == col, 1.0, 0.0).astype(F32)
    first_half = row < C
    heads = [hg * _GDN_HB + hh for hh in range(_GDN_HB)]
    row2 = lax.broadcasted_iota(jnp.int32, (2 * LANES, 2 * LANES), 0) % LANES
    col2 = lax.broadcasted_iota(jnp.int32, (2 * LANES, 2 * LANES), 1)
    colr = lax.broadcasted_iota(jnp.int32, (P, 2 * LANES), 1) % LANES
    sel_gb = [jnp.where(row2 == jnp.where(col2 < LANES, LANE_A + h, LANE_B + h), 1.0, 0.0).astype(BF16) for h in heads]
    sel_gr = [jnp.where(colr == LANE_A + h, 1.0, 0.0).astype(BF16) for h in heads]

    def prep(it, between=lambda k: None):
        it = jnp.asarray(it, jnp.int32)
        chains = [(it * _GDN_UNROLL + u, hh) for u in range(_GDN_UNROLL) for hh in range(_GDN_HB)]
        r0 = [pl.multiple_of(p * P, P) for p, _ in chains]
        ln = [slice(hh * LANES, (hh + 1) * LANES) for _, hh in chains]
        hh_ = [hh for _, hh in chains]
        n = len(chains)
        q = [q_ref[pl.ds(r0[i], P), ln[i]].astype(F32) for i in range(n)]
        k = [k_ref[pl.ds(r0[i], P), ln[i]].astype(F32) for i in range(n)]
        v = [v_ref[pl.ds(r0[i], P), ln[i]].astype(F32) for i in range(n)]
        ga = [ga_ref[pl.ds(r0[i], P), :] for i in range(n)]
        gah = [x.astype(BF16) for x in ga]
        gam = [(ga[i] - gah[i].astype(F32)).astype(BF16) for i in range(n)]
        ga2 = [jnp.concatenate([gah[i], gam[i]], axis=1) for i in range(n)]
        gb = [_dot(ga2[i], sel_gb[hh_[i]]) for i in range(n)]
        gcol = [x[:, :LANES] for x in gb]
        bcol = [x[:, LANES:] for x in gb]
        grow = [_dot(sel_gr[hh_[i]], ga2[i], _NT) for i in range(n)]
        kb = [x.astype(BF16) for x in k]
        kk = [_dot(x, x, _NT) for x in kb]
        qkr = [_dot(q[i].astype(BF16), kb[i], _NT) for i in range(n)]
        between(0)
        glast = [jnp.where(first_half, g[C - 1:C, :], g[P - 1:P, :]) for g in gcol]
        decay = [jnp.where(causal, jnp.exp(jnp.where(causal, gcol[i] - grow[i], 0.0)), 0.0) for i in range(n)]
        low = [jnp.where(strict, bcol[i] * kk[i] * decay[i], 0.0) for i in range(n)]
        tinv = _inv_unit_lower(low, blk16, eye, between)
        eg = [jnp.exp(g) for g in gcol]
        rhs = [jnp.concatenate([v[i] * bcol[i], k[i] * (bcol[i] * eg[i])], axis=1) for i in range(n)]
        solb = [_dot3(tinv[i], rhs[i]).astype(BF16) for i in range(n)]
        between(3)
        qw = [_dot((qkr[i] * decay[i]).astype(BF16), solb[i]) for i in range(n)]
        ktail = [(k[i] * jnp.exp(glast[i] - gcol[i])).astype(BF16) for i in range(n)]
        for i, (p, hh) in enumerate(chains):
            qp_s[hh, pl.ds(r0[i], P), :] = (q[i] * eg[i] - qw[i][:, LANES:]).astype(BF16)
            oi_s[hh, pl.ds(r0[i], P), :] = qw[i][:, :LANES]
            for half in range(2):
                rows = slice(half * C, (half + 1) * C)
                c = 2 * p + half
                kwc = _dot(ktail[i][rows], solb[i][rows], _TN)
                c_s[hh, c] = kwc[:, :LANES]
                kw_s[hh, c] = kwc[:, LANES:].astype(BF16)
                a_s[hh, c] = jnp.broadcast_to(jnp.exp(glast[i][half * C:half * C + 1, :]), (8, LANES))

    def step(c, states):
        r0 = pl.multiple_of(c * C, C)
        hs = range(_GDN_HB)
        sb = [states[hh].astype(BF16) for hh in hs]
        ks = [_dot(kw_s[hh, c], sb[hh]) for hh in hs]
        os_ = [_dot(qp_s[hh, pl.ds(r0, C), :], sb[hh]) for hh in hs]
        new = [a_s[hh, c][0:1, :] * states[hh] - ks[hh] + c_s[hh, c] for hh in hs]
        for hh in hs:
            lanes = slice(hh * LANES, (hh + 1) * LANES)
            o = os_[hh] + oi_s[hh, pl.ds(r0, C), :]
            on = o * lax.rsqrt(jnp.mean(o * o, axis=-1, keepdims=True) + EPS) * gn_ref[...]
            z = z_ref[pl.ds(r0, C), lanes].astype(F32)
            o_ref[pl.ds(r0, C), lanes] = (on * _silu(z)).astype(o_ref.dtype)
        return tuple(new)

    per_it = 2 * _GDN_UNROLL
    n_it = S // (P * _GDN_UNROLL)

    def fused(it, states):
        st = [states]

        def between(k):
            st[0] = step((it - 1) * per_it + k, st[0])

        prep(it, between)
        return st[0]

    prep(0)
    states = lax.fori_loop(1, n_it, fused, tuple(jnp.zeros((GDN_DK, LANES), F32) for _ in range(_GDN_HB)))
    for k in range(per_it):
        states = step((n_it - 1) * per_it + k, states)


def _gdn(qkvn, gact, proj, gnorm, B, S):
    T = B * S
    nc = S // GDN_CHUNK
    hb = _GDN_HB
    ng = GDN_HEADS // hb
    w = hb * LANES
    blk = lambda off: pl.BlockSpec((S, w), lambda b, h: (b, off + h))
    return pl.pallas_call(
        functools.partial(_gdn_kernel, S=S),
        grid=(B, ng),
        in_specs=[
            blk(0), blk(ng), blk(2 * ng),
            pl.BlockSpec((S, LANES), lambda b, h: (b, 0)),
            blk(COL_GZ // w),
            pl.BlockSpec((1, LANES), lambda b, h: (0, 0)),
        ],
        out_specs=pl.BlockSpec((S, w), lambda b, h: (b, h)),
        out_shape=jax.ShapeDtypeStruct((T, GDN_HEADS * LANES), BF16),
        scratch_shapes=[
            pltpu.VMEM((hb, nc, GDN_DK, LANES), BF16),
            pltpu.VMEM((hb, nc, GDN_DK, LANES), F32),
            pltpu.VMEM((hb, S, LANES), BF16),
            pltpu.VMEM((hb, S, LANES), F32),
            pltpu.VMEM((hb, nc, 8, LANES), F32),
        ],
        compiler_params=_cparams(("parallel", "parallel"), VMEM_LIMIT_GDN),
        name="gdn",
    )(qkvn, qkvn, qkvn, gact, proj, gnorm)


AUG0 = FOX_HD
_NPAIR = FOX_HEADS * FOX_HD // LANES


def _fox_aug_selectors():
    import numpy as np
    sel = np.zeros((2 * _NPAIR, 3 * LANES, 2 * LANES), np.float32)
    for j in range(2 * _NPAIR):
        is_q = j < _NPAIR
        for slot in range(2):
            head = 2 * (j % _NPAIR) + slot
            c_lane, one_lane, sign = (AUG0, AUG0 + 3, 1.0) if is_q else (AUG0 + 3, AUG0, -1.0)
            for piece in range(3):
                sel[j, piece * LANES + LANE_F + head, slot * LANES + c_lane + piece] = sign
                sel[j, LANE_ONE, slot * LANES + one_lane + piece] = 1.0
    return jnp.asarray(sel, BF16)


def _fox_prep_kernel(x_ref, ga_ref, g_ref, sel_ref, o_ref):
    is_q = pl.program_id(1) < _NPAIR
    x = x_ref[...].astype(F32)
    row = lax.broadcasted_iota(jnp.int32, (LANES, LANES), 0)
    col = lax.broadcasted_iota(jnp.int32, (LANES, LANES), 1)
    grp = jnp.where(row // FOX_HD == col // FOX_HD, 1.0, 0.0).astype(BF16)
    xx = x * x
    hi = xx.astype(BF16)
    lo = (xx - hi.astype(F32)).astype(BF16)
    ms = _dot(jnp.concatenate([hi, lo], axis=1), jnp.concatenate([grp, grp], axis=0)) * (1.0 / FOX_HD)
    scale = jnp.where(is_q, LOG2E * FOX_HD ** -0.5, 1.0).astype(F32)
    xn = x * lax.rsqrt(ms + EPS) * (g_ref[0] * scale)
    xr = pltpu.roll(xn, FOX_HD, axis=1)
    aug = _dot(jnp.concatenate(_split3(ga_ref[...]), axis=1), sel_ref[0])
    lane = lax.broadcasted_iota(jnp.int32, x.shape, 1)
    o_ref[:, :LANES] = jnp.where(lane < FOX_HD, xn, aug[:, :LANES]).astype(o_ref.dtype)
    o_ref[:, LANES:] = jnp.where(lane < FOX_HD, xr, aug[:, LANES:]).astype(o_ref.dtype)


def _fox_prep(proj, gact, gains, T):
    tm = min(2048, T)
    ncol = 2 * _NPAIR
    return pl.pallas_call(
        _fox_prep_kernel,
        grid=(T // tm, ncol),
        in_specs=[
            pl.BlockSpec((tm, LANES), lambda i, j: (i, COL_FQKV // LANES + j)),
            pl.BlockSpec((tm, LANES), lambda i, j: (i, 0)),
            pl.BlockSpec((1, 1, LANES), lambda i, j: (j, 0, 0)),
            pl.BlockSpec((1, 3 * LANES, 2 * LANES), lambda i, j: (j, 0, 0)),
        ],
        out_specs=pl.BlockSpec((tm, 2 * LANES), lambda i, j: (i, j)),
        out_shape=jax.ShapeDtypeStruct((T, ncol * 2 * LANES), BF16),
        compiler_params=_cparams(("parallel", "arbitrary")),
        name="fox_prep",
    )(proj, gact, gains, _fox_aug_selectors())


def _vt_kernel(v_ref, o_ref):
    o_ref[...] = jnp.transpose(v_ref[...].astype(F32)).astype(o_ref.dtype)


def _fox_vt(proj, T):
    tm = min(2048, T)
    nhp = FOX_HEADS // 2
    return pl.pallas_call(
        _vt_kernel,
        grid=(T // tm, nhp),
        in_specs=[pl.BlockSpec((tm, LANES), lambda i, h: (i, COL_FQKV // LANES + 2 * nhp + h))],
        out_specs=pl.BlockSpec((LANES, tm), lambda i, h: (h, i)),
        out_shape=jax.ShapeDtypeStruct((nhp * LANES, T), BF16),
        compiler_params=_cparams(("parallel", "parallel")),
        name="fox_vt",
    )(proj)


def _fox_kernel(q_ref, k_ref, vt_ref, o_ref, *, tq):
    i = pl.program_id(2)
    hs = range(2)
    q = [q_ref[:, hh * LANES:(hh + 1) * LANES] for hh in hs]
    th = tq // 2
    keyi = lax.broadcasted_iota(jnp.int32, (th, th), 0)
    qryi = lax.broadcasted_iota(jnp.int32, (th, th), 1)
    tri = keyi <= qryi

    def kv_step(r0, nk, q0, carry, masked):
        ms, ls, accs = carry
        vt = [vt_ref[hh * FOX_HD:(hh + 1) * FOX_HD, pl.ds(r0, nk)] for hh in hs]
        s = [_dot(k_ref[pl.ds(r0, nk), hh * LANES:(hh + 1) * LANES], q[hh][q0:, :], _NT) for hh in hs]
        if masked:
            s = [jnp.where(tri, x, -jnp.inf) if x.shape[1] == th else
                 jnp.concatenate([jnp.where(tri, x[:, :th], -jnp.inf), x[:, th:]], axis=1) for x in s]
        m_old = [ms[hh][:, q0:] for hh in hs]
        m_new = [jnp.maximum(m_old[hh], jnp.max(s[hh], axis=0, keepdims=True)) for hh in hs]
        pr = [jnp.exp2(s[hh] - m_new[hh]) for hh in hs]
        alpha = [jnp.exp2(m_old[hh] - m_new[hh]) for hh in hs]
        pv = [_dot(vt[hh], pr[hh].astype(BF16)) for hh in hs]
        l_new = [alpha[hh] * ls[hh][:, q0:] + jnp.sum(pr[hh], axis=0, keepdims=True) for hh in hs]
        a_new = [alpha[hh] * accs[hh][:, q0:] + pv[hh] for hh in hs]
        keep = lambda old, new: [jnp.concatenate([old[hh][:, :q0], new[hh]], axis=1) if q0 else new[hh] for hh in hs]
        return keep(ms, m_new), keep(ls, l_new), keep(accs, a_new)

    init = ([jnp.full((1, tq), -jnp.inf, F32) for _ in hs], [jnp.zeros((1, tq), F32) for _ in hs],
            [jnp.zeros((FOX_HD, tq), F32) for _ in hs])
    carry = lax.fori_loop(0, i, lambda jb, c: kv_step(pl.multiple_of(jb * tq, tq), tq, 0, c, False), init)
    d0 = pl.multiple_of(i * tq, tq)
    carry = kv_step(d0, th, 0, carry, True)
    _, ls, accs = kv_step(pl.multiple_of(d0 + th, th), th, th, carry, True)
    ot = jnp.concatenate([accs[hh] / ls[hh] for hh in hs], axis=0)
    o_ref[...] = jnp.transpose(ot).astype(o_ref.dtype)


def _fox(qka, vt, B, S):
    T = B * S
    tq = min(1024, S)
    nq = S // tq
    nhp = FOX_HEADS // 2
    return pl.pallas_call(
        functools.partial(_fox_kernel, tq=tq),
        grid=(B, nhp, nq),
        in_specs=[
            pl.BlockSpec((tq, 2 * LANES), lambda b, h, i: (b * nq + i, h)),
            pl.BlockSpec((S, 2 * LANES), lambda b, h, i: (b, nhp + h)),
            pl.BlockSpec((LANES, S), lambda b, h, i: (h, b)),
        ],
        out_specs=pl.BlockSpec((tq, LANES), lambda b, h, i: (b * nq + i, h)),
        out_shape=jax.ShapeDtypeStruct((T, nhp * LANES), BF16),
        compiler_params=_cparams(("parallel", "parallel", "arbitrary")),
        name="fox",
    )(qka, qka, vt)


HALF = D_MODEL // 2


def _rms(x, gain):
    return x * lax.rsqrt(jnp.mean(x * x, axis=-1, keepdims=True) + EPS) * gain


def _pack_halves(x):
    lo = lax.bitcast_convert_type(x[:, :HALF].astype(BF16).astype(F32), jnp.uint32)
    hi = lax.bitcast_convert_type(x[:, HALF:].astype(BF16).astype(F32), jnp.uint32)
    return lax.shift_right_logical(lo, jnp.uint32(16)) | (hi & jnp.uint32(0xFFFF0000))


def _unpack_halves(w):
    lo = lax.bitcast_convert_type(lax.shift_left(w, jnp.uint32(16)), F32)
    hi = lax.bitcast_convert_type(w & jnp.uint32(0xFFFF0000), F32)
    return lo, hi


def _mix_kernel(oa_ref, of_ref, ma_ref, mf_ref, x_ref, wa_ref, wf_ref, wo_ref, g_ref, h_ref, hp_ref):
    ya = _dot(oa_ref[...], wa_ref[...])
    yf = _dot(of_ref[...], wf_ref[...])
    m = _sigmoid(ma_ref[...].astype(F32)) * ya + _sigmoid(mf_ref[...].astype(F32)) * yf
    h = x_ref[...] + _dot(m.astype(BF16), wo_ref[...])
    h_ref[...] = h
    hp_ref[...] = _pack_halves(_rms(h, g_ref[...]))


def _mix(o_a, o_f, proj, x2, wa, wf, wo, gain):
    T = x2.shape[0]
    tm = min(512, T)
    row = lambda c: pl.BlockSpec((tm, D_MODEL), lambda i: (i, c))
    full = pl.BlockSpec((D_MODEL, D_MODEL), lambda i: (0, 0))
    return pl.pallas_call(
        _mix_kernel,
        grid=(T // tm,),
        in_specs=[row(0), row(0), row(COL_MERGE // D_MODEL), row(COL_MERGE // D_MODEL + 1), row(0),
                  full, full, full, pl.BlockSpec((1, D_MODEL), lambda i: (0, 0))],
        out_specs=[row(0), pl.BlockSpec((tm, HALF), lambda i: (i, 0))],
        out_shape=[jax.ShapeDtypeStruct((T, D_MODEL), F32), jax.ShapeDtypeStruct((T, HALF), jnp.uint32)],
        compiler_params=_cparams(("parallel",)),
        name="mix_out",
    )(o_a, o_f, proj, proj, x2, wa, wf, wo, gain)


def _router_kernel(h_ref, g_ref, wr_ref, bias_ref, gates_ref, sel_ref, cnt_ref):
    @pl.when(pl.program_id(0) == 0)
    def _():
        cnt_ref[...] = jnp.zeros_like(cnt_ref)

    hn = _rms(h_ref[...], g_ref[...])
    hh = hn.astype(BF16)
    hl = (hn - hh.astype(F32)).astype(BF16)
    logits = _dot(wr_ref[0], hh, _NT) + _dot(wr_ref[0], hl, _NT) + _dot(wr_ref[1], hh, _NT)
    scores = _sigmoid(logits)
    work = scores + bias_ref[...]
    eidx = lax.broadcasted_iota(jnp.int32, work.shape, 0)
    sel = jnp.zeros(work.shape, F32)
    for _ in range(TOP_K):
        mx = jnp.max(work, axis=0, keepdims=True)
        first = jnp.min(jnp.where(work == mx, eidx, N_EXPERTS), axis=0, keepdims=True)
        onehot = eidx == first
        sel = jnp.where(onehot, 1.0, sel)
        work = jnp.where(onehot, -jnp.inf, work)
    s = jnp.where(sel > 0.0, scores, 0.0)
    gates_ref[...] = s / jnp.sum(s, axis=0, keepdims=True) * ROUTED_SCALE
    sel_ref[...] = sel
    tot = jnp.zeros((N_EXPERTS, 1), F32)
    for t0 in range(0, sel.shape[1], MOE_TT):
        n = jnp.sum(sel[:, t0:t0 + MOE_TT], axis=1, keepdims=True)
        tot = tot + jnp.floor((n + (SUB - 1)) * (1.0 / SUB)) * SUB
    cnt_ref[...] += jnp.broadcast_to(tot, cnt_ref.shape)


def _router(h1, gain, wr2, bias_col):
    T = h1.shape[0]
    tm = min(2 * MOE_TT, T)
    return pl.pallas_call(
        _router_kernel,
        grid=(T // tm,),
        in_specs=[
            pl.BlockSpec((tm, D_MODEL), lambda i: (i, 0)),
            pl.BlockSpec((1, D_MODEL), lambda i: (0, 0)),
            pl.BlockSpec((2, N_EXPERTS, D_MODEL), lambda i: (0, 0, 0)),
            pl.BlockSpec((N_EXPERTS, 1), lambda i: (0, 0)),
        ],
        out_specs=[
            pl.BlockSpec((N_EXPERTS, tm), lambda i: (0, i)),
            pl.BlockSpec((N_EXPERTS, tm), lambda i: (0, i)),
            pl.BlockSpec((N_EXPERTS, LANES), lambda i: (0, 0)),
        ],
        out_shape=[
            jax.ShapeDtypeStruct((N_EXPERTS, T), F32),
            jax.ShapeDtypeStruct((N_EXPERTS, T), F32),
            jax.ShapeDtypeStruct((N_EXPERTS, LANES), F32),
        ],
        compiler_params=_cparams(("arbitrary",)),
        name="router",
    )(h1, gain, wr2, bias_col)


MOE_TT = 256
SUB = 8
MOE_R = MOE_TT * TOP_K + N_EXPERTS * SUB


def _pos_kernel(sel_ref, gates_ref, cnt_ref, lpos_ref, w_ref, tab_ref, base_s, carry_s, *, tt, tme):
    r = lax.broadcasted_iota(jnp.int32, (N_EXPERTS, N_EXPERTS), 0)
    c = lax.broadcasted_iota(jnp.int32, (N_EXPERTS, N_EXPERTS), 1)
    below = jnp.where(r > c, 1.0, 0.0).astype(BF16)

    @pl.when(pl.program_id(0) == 0)
    def _():
        padded = jnp.floor((cnt_ref[...] + (tme - 1)) * (1.0 / tme)) * tme
        base_s[...] = _dot_lx(below, padded)
        carry_s[...] = jnp.zeros_like(carry_s)

    sel = sel_ref[...]
    gates = gates_ref[...]
    rr = lax.broadcasted_iota(jnp.int32, (tt, tt), 0)
    cc = lax.broadcasted_iota(jnp.int32, (tt, tt), 1)
    cs = _dot(sel.astype(BF16), jnp.where(rr <= cc, 1.0, 0.0).astype(BF16))
    n = jnp.sum(sel, axis=1, keepdims=True)
    npad = jnp.broadcast_to(jnp.floor((n + (SUB - 1)) * (1.0 / SUB)) * SUB, (N_EXPERTS, LANES))
    loff = _dot_lx(below, npad)
    gstart = base_s[...] + carry_s[...]
    carry_s[...] += npad

    wide = lambda v: jnp.concatenate([v] * (TAB_W // LANES), axis=1)
    incl = jnp.where(r >= c, 1.0, 0.0).astype(BF16)
    nfull = jnp.floor(npad * (1.0 / PIECE))
    has8 = npad * (1.0 / SUB) - 2.0 * nfull
    cend = _dot_lx(incl, nfull)
    hend = _dot_lx(incl, has8)
    jl = lax.broadcasted_iota(jnp.int32, (N_EXPERTS, TAB_W), 1).astype(F32)
    ei = lax.broadcasted_iota(jnp.int32, (N_EXPERTS, TAB_W), 0).astype(F32)

    def owner(ends):
        e = jnp.sum(jnp.where(wide(ends) <= jl, 1.0, 0.0), axis=0, keepdims=True)
        return ei == jnp.minimum(e, N_EXPERTS - 1.0)

    take = lambda oh, v: jnp.sum(jnp.where(oh, wide(v), 0.0), axis=0, keepdims=True)
    oh16 = owner(cend)
    oh8 = owner(hend)
    off16 = PIECE * (jl[0:1, :] - take(oh16, cend - nfull))
    off8 = PIECE * take(oh8, nfull)
    last = N_EXPERTS - 1
    lane = lax.broadcasted_iota(jnp.int32, (1, TAB_W), 1)
    misc = jnp.where(lane == 0, wide(cend[last:, :]),
                     jnp.where(lane == 1, wide(hend[last:, :]), wide(loff[last:, :] + npad[last:, :])))
    tab_ref[0] = jnp.concatenate([take(oh16, gstart) + off16, take(oh16, loff) + off16,
                                  take(oh8, gstart) + off8, take(oh8, loff) + off8, misc,
                                  jnp.zeros((3, TAB_W), F32)], axis=0)
    lposd = cs - sel + jnp.concatenate([loff] * (tt // LANES), axis=1)
    eidx = lax.broadcasted_iota(jnp.int32, sel.shape, 0)
    rem = sel
    ps, ws = [], []
    for _ in range(TOP_K):
        first = jnp.min(jnp.where(rem > 0.0, eidx, N_EXPERTS), axis=0, keepdims=True)
        onehot = eidx == first
        ps.append(jnp.sum(jnp.where(onehot, lposd, 0.0), axis=0, keepdims=True))
        ws.append(jnp.sum(jnp.where(onehot, gates, 0.0), axis=0, keepdims=True))
        rem = jnp.where(onehot, 0.0, rem)
    lpos_ref[...] = jnp.concatenate(ps, axis=0).astype(jnp.int32)
    w_ref[...] = jnp.concatenate(ws, axis=0)


def _positions(sel, gates, cnt, tme):
    T = sel.shape[1]
    tt = MOE_TT
    blk = pl.BlockSpec((N_EXPERTS, tt), lambda i: (0, i))
    return pl.pallas_call(
        functools.partial(_pos_kernel, tt=tt, tme=tme),
        grid=(T // tt,),
        in_specs=[blk, blk, pl.BlockSpec((N_EXPERTS, LANES), lambda i: (0, 0))],
        out_specs=[pl.BlockSpec((TOP_K, tt), lambda i: (0, i)), pl.BlockSpec((TOP_K, tt), lambda i: (0, i)),
                   pl.BlockSpec((1, TAB_ROWS, TAB_W), lambda i: (i, 0, 0))],
        out_shape=[jax.ShapeDtypeStruct((TOP_K, T), jnp.int32), jax.ShapeDtypeStruct((TOP_K, T), F32),
                   jax.ShapeDtypeStruct((T // tt, TAB_ROWS, TAB_W), F32)],
        scratch_shapes=[pltpu.VMEM((N_EXPERTS, LANES), F32), pltpu.VMEM((N_EXPERTS, LANES), F32)],
        compiler_params=_cparams(("arbitrary",)),
        name="moe_positions",
    )(sel, gates, cnt)


PIECE = 2 * SUB
TAB_W = 2 * LANES
TAB_ROWS = 8
TAB_G16, TAB_L16, TAB_G8, TAB_L8 = 0, TAB_W, 2 * TAB_W, 3 * TAB_W
TAB_N16, TAB_N8, TAB_USED = 4 * TAB_W, 4 * TAB_W + 1, 4 * TAB_W + 2
TAB_LEN = TAB_ROWS * TAB_W


def _segment_copies(tab_ref, src_of, dst_of, sem):
    def pieces(g0, l0, rows):
        def body(j, carry):
            g = pl.multiple_of(tab_ref[0, 0, g0 + j], SUB)
            l = pl.multiple_of(tab_ref[0, 0, l0 + j], SUB)
            pltpu.make_async_copy(src_of(g, l, rows), dst_of(g, l, rows), sem).start()
            return carry
        return body

    lax.fori_loop(0, tab_ref[0, 0, TAB_N16], pieces(TAB_G16, TAB_L16, PIECE), 0)
    lax.fori_loop(0, tab_ref[0, 0, TAB_N8], pieces(TAB_G8, TAB_L8, SUB), 0)


def _wait_segment_copies(tab_ref, src_of, dst_of, sem):
    used = tab_ref[0, 0, TAB_USED]

    def unit(rows):
        def body(i, carry):
            pltpu.make_async_copy(src_of(0, 0, rows), dst_of(0, 0, rows), sem).wait()
            return carry
        return body

    lax.fori_loop(0, lax.shift_right_logical(used, 7), unit(LANES), 0)
    lax.fori_loop(0, lax.shift_right_logical(used, 3) & (LANES // SUB - 1), unit(SUB), 0)


_ROW_CHUNK = MOE_R // 4


def _pack2(lo, hi):
    lo = lax.bitcast_convert_type(lo.astype(BF16).astype(F32), jnp.uint32)
    hi = lax.bitcast_convert_type(hi.astype(BF16).astype(F32), jnp.uint32)
    return lax.shift_right_logical(lo, jnp.uint32(16)) | (hi & jnp.uint32(0xFFFF0000))


def _dispatch_kernel(meta_ref, seg_ref, prev_ref, lpos_ref, hp_ref, xs_ref, zero_s, sort_s, sem, zsem, *,
                     tt, tme, n_tiles):
    i = pl.program_id(0)
    slot = i % 2

    @pl.when(i == 0)
    def _():
        zero_s[...] = jnp.zeros_like(zero_s)
        for e in range(N_EXPERTS):
            pltpu.make_async_copy(zero_s, xs_ref.at[pl.ds(pl.multiple_of(meta_ref[e], SUB), tme)], zsem).start()
        for e in range(N_EXPERTS):
            pltpu.make_async_copy(zero_s, xs_ref.at[pl.ds(pl.multiple_of(meta_ref[e], SUB), tme)], zsem).wait()

        def zero_tile(t, carry):
            cp = pltpu.make_async_copy(zero_s, xs_ref.at[pl.ds(pl.multiple_of(t * tme, tme), tme)], zsem)
            cp.start()
            cp.wait()
            return carry

        lax.fori_loop(meta_ref[N_EXPERTS], n_tiles, zero_tile, 0)

    lo, hi = _unpack_halves(hp_ref[...])
    lo = lo.astype(BF16)
    hi = hi.astype(BF16)
    lp = lpos_ref[...].astype(jnp.int16)
    one = jnp.ones((1, 1), BF16)
    for rc in range(MOE_R // _ROW_CHUNK):
        rowi = (lax.broadcasted_iota(jnp.int32, (_ROW_CHUNK, tt), 0) + rc * _ROW_CHUNK).astype(jnp.int16)
        pm = jnp.zeros((_ROW_CHUNK, tt), BF16)
        for k in range(TOP_K):
            pm = jnp.where(rowi == lp[k:k + 1, :], one, pm)
        sort_s[slot, rc * _ROW_CHUNK:(rc + 1) * _ROW_CHUNK, :] = _pack2(_dot(pm, lo), _dot(pm, hi))

    src_of = lambda s: (lambda g, l, rows: sort_s.at[s, pl.ds(l, rows)])
    dst_of = lambda g, l, rows: xs_ref.at[pl.ds(g, rows)]

    @pl.when(i > 0)
    def _():
        _wait_segment_copies(prev_ref, src_of(1 - slot), dst_of, sem.at[1 - slot])

    _segment_copies(seg_ref, src_of(slot), dst_of, sem.at[slot])

    @pl.when(i == pl.num_programs(0) - 1)
    def _():
        _wait_segment_copies(seg_ref, src_of(slot), dst_of, sem.at[slot])


def _dispatch(meta, seg, lpos, hp, n_tiles, tme):
    T = hp.shape[0]
    tt = MOE_TT
    return pl.pallas_call(
        functools.partial(_dispatch_kernel, tt=tt, tme=tme, n_tiles=n_tiles),
        grid_spec=pltpu.PrefetchScalarGridSpec(
            num_scalar_prefetch=1,
            grid=(T // tt,),
            in_specs=[
                pl.BlockSpec((1, 1, TAB_LEN), lambda i, f: (i, 0, 0), memory_space=pltpu.SMEM),
                pl.BlockSpec((1, 1, TAB_LEN), lambda i, f: (jnp.maximum(i - 1, 0), 0, 0), memory_space=pltpu.SMEM),
                pl.BlockSpec((TOP_K, tt), lambda i, f: (0, i)),
                pl.BlockSpec((tt, HALF), lambda i, f: (i, 0)),
            ],
            out_specs=pl.BlockSpec(memory_space=pl.ANY),
            scratch_shapes=[pltpu.VMEM((tme, HALF), jnp.uint32), pltpu.VMEM((2, MOE_R, HALF), jnp.uint32),
                            pltpu.SemaphoreType.DMA((2,)), pltpu.SemaphoreType.DMA],
        ),
        out_shape=jax.ShapeDtypeStruct((n_tiles * tme, HALF), jnp.uint32),
        compiler_params=_cparams(("arbitrary",)),
        name="moe_dispatch",
    )(meta, seg, seg, lpos, hp)


def _expert_kernel(te_ref, nu_ref, xs_ref, wgu_ref, wd_ref, ys_ref):
    @pl.when(pl.program_id(0) < nu_ref[0])
    def _():
        lo, hi = _unpack_halves(xs_ref[...])
        gu = _dot(lo.astype(BF16), wgu_ref[0, :HALF, :]) + _dot(hi.astype(BF16), wgu_ref[0, HALF:, :])
        hid = _silu(gu[:, :D_EXPERT]) * gu[:, D_EXPERT:]
        ys_ref[...] = _pack_halves(_dot(hid.astype(BF16), wd_ref[0]))

    @pl.when(pl.program_id(0) >= nu_ref[0])
    def _():
        ys_ref[...] = jnp.zeros_like(ys_ref)


def _experts(tile_e, n_used, xs, wgu, wd, n_tiles, tme):
    clamp = lambda i, te, nu: (jnp.minimum(i, nu[0] - 1), 0)
    return pl.pallas_call(
        _expert_kernel,
        grid_spec=pltpu.PrefetchScalarGridSpec(
            num_scalar_prefetch=2,
            grid=(n_tiles,),
            in_specs=[
                pl.BlockSpec((tme, HALF), clamp),
                pl.BlockSpec((1, D_MODEL, 2 * D_EXPERT), lambda i, te, nu: (te[i], 0, 0)),
                pl.BlockSpec((1, D_EXPERT, D_MODEL), lambda i, te, nu: (te[i], 0, 0)),
            ],
            out_specs=pl.BlockSpec((tme, HALF), lambda i, te, nu: (i, 0)),
        ),
        out_shape=jax.ShapeDtypeStruct((n_tiles * tme, HALF), jnp.uint32),
        compiler_params=_cparams(("arbitrary",)),
        name="moe_experts",
    )(tile_e, n_used, xs, wgu, wd)


def _tail_kernel(seg_ref, next_ref, ys_ref, lpos_ref, w_ref, h_ref, p_ref, gf_ref, wsgu_ref, wsd_ref, wple_ref,
                 gple_ref, gpg_ref, wpg_ref, o_ref, buf, sem, *, tt):
    i = pl.program_id(0)
    slot = i % 2
    src_of = lambda g, l, rows: ys_ref.at[pl.ds(g, rows)]
    dst_of = lambda s: (lambda g, l, rows: buf.at[s, pl.ds(l, rows)])

    def fetch(tab_ref, s):
        _segment_copies(tab_ref, src_of, dst_of(s), sem.at[s])
        def clear(j, carry):
            buf[s, pl.ds(pl.multiple_of(j * SUB, SUB), SUB), :] = jnp.zeros((SUB, HALF), jnp.uint32)
            return carry

        lax.fori_loop(lax.shift_right_logical(tab_ref[0, 0, TAB_USED], 3), MOE_R // SUB, clear, 0)

    @pl.when(i == 0)
    def _():
        fetch(seg_ref, slot)

    @pl.when(i + 1 < pl.num_programs(0))
    def _():
        fetch(next_ref, 1 - slot)

    h = h_ref[...]
    hn = _rms(h, gf_ref[...]).astype(BF16)
    sgu = _dot(hn, wsgu_ref[...])
    shared = _dot((_silu(sgu[:, :D_SHARED]) * sgu[:, D_SHARED:]).astype(BF16), wsd_ref[...])
    e = _rms(_dot(p_ref[...].astype(BF16), wple_ref[...]), gple_ref[...])
    _wait_segment_copies(seg_ref, src_of, dst_of(slot), sem.at[slot])

    lp = lpos_ref[...].astype(jnp.int16)
    wt = w_ref[...].astype(BF16)
    acc_lo = jnp.zeros((tt, HALF), F32)
    acc_hi = jnp.zeros((tt, HALF), F32)
    for rc in range(MOE_R // _ROW_CHUNK):
        coli = (lax.broadcasted_iota(jnp.int32, (tt, _ROW_CHUNK), 1) + rc * _ROW_CHUNK).astype(jnp.int16)
        g = jnp.zeros((tt, _ROW_CHUNK), BF16)
        for k in range(TOP_K):
            g = jnp.where(coli == lp[:, k:k + 1], wt[:, k:k + 1], g)
        lo, hi = _unpack_halves(buf[slot, rc * _ROW_CHUNK:(rc + 1) * _ROW_CHUNK, :])
        acc_lo = acc_lo + _dot(g, lo.astype(BF16))
        acc_hi = acc_hi + _dot(g, hi.astype(BF16))
    h2 = h + jnp.concatenate([acc_lo, acc_hi], axis=1) + shared
    gte = _sigmoid(_dot(_rms(h2, gpg_ref[...]).astype(BF16), wpg_ref[...]))
    o_ref[...] = h2 + gte * e


def _tail(seg, ys, lpos_tok, w_tok, h1, p2, gf, wsgu, wsd, wple, gple, gpg, wpg):
    T = h1.shape[0]
    tt = MOE_TT
    row = lambda n: pl.BlockSpec((tt, n), lambda i: (i, 0))
    full = lambda a: pl.BlockSpec(a.shape, lambda i: (0,) * a.ndim)
    return pl.pallas_call(
        functools.partial(_tail_kernel, tt=tt),
        grid=(T // tt,),
        in_specs=[
            pl.BlockSpec((1, 1, TAB_LEN), lambda i: (i, 0, 0), memory_space=pltpu.SMEM),
            pl.BlockSpec((1, 1, TAB_LEN), lambda i: (jnp.minimum(i + 1, T // tt - 1), 0, 0), memory_space=pltpu.SMEM),
            pl.BlockSpec(memory_space=pl.ANY),
            row(TOP_K), row(TOP_K), row(D_MODEL), row(PLE_DIM),
            full(gf), full(wsgu), full(wsd), full(wple), full(gple), full(gpg), full(wpg),
        ],
        out_specs=row(D_MODEL),
        out_shape=jax.ShapeDtypeStruct((T, D_MODEL), F32),
        scratch_shapes=[pltpu.VMEM((2, MOE_R, HALF), jnp.uint32), pltpu.SemaphoreType.DMA((2,))],
        compiler_params=_cparams(("arbitrary",)),
        name="moe_combine_ple",
    )(seg, seg, ys, lpos_tok, w_tok, h1, p2, gf, wsgu, wsd, wple, gple, gpg, wpg)


def _stages(x, p, norm_mix, w_in, conv_w, a_log, dt_bias, gdn_norm, fox_f_bias, q_norm, k_norm,
            w_branch_gdn, w_branch_fox, w_out, norm_ffn, w_router, router_bias,
            we_gate, we_up, we_down, ws_gate, ws_up, ws_down, w_ple, ple_norm, ple_gate_norm, w_ple_gate):
    B, S, _ = x.shape
    T = B * S
    st = {}
    wi = w_in[0]
    o0 = 3 * GDN_HEADS * GDN_DK
    o1 = o0 + GDN_HEADS * GDN_DK
    o2 = o1 + GDN_HEADS
    o3 = o2 + GDN_HEADS
    o4 = o3 + 3 * FOX_HEADS * FOX_HD
    o5 = o4 + FOX_HEADS
    w_big = jnp.concatenate([wi[:, :o1], wi[:, o3:o4], wi[:, o5:]], axis=1).astype(BF16)
    w_small = jnp.concatenate([wi[:, o1:o3], wi[:, o4:o5],
                               jnp.zeros((D_MODEL, LANES - 2 * GDN_HEADS - FOX_HEADS), F32)], axis=1).astype(BF16)
    proj, small = _in_proj(x.reshape(T, D_MODEL), norm_mix[0].reshape(1, D_MODEL), w_big, w_small)
    st["proj_big"] = proj
    st["small"] = small

    pad = lambda v, off: jnp.zeros((1, LANES), F32).at[0, off:off + v.shape[0]].set(v)
    alog_row = pad(a_log[0], LANE_A)
    bias_row = pad(dt_bias[0], LANE_A) + pad(fox_f_bias[0], LANE_F)
    gact = _gates(small, alog_row, bias_row, B, S)
    st["gact"] = gact

    qkvn = _gdn_prep(proj, conv_w[0], B, S)
    st["qkvn"] = qkvn
    o_a = _gdn(qkvn, gact, proj, gdn_norm[0].reshape(1, LANES), B, S)
    st["o_a"] = o_a

    nrep = FOX_HEADS * FOX_HD // LANES
    gains = jnp.concatenate([jnp.tile(q_norm[0], (nrep, LANES // FOX_HD)),
                             jnp.tile(k_norm[0], (nrep, LANES // FOX_HD))], axis=0).reshape(2 * nrep, 1, LANES)
    qka = _fox_prep(proj, gact, gains, T)
    o_f = _fox(qka, _fox_vt(proj, T), B, S)
    st["o_f"] = o_f

    x2 = x.reshape(T, D_MODEL)
    gf = norm_ffn[0].reshape(1, D_MODEL)
    h1, hp = _mix(o_a, o_f, proj, x2, w_branch_gdn[0].astype(BF16), w_branch_fox[0].astype(BF16),
                  w_out[0].astype(BF16), gf)
    st["h1"] = h1

    wrt = w_router[0].T
    wr_hi = wrt.astype(BF16)
    wr2 = jnp.stack([wr_hi, (wrt - wr_hi.astype(F32)).astype(BF16)])
    gates, sel, cnt = _router(h1, gf, wr2, router_bias[0].reshape(N_EXPERTS, 1))
    st["gates_t"] = gates

    tme = 512 if T * TOP_K // N_EXPERTS >= 2048 else 64
    lpos, wts, segf = _positions(sel, gates, cnt, tme)
    nt_tok = T // MOE_TT
    seg = segf.astype(jnp.int32).reshape(nt_tok, 1, TAB_LEN)
    cnt1 = cnt[:, 0].astype(jnp.int32)
    padded = (cnt1 + (tme - 1)) // tme * tme
    ends = jnp.cumsum(padded)
    n_tiles = (T * TOP_K + (SUB - 1) * N_EXPERTS * nt_tok) // tme + N_EXPERTS
    tile_start = jnp.arange(n_tiles, dtype=jnp.int32) * tme
    tile_e = jnp.minimum(jnp.sum((ends[None, :] <= tile_start[:, None]).astype(jnp.int32), axis=1), N_EXPERTS - 1)
    n_used = (ends[-1:] // tme).astype(jnp.int32)
    fill = (ends - padded + cnt1).astype(jnp.int32)
    xs = _dispatch(jnp.concatenate([fill, n_used]), seg, lpos, hp, n_tiles, tme)
    wgu = jnp.concatenate([we_gate[0], we_up[0]], axis=2).astype(BF16)
    ys = _experts(tile_e, n_used, xs, wgu, we_down[0].astype(BF16), n_tiles, tme)

    row = lambda v: v.reshape(1, D_MODEL)
    wsgu = jnp.concatenate([ws_gate[0], ws_up[0]], axis=1).astype(BF16)
    out = _tail(seg, ys, lpos.T, wts.T, h1, p[0].reshape(T, PLE_DIM), gf, wsgu, ws_down[0].astype(BF16),
                w_ple[0].astype(BF16), row(ple_norm[0]), row(ple_gate_norm[0]), w_ple_gate[0].astype(BF16))
    st["out"] = out.reshape(B, S, D_MODEL)
    return st


def kernel(x, p, norm_mix, w_in, conv_w, a_log, dt_bias, gdn_norm, fox_f_bias, q_norm, k_norm, w_branch_gdn, w_branch_fox, w_out, norm_ffn, w_router, router_bias, we_gate, we_up, we_down, ws_gate, ws_up, ws_down, w_ple, ple_norm, ple_gate_norm, w_ple_gate):
    return _stages(x, p, norm_mix, w_in, conv_w, a_log, dt_bias, gdn_norm, fox_f_bias, q_norm, k_norm,
                   w_branch_gdn, w_branch_fox, w_out, norm_ffn, w_router, router_bias,
                   we_gate, we_up, we_down, ws_gate, ws_up, ws_down, w_ple, ple_norm, ple_gate_norm,
                   w_ple_gate)["out"]
```
